```python
import jax, jax.numpy as jnp
from jax import lax
import numpy as np

D_MODEL = 2048
BATCH = 2
SEQ = 4096
DEPTH = 2

RWKV_WIDTH = D_MODEL // 2
HEAD_SIZE = 64
N_HEADS = RWKV_WIDTH // HEAD_SIZE
DECAY_LORA = max(32, int(round(1.8 * D_MODEL ** 0.5 / 32)) * 32)
ICLR_LORA = max(32, int(round(1.8 * D_MODEL ** 0.5 / 32)) * 32)
VRES_LORA = max(32, int(round(1.3 * D_MODEL ** 0.5 / 32)) * 32)
GATE_LORA = max(32, int(round(0.6 * D_MODEL ** 0.8 / 32)) * 32)
GN_EPS = 64e-5
POOL_WIDTH = D_MODEL // 2
POOL_WINDOWS = (2, 4, 8, 16)
POOL_GROUPS = len(POOL_WINDOWS)
POOL_GROUP_WIDTH = POOL_WIDTH // POOL_GROUPS
SHIFT_WIDTH = 3 * RWKV_WIDTH + DECAY_LORA + ICLR_LORA + GATE_LORA
IN_WIDTH = SHIFT_WIDTH + POOL_WIDTH + 2 * D_MODEL
D_FF_DENSE = 256 * ((8 * D_MODEL // 3 + 255) // 256)
N_EXPERTS = 8
TOP_K = 2
D_FF_EXPERT = 7 * D_MODEL // 2
MOE_BLOCK = 128
N_DENSE = (DEPTH + 1) // 2
N_MOE = DEPTH // 2
NORM_EPS = 1e-6

kernel_name = "rwkv7_pool_gated_hybrid_moe"


def rmsnorm(x, g):
    xf = x.astype(jnp.float32)
    y = xf * lax.rsqrt(jnp.mean(xf * xf, axis=-1, keepdims=True) + NORM_EPS)
    return (y * g.astype(jnp.float32)).astype(x.dtype)


def token_shift(z, mu):
    prev = jnp.pad(z, ((0, 0), (1, 0), (0, 0)))[:, :-1]
    return z + (prev - z) * mu


def wkv7_scan(r, decay, k, v, a, b):
    Bsz, S, H, N = r.shape

    def step(state, inp):
        r_t, w_t, k_t, v_t, a_t, b_t = inp
        sa = jnp.einsum("bhvk,bhk->bhv", state, a_t)
        state = (state * w_t[:, :, None, :] + sa[..., :, None] * b_t[..., None, :]
                 + v_t[..., :, None] * k_t[..., None, :])
        return state, jnp.einsum("bhvk,bhk->bhv", state, r_t)

    xs = tuple(jnp.moveaxis(t, 1, 0) for t in (r, decay, k, v, a, b))
    _, ys = lax.scan(step, jnp.zeros((Bsz, H, N, N), jnp.float32), xs)
    return jnp.moveaxis(ys, 0, 1)


def rwkv7_core(r, k, v, wd, ad, gd, decay_w0, decay_up, iclr_a0, iclr_up, outgate_up,
               k_k, k_a, r_k, lnx_w, lnx_b):
    Bsz, S, _ = r.shape
    f32 = jnp.float32
    heads = lambda t: t.astype(f32).reshape(Bsz, S, N_HEADS, HEAD_SIZE)
    w_log = -jax.nn.softplus(-(decay_w0 + jnp.tanh(wd) @ decay_up).astype(f32)) - 0.5
    decay = jnp.exp(-jnp.exp(w_log))
    a = jax.nn.sigmoid((iclr_a0 + ad @ iclr_up).astype(f32))
    g = jax.nn.sigmoid(gd) @ outgate_up
    kk = heads(k * k_k)
    kk = kk / jnp.maximum(jnp.sqrt(jnp.sum(kk * kk, axis=-1, keepdims=True)), 1e-12)
    k = k.astype(f32) * (1.0 + (a - 1.0) * k_a.astype(f32))
    rh, kh, vh, ah, dh = heads(r), heads(k), heads(v), heads(a), heads(decay)
    y = wkv7_scan(rh, dh, kh, vh, -kk, kk * ah)
    mu = jnp.mean(y, axis=-1, keepdims=True)
    var = jnp.mean(jnp.square(y - mu), axis=-1, keepdims=True)
    y = ((y - mu) * lax.rsqrt(var + GN_EPS)).reshape(Bsz, S, RWKV_WIDTH)
    y = y * lnx_w.astype(f32) + lnx_b.astype(f32)
    bonus = jnp.sum(rh * kh * r_k.astype(f32), axis=-1, keepdims=True) * vh
    y = y + bonus.reshape(Bsz, S, RWKV_WIDTH)
    return (y * g.astype(f32)).astype(r.dtype)


def multiscale_pool(u, pool_w, pool_scale):
    Bsz, S, _ = u.shape
    uf = u.astype(jnp.float32).reshape(Bsz, S, POOL_GROUPS, POOL_GROUP_WIDTH)
    c = jnp.cumsum(uf, axis=1)
    t = jnp.arange(S)
    outs = []
    for gi, w in enumerate(POOL_WINDOWS):
        cg = c[:, :, gi]
        prev = jnp.pad(cg, ((0, 0), (w, 0), (0, 0)))[:, :S]
        count = jnp.minimum(t + 1, w).astype(jnp.float32)[None, :, None]
        outs.append((cg - prev) / count)
    mixed = jnp.stack(outs, axis=2) - uf
    y = jnp.einsum("bsgc,gcd->bsgd", mixed, pool_w.astype(jnp.float32)).reshape(Bsz, S, POOL_WIDTH)
    return (y * pool_scale.astype(jnp.float32)).astype(u.dtype)


def swiglu(h, w_gate, w_up, w_down):
    return (jax.nn.silu(h @ w_gate) * (h @ w_up)) @ w_down


def moe_swiglu(h, w_router, w_gate, w_up, w_down):
    Bsz, S, D = h.shape
    n_tok = Bsz * S
    n_slot = n_tok * TOP_K
    xf = h.reshape(n_tok, D)
    logits = xf.astype(jnp.float32) @ w_router.astype(jnp.float32)
    top_logit, top_e = lax.top_k(logits, TOP_K)
    gate = jax.nn.softmax(top_logit, axis=-1)
    flat_e = top_e.reshape(n_slot)
    order = jnp.argsort(flat_e)
    sorted_e = flat_e[order]
    counts = jnp.bincount(flat_e, length=N_EXPERTS)
    padded = (counts + MOE_BLOCK - 1) // MOE_BLOCK * MOE_BLOCK
    start = jnp.cumsum(counts) - counts
    pend = jnp.cumsum(padded)
    dest = (pend - padded)[sorted_e] + jnp.arange(n_slot) - start[sorted_e]
    n_blocks = -(-n_slot // MOE_BLOCK) + N_EXPERTS
    tok = jnp.full((n_blocks * MOE_BLOCK,), n_tok, jnp.int32).at[dest].set((order // TOP_K).astype(jnp.int32))
    blk_e = jnp.minimum(jnp.searchsorted(pend, jnp.arange(n_blocks) * MOE_BLOCK, side="right"), N_EXPERTS - 1)
    x_pad = jnp.concatenate([xf, jnp.zeros((1, D), xf.dtype)], axis=0)

    def expert_block(args):
        idx, e = args
        xb = x_pad[idx]
        return (jax.nn.silu(xb @ w_gate[e]) * (xb @ w_up[e])) @ w_down[e]

    y_buf = lax.map(expert_block, (tok.reshape(n_blocks, MOE_BLOCK), blk_e)).reshape(-1, D)
    y_slot = jnp.zeros((n_slot, D), y_buf.dtype).at[order].set(y_buf[dest])
    y = jnp.einsum("nkd,nk->nd", y_slot.reshape(n_tok, TOP_K, D), gate.astype(y_slot.dtype))
    return y.reshape(Bsz, S, D)


def setup_inputs(seed: int = 0) -> dict:
    key = jax.random.key(seed)
    keys = iter(jax.random.split(key, 40))
    nrm = lambda shape, scale: jax.random.normal(next(keys), shape, jnp.float32) * scale
    uni = lambda shape, lo, hi: jax.random.uniform(next(keys), shape, jnp.float32, lo, hi)
    L, V = DEPTH, DEPTH - 1
    D, R = D_MODEL, RWKV_WIDTH
    return {
        "x": nrm((BATCH, SEQ, D), 1.0),
        "norm_mix": 1.0 + nrm((L, D), 0.02),
        "w_in": nrm((L, D, IN_WIDTH), D ** -0.5),
        "shift_mu": uni((L, SHIFT_WIDTH), 0.0, 1.0),
        "decay_w0": uni((L, R), -6.0, 1.0),
        "decay_up": nrm((L, DECAY_LORA, R), DECAY_LORA ** -0.5),
        "iclr_a0": nrm((L, R), 0.5),
        "iclr_up": nrm((L, ICLR_LORA, R), ICLR_LORA ** -0.5),
        "outgate_up": nrm((L, GATE_LORA, R), GATE_LORA ** -0.5),
        "k_k": 0.85 + nrm((L, R), 0.02),
        "k_a": 1.0 + nrm((L, R), 0.02),
        "r_k": nrm((L, N_HEADS, HEAD_SIZE), 0.1),
        "lnx_w": 1.0 + nrm((L, R), 0.02),
        "lnx_b": nrm((L, R), 0.02),
        "vres_down": nrm((V, D, VRES_LORA), D ** -0.5),
        "vres_mu": uni((V, VRES_LORA), 0.0, 1.0),
        "vres_up": nrm((V, VRES_LORA, R), VRES_LORA ** -0.5),
        "vres_v0": nrm((V, R), 0.5),
        "pool_w": nrm((L, POOL_GROUPS, POOL_GROUP_WIDTH, POOL_GROUP_WIDTH), POOL_GROUP_WIDTH ** -0.5),
        "pool_scale": 1.0 + nrm((L, POOL_WIDTH), 0.1),
        "proj_a": nrm((L, R, D), R ** -0.5),
        "proj_b": nrm((L, POOL_WIDTH, D), POOL_WIDTH ** -0.5),
        "w_o": nrm((L, D, D), D ** -0.5),
        "norm_ffn": 1.0 + nrm((L, D), 0.02),
        "ffn_gate": nrm((N_DENSE, D, D_FF_DENSE), D ** -0.5),
        "ffn_up": nrm((N_DENSE, D, D_FF_DENSE), D ** -0.5),
        "ffn_down": nrm((N_DENSE, D_FF_DENSE, D), D_FF_DENSE ** -0.5),
        "router": nrm((N_MOE, D, N_EXPERTS), D ** -0.5),
        "moe_gate": nrm((N_MOE, N_EXPERTS, D, D_FF_EXPERT), D ** -0.5),
        "moe_up": nrm((N_MOE, N_EXPERTS, D, D_FF_EXPERT), D ** -0.5),
        "moe_down": nrm((N_MOE, N_EXPERTS, D_FF_EXPERT, D), D_FF_EXPERT ** -0.5),
        "norm_final": 1.0 + nrm((D,), 0.02),
    }


def reference(x, norm_mix, w_in, shift_mu, decay_w0, decay_up, iclr_a0, iclr_up, outgate_up,
              k_k, k_a, r_k, lnx_w, lnx_b, vres_down, vres_mu, vres_up, vres_v0, pool_w, pool_scale,
              proj_a, proj_b, w_o, norm_ffn, ffn_gate, ffn_up, ffn_down, router, moe_gate, moe_up,
              moe_down, norm_final):
    R = RWKV_WIDTH
    rwkv_splits = [R, 2 * R, 3 * R, 3 * R + DECAY_LORA, 3 * R + DECAY_LORA + ICLR_LORA]
    in_splits = [SHIFT_WIDTH, SHIFT_WIDTH + POOL_WIDTH, SHIFT_WIDTH + POOL_WIDTH + D_MODEL]
    v_first = None
    for l in range(DEPTH):
        h = rmsnorm(x, norm_mix[l])
        p = h @ w_in[l]
        p_rwkv, p_pool, gate_a, gate_b = jnp.split(p, in_splits, axis=-1)
        z = token_shift(p_rwkv, shift_mu[l])
        r, k, v, wd, ad, gd = jnp.split(z, rwkv_splits, axis=-1)
        if l == 0:
            v_first = v
        else:
            vd = token_shift(h @ vres_down[l - 1], vres_mu[l - 1])
            v = v + (v_first - v) * jax.nn.sigmoid(vres_v0[l - 1] + vd @ vres_up[l - 1])
        y_a = rwkv7_core(r, k, v, wd, ad, gd, decay_w0[l], decay_up[l], iclr_a0[l], iclr_up[l],
                         outgate_up[l], k_k[l], k_a[l], r_k[l], lnx_w[l], lnx_b[l])
        y_b = multiscale_pool(p_pool, pool_w[l], pool_scale[l])
        merged = (jax.nn.sigmoid(gate_a) * (y_a @ proj_a[l])
                  + jax.nn.sigmoid(gate_b) * (y_b @ proj_b[l]))
        x = x + merged @ w_o[l]
        h2 = rmsnorm(x, norm_ffn[l])
        if l % 2 == 0:
            i = l // 2
            x = x + swiglu(h2, ffn_gate[i], ffn_up[i], ffn_down[i])
        else:
            i = l // 2
            x = x + moe_swiglu(h2, router[i], moe_gate[i], moe_up[i], moe_down[i])
    return rmsnorm(x, norm_final)
```

```python
import functools

import jax
import jax.numpy as jnp
from jax import lax
from jax.experimental import pallas as pl
from jax.experimental.pallas import tpu as pltpu

F32 = jnp.float32
BF16 = jnp.bfloat16

HEAD_SIZE = 64
POOL_WINDOWS = (2, 4, 8, 16)
N_EXPERTS = 8
TOP_K = 2
NORM_EPS = 1e-6
GN_EPS = 64e-5
LANES = 128
VMEM_LIMIT_BYTES = 56 * 1024 * 1024
WKV_CHUNK = 64
POOL_HALO = 16


def _cparams(n_axes):
    return pltpu.CompilerParams(dimension_semantics=("arbitrary",) * n_axes,
                                vmem_limit_bytes=VMEM_LIMIT_BYTES)


def _dot(a, b):
    return jnp.dot(a.astype(BF16), b.astype(BF16), preferred_element_type=F32)


def _split(x):
    hi = x.astype(BF16)
    lo = (x - hi.astype(F32)).astype(BF16)
    return hi, lo


def _dot_x3(a, b):
    a_hi, a_lo = _split(a)
    b_hi, b_lo = _split(b)
    d = functools.partial(jnp.dot, preferred_element_type=F32)
    return d(a_hi, b_hi) + d(a_hi, b_lo) + d(a_lo, b_hi)


def _dot_lhs_split(a, b_bf16):
    a_hi, a_lo = _split(a)
    d = functools.partial(jnp.dot, preferred_element_type=F32)
    return d(a_hi, b_bf16) + d(a_lo, b_bf16)


def _dot_tn(a, b):
    return lax.dot_general(a.astype(BF16), b.astype(BF16), (((0,), (0,)), ((), ())),
                           preferred_element_type=F32)


def _dot_nt(a, b):
    return lax.dot_general(a.astype(BF16), b.astype(BF16), (((1,), (1,)), ((), ())),
                           preferred_element_type=F32)


def _sigmoid(x):
    return 1.0 / (1.0 + jnp.exp(-x))


def _rmsnorm_body(x_ref, g_ref, o_ref):
    x = x_ref[...]
    inv = lax.rsqrt(jnp.mean(x * x, axis=-1, keepdims=True) + NORM_EPS)
    o_ref[...] = (x * inv * g_ref[...]).astype(o_ref.dtype)


def _rmsnorm(x, g, out_dtype, tm=512):
    m, d = x.shape
    return pl.pallas_call(
        _rmsnorm_body,
        grid=(m // tm,),
        in_specs=[pl.BlockSpec((tm, d), lambda i: (i, 0)),
                  pl.BlockSpec((1, d), lambda i: (0, 0))],
        out_specs=pl.BlockSpec((tm, d), lambda i: (i, 0)),
        out_shape=jax.ShapeDtypeStruct((m, d), out_dtype),
        compiler_params=_cparams(1),
        name="rmsnorm",
    )(x, g.reshape(1, d))


def _mm_plain_body(a_ref, w_ref, o_ref):
    o_ref[...] = jnp.dot(a_ref[...], w_ref[...], preferred_element_type=F32).astype(o_ref.dtype)


def _mm_swiglu_body(a_ref, wg_ref, wu_ref, o_ref):
    a = a_ref[...]
    g = jnp.dot(a, wg_ref[...], preferred_element_type=F32)
    u = jnp.dot(a, wu_ref[...], preferred_element_type=F32)
    o_ref[...] = (g * _sigmoid(g) * u).astype(o_ref.dtype)


def _mm_residual_body(a_ref, w_ref, res_ref, o_ref):
    o_ref[...] = res_ref[...] + jnp.dot(a_ref[...], w_ref[...], preferred_element_type=F32)


def _matmul(a, ws, *, tm, tn, out_dtype, residual=None):
    m, k = a.shape
    n = ws[0].shape[1]
    a_spec = pl.BlockSpec((tm, k), lambda j, i: (i, 0))
    w_spec = pl.BlockSpec((k, tn), lambda j, i: (0, j))
    o_spec = pl.BlockSpec((tm, tn), lambda j, i: (i, j))
    if len(ws) == 2:
        body, in_specs, args = _mm_swiglu_body, [a_spec, w_spec, w_spec], (a, ws[0], ws[1])
    elif residual is not None:
        body, in_specs, args = _mm_residual_body, [a_spec, w_spec, o_spec], (a, ws[0], residual)
    else:
        body, in_specs, args = _mm_plain_body, [a_spec, w_spec], (a, ws[0])
    return pl.pallas_call(
        body,
        grid=(n // tn, m // tm),
        in_specs=in_specs,
        out_specs=o_spec,
        out_shape=jax.ShapeDtypeStruct((m, n), out_dtype),
        compiler_params=_cparams(2),
        name="matmul",
    )(*args)


def _head_sum(x, sel, sel_t):
    s = _dot_lhs_split(x, sel)
    return _dot_lhs_split(s, sel_t)


def _token_shift(p, carry_ref, mu):
    rows = p.shape[0]
    row = lax.broadcasted_iota(jnp.int32, p.shape, 0)
    prev = jnp.where(row == 0, carry_ref[...], pltpu.roll(p, 1, axis=0))
    carry_ref[...] = p[rows - 1:rows, :]
    return p + (prev - p) * mu


def _prep_body(*refs, tiles_per_seq, has_vres, width):
    (rkv_ref, lora_ref, mu_rkv_ref, mu_lora_ref, w0_ref, a0_ref, kk_ref, ka_ref,
     dup_ref, iup_ref, gup_ref, sel_ref, selt_ref) = refs[:13]
    pos = 13
    if has_vres:
        vfirst_ref, v0_ref, vup_ref = refs[pos:pos + 3]
        pos += 3
    r_o, lw_o, k_o, v_o, a_o, b_o, g_o, carry_rkv, carry_lora = refs[pos:]
    R = width

    @pl.when(pl.program_id(0) % tiles_per_seq == 0)
    def _():
        carry_rkv[...] = jnp.zeros_like(carry_rkv)
        carry_lora[...] = jnp.zeros_like(carry_lora)

    z = _token_shift(rkv_ref[...], carry_rkv, mu_rkv_ref[...])
    zl = _token_shift(lora_ref[...], carry_lora, mu_lora_ref[...])
    r, k, v = z[:, :R], z[:, R:2 * R], z[:, 2 * R:]

    decay_logit = w0_ref[...] + _dot_x3(jnp.tanh(zl[:, :128]), dup_ref[...])
    neg = -decay_logit
    softplus = jnp.maximum(neg, 0.0) + jnp.log(1.0 + jnp.exp(-jnp.abs(neg)))
    lw_o[...] = -jnp.exp(-softplus - 0.5)
    a = _sigmoid(a0_ref[...] + _dot_x3(zl[:, :256], iup_ref[...]))
    g_o[...] = _dot(_sigmoid(zl[:, 256:]), gup_ref[...])
    if has_vres:
        mix = _sigmoid(v0_ref[...] + _dot(zl[:, 128:256], vup_ref[...]))
        v = v + (vfirst_ref[...] - v) * mix

    kk = k * kk_ref[...]
    norm = jnp.sqrt(_head_sum(kk * kk, sel_ref[...], selt_ref[...]))
    kk = kk / jnp.maximum(norm, 1e-12)
    r_o[...] = r
    k_o[...] = k * (1.0 + (a - 1.0) * ka_ref[...])
    v_o[...] = v
    a_o[...] = -kk
    b_o[...] = kk * a


def _rwkv_prep(p_all, seq, width, consts, vres, tm=256):
    m = p_all.shape[0]
    R = width
    lora_w = 512
    lora_blk = (p_all.shape[1] - lora_w) // lora_w
    row = lambda i: (i, 0)
    fixed = lambda i: (0, 0)
    in_specs = [pl.BlockSpec((tm, 3 * R), row),
                pl.BlockSpec((tm, lora_w), lambda i: (i, lora_blk))]
    args = [p_all, p_all]
    for c in consts:
        in_specs.append(pl.BlockSpec(c.shape, fixed))
        args.append(c)
    if vres is not None:
        vfirst, v0, vup = vres
        in_specs += [pl.BlockSpec((tm, R), row), pl.BlockSpec(v0.shape, fixed), pl.BlockSpec(vup.shape, fixed)]
        args += [vfirst, v0, vup]
    out = jax.ShapeDtypeStruct((m, R), F32)
    return pl.pallas_call(
        functools.partial(_prep_body, tiles_per_seq=seq // tm, has_vres=vres is not None, width=R),
        grid=(m // tm,),
        in_specs=in_specs,
        out_specs=[pl.BlockSpec((tm, R), row)] * 7,
        out_shape=[out] * 7,
        scratch_shapes=[pltpu.VMEM((1, 3 * R), F32), pltpu.VMEM((1, lora_w), F32)],
        compiler_params=_cparams(1),
        name="rwkv_prep",
    )(*args)


def _wkv_chunk(r, lw, k, v, a, b, state, masks):
    tri_incl, strict, eye = masks
    C = r.shape[0]
    lw_hi = lw.astype(BF16)
    lw_mid = (lw - lw_hi.astype(F32))
    lw_mid_b = lw_mid.astype(BF16)
    lw_lo = (lw_mid - lw_mid_b.astype(F32)).astype(BF16)
    d = functools.partial(jnp.dot, preferred_element_type=F32)
    tri_b = tri_incl.astype(BF16)
    cum = d(tri_b, lw_hi) + d(tri_b, lw_mid_b) + d(tri_b, lw_lo)
    cum_last = cum[C - 1:C, :]
    w_inv = jnp.exp(-cum)
    w_rem = jnp.exp(cum_last - cum)
    a_t = a * jnp.exp(cum - lw)
    r_t = r * jnp.exp(cum)
    b_t = b * w_inv
    k_t = k * w_inv
    b_p = b * w_rem
    k_p = k * w_rem
    w_c = jnp.exp(cum_last)

    lhs = jnp.concatenate([a_t, r_t], axis=0)
    ab = _dot_nt(lhs, b_t)
    ak = _dot_nt(lhs, k_t)
    zero = jnp.zeros((C, C), F32)
    a_ab = jnp.where(strict, ab[:C], zero)
    a_rb = jnp.where(tri_incl, ab[C:], zero)
    a_ak = jnp.where(strict, ak[:C], zero)
    a_rk = jnp.where(tri_incl, ak[C:], zero)

    x = jnp.concatenate([a_t, _dot(a_ak, v)], axis=1)
    n = a_ab
    levels = C.bit_length() - 1
    for lvl in range(levels):
        x = x + _dot(n, x)
        if lvl + 1 < levels:
            n = _dot(n, n)
    hs = HEAD_SIZE
    bx = _dot_tn(b_p, x)
    trans = jnp.where(eye, w_c, 0.0) + bx[:, :hs]
    inject = bx[:, hs:] + _dot_tn(k_p, v)
    rx = _dot(a_rb, x)
    r_eff = r_t + rx[:, :hs]
    y = _dot(r_eff, state) + rx[:, hs:] + _dot(a_rk, v)
    new_state = _dot_x3(trans, state) + inject
    return y, new_state


def _wkv_body(r_ref, lw_ref, k_ref, v_ref, a_ref, b_ref, y_ref, state_ref, *, heads, chunks, chunk):
    @pl.when(pl.program_id(1) == 0)
    def _():
        state_ref[...] = jnp.zeros_like(state_ref)

    C = chunk
    row = lax.broadcasted_iota(jnp.int32, (C, C), 0)
    col = lax.broadcasted_iota(jnp.int32, (C, C), 1)
    masks = (row >= col, row > col, row == col)

    def step(c, carry):
        sl = pl.ds(pl.multiple_of(c * C, C), C)
        for h in range(heads):
            y, s = _wkv_chunk(r_ref[h, sl, :], lw_ref[h, sl, :], k_ref[h, sl, :], v_ref[h, sl, :],
                              a_ref[h, sl, :], b_ref[h, sl, :], state_ref[h], masks)
            y_ref[h, sl, :] = y
            state_ref[h] = s
        return carry

    lax.fori_loop(0, chunks, step, 0)


def _wkv(r, lw, k, v, a, b, heads=4, span=512):
    bh, s, hs = r.shape
    spec = pl.BlockSpec((heads, span, hs), lambda g, t: (g, t, 0))
    return pl.pallas_call(
        functools.partial(_wkv_body, heads=heads, chunks=span // WKV_CHUNK, chunk=WKV_CHUNK),
        grid=(bh // heads, s // span),
        in_specs=[spec] * 6,
        out_specs=spec,
        out_shape=jax.ShapeDtypeStruct((bh, s, hs), F32),
        scratch_shapes=[pltpu.VMEM((heads, hs, hs), F32)],
        compiler_params=_cparams(2),
        name="wkv7",
    )(r, lw, k, v, a, b)


def _post_pool_body(y_ref, r_ref, k_ref, v_ref, g_ref, u_ref, lnw_ref, lnb_ref, rk_ref, sel_ref, selt_ref,
                    pw_ref, ps_ref, ya_o, yb_o, ext_ref, *, tiles_per_seq, tm, group_width):
    sel, selt = sel_ref[...], selt_ref[...]
    inv_n = 1.0 / HEAD_SIZE
    y = y_ref[...]
    mean = _head_sum(y, sel, selt) * inv_n
    yc = y - mean
    var = _head_sum(yc * yc, sel, selt) * inv_n
    yn = yc * lax.rsqrt(var + GN_EPS) * lnw_ref[...] + lnb_ref[...]
    v = v_ref[...]
    bonus = _head_sum(r_ref[...] * k_ref[...] * rk_ref[...], sel, selt) * v
    ya_o[...] = ((yn + bonus) * g_ref[...]).astype(ya_o.dtype)

    seq_tile = pl.program_id(0) % tiles_per_seq

    @pl.when(seq_tile == 0)
    def _():
        ext_ref[0:POOL_HALO, :] = jnp.zeros((POOL_HALO, ext_ref.shape[1]), F32)

    u = u_ref[...]
    ext_ref[POOL_HALO:POOL_HALO + tm, :] = u
    t_pos = seq_tile * tm + lax.broadcasted_iota(jnp.int32, (tm, group_width), 0)
    for gi, w in enumerate(POOL_WINDOWS):
        lanes = slice(gi * group_width, (gi + 1) * group_width)
        acc = u[:, lanes]
        for j in range(1, w):
            acc = acc + ext_ref[POOL_HALO - j:POOL_HALO - j + tm, lanes]
        count = jnp.minimum(t_pos + 1, w).astype(F32)
        mixed = acc / count - u[:, lanes]
        yg = _dot_x3(mixed, pw_ref[gi])
        yb_o[:, lanes] = (yg * ps_ref[:, lanes]).astype(yb_o.dtype)
    ext_ref[0:POOL_HALO, :] = u[tm - POOL_HALO:, :]


def _post_pool(y, r, k, v, g, p_all, pool_blk, consts, seq, tm=256):
    m, R = y.shape
    lnw, lnb, rk, sel, selt, pw, ps = consts
    row = lambda i: (i, 0)
    fixed2 = lambda i: (0, 0)
    in_specs = [pl.BlockSpec((tm, R), row)] * 5
    in_specs.append(pl.BlockSpec((tm, R), lambda i: (i, pool_blk)))
    in_specs += [pl.BlockSpec(c.shape, fixed2) for c in (lnw, lnb, rk, sel, selt)]
    in_specs += [pl.BlockSpec(pw.shape, lambda i: (0, 0, 0)), pl.BlockSpec(ps.shape, fixed2)]
    out = jax.ShapeDtypeStruct((m, R), BF16)
    return pl.pallas_call(
        functools.partial(_post_pool_body, tiles_per_seq=seq // tm, tm=tm, group_width=R // len(POOL_WINDOWS)),
        grid=(m // tm,),
        in_specs=in_specs,
        out_specs=[pl.BlockSpec((tm, R), row)] * 2,
        out_shape=[out, out],
        scratch_shapes=[pltpu.VMEM((POOL_HALO + tm, R), F32)],
        compiler_params=_cparams(1),
        name="post_pool",
    )(y, r, k, v, g, p_all, lnw, lnb, rk, sel, selt, pw, ps)


def _merge_body(ya_ref, yb_ref, pa_ref, pb_ref, ga_ref, gb_ref, o_ref):
    ma = jnp.dot(ya_ref[...], pa_ref[...], preferred_element_type=F32)
    mb = jnp.dot(yb_ref[...], pb_ref[...], preferred_element_type=F32)
    o_ref[...] = (_sigmoid(ga_ref[...]) * ma + _sigmoid(gb_ref[...]) * mb).astype(o_ref.dtype)


def _merge(ya, yb, pa, pb, p_all, ga_col, gb_col, tm=512, tn=512):
    m, R = ya.shape
    d = pa.shape[1]
    ga_blk, gb_blk = ga_col // tn, gb_col // tn
    return pl.pallas_call(
        _merge_body,
        grid=(d // tn, m // tm),
        in_specs=[pl.BlockSpec((tm, R), lambda j, i: (i, 0)),
                  pl.BlockSpec((tm, R), lambda j, i: (i, 0)),
                  pl.BlockSpec((R, tn), lambda j, i: (0, j)),
                  pl.BlockSpec((R, tn), lambda j, i: (0, j)),
                  pl.BlockSpec((tm, tn), lambda j, i: (i, ga_blk + j)),
                  pl.BlockSpec((tm, tn), lambda j, i: (i, gb_blk + j))],
        out_specs=pl.BlockSpec((tm, tn), lambda j, i: (i, j)),
        out_shape=jax.ShapeDtypeStruct((m, d), BF16),
        compiler_params=_cparams(2),
        name="merge",
    )(ya, yb, pa, pb, p_all, p_all)


def _router_body(x_ref, g_ref, wr_ref, h_o, idx_o, gate_o):
    x = x_ref[...]
    inv = lax.rsqrt(jnp.mean(x * x, axis=-1, keepdims=True) + NORM_EPS)
    h = x * inv * g_ref[...]
    h_o[...] = h
    logits = _dot_x3(h, wr_ref[...])
    lane = lax.broadcasted_iota(jnp.int32, logits.shape, 1)
    neg_inf = jnp.float32(-jnp.inf)
    l1 = jnp.where(lane < N_EXPERTS, logits, neg_inf)
    m1 = jnp.max(l1, axis=-1, keepdims=True)
    i1 = jnp.min(jnp.where(l1 == m1, lane, LANES), axis=-1, keepdims=True)
    l2 = jnp.where(lane == i1, neg_inf, l1)
    m2 = jnp.max(l2, axis=-1, keepdims=True)
    i2 = jnp.min(jnp.where(l2 == m2, lane, LANES), axis=-1, keepdims=True)
    e = jnp.exp(m2 - m1)
    g1 = 1.0 / (1.0 + e)
    idx_o[...] = jnp.where(lane == 0, i1, jnp.where(lane == 1, i2, 0))
    gate_o[...] = jnp.where(lane == 0, g1, jnp.where(lane == 1, e * g1, 0.0))


def _router(x, g, wr_pad, tm=256):
    m, d = x.shape
    row = lambda i: (i, 0)
    return pl.pallas_call(
        _router_body,
        grid=(m // tm,),
        in_specs=[pl.BlockSpec((tm, d), row), pl.BlockSpec((1, d), lambda i: (0, 0)),
                  pl.BlockSpec(wr_pad.shape, lambda i: (0, 0))],
        out_specs=[pl.BlockSpec((tm, d), row), pl.BlockSpec((tm, LANES), row), pl.BlockSpec((tm, LANES), row)],
        out_shape=[jax.ShapeDtypeStruct((m, d), F32), jax.ShapeDtypeStruct((m, LANES), jnp.int32),
                   jax.ShapeDtypeStruct((m, LANES), F32)],
        compiler_params=_cparams(1),
        name="router",
    )(x, g.reshape(1, d), wr_pad)


def _gather_body(tok_ref, h_hbm, o_ref, buf_ref, sem, *, tg):
    base = pl.program_id(0) * tg

    def row_copy(i):
        return pltpu.make_async_copy(h_hbm.at[pl.ds(tok_ref[base + i], 1), :], buf_ref.at[pl.ds(i, 1), :], sem)

    def start(i, c):
        row_copy(i).start()
        return c

    def wait(i, c):
        row_copy(i).wait()
        return c

    lax.fori_loop(0, tg, start, 0)
    lax.fori_loop(0, tg, wait, 0)
    o_ref[...] = buf_ref[...].astype(o_ref.dtype)


def _gather_rows(tok, h, tg=256):
    p = tok.shape[0]
    d = h.shape[1]
    return pl.pallas_call(
        functools.partial(_gather_body, tg=tg),
        grid_spec=pltpu.PrefetchScalarGridSpec(
            num_scalar_prefetch=1,
            grid=(p // tg,),
            in_specs=[pl.BlockSpec(memory_space=pl.ANY)],
            out_specs=pl.BlockSpec((tg, d), lambda i, tok: (i, 0)),
            scratch_shapes=[pltpu.VMEM((tg, d), F32), pltpu.SemaphoreType.DMA(())]),
        out_shape=jax.ShapeDtypeStruct((p, d), BF16),
        compiler_params=_cparams(1),
        name="moe_gather",
    )(tok, h)


def _grouped_body(meta_ref, *refs, body):
    @pl.when(pl.program_id(0) < meta_ref[0])
    def _():
        body(*refs)

    @pl.when(pl.program_id(0) >= meta_ref[0])
    def _():
        refs[-1][...] = jnp.zeros_like(refs[-1])


def _grouped_matmul(meta, a, ws, *, tm, tn, out_dtype):
    p, k = a.shape
    n = ws[0].shape[2]
    nj = n // tn

    def blk(b, meta):
        return jnp.minimum(b, meta[0] - 1)

    def col(b, j, meta):
        return jnp.where(b < meta[0], j, nj - 1)

    a_spec = pl.BlockSpec((tm, k), lambda b, j, meta: (blk(b, meta), 0))
    w_spec = pl.BlockSpec((None, k, tn), lambda b, j, meta: (meta[1 + blk(b, meta)], 0, col(b, j, meta)))
    o_spec = pl.BlockSpec((tm, tn), lambda b, j, meta: (b, j))
    body = _mm_swiglu_body if len(ws) == 2 else _mm_plain_body
    return pl.pallas_call(
        functools.partial(_grouped_body, body=body),
        grid_spec=pltpu.PrefetchScalarGridSpec(
            num_scalar_prefetch=1,
            grid=(p // tm, nj),
            in_specs=[a_spec] + [w_spec] * len(ws),
            out_specs=o_spec),
        out_shape=jax.ShapeDtypeStruct((p, n), out_dtype),
        compiler_params=_cparams(2),
        name="moe_matmul",
    )(meta, a, *ws)


def _combine_body(d0_ref, d1_ref, x_ref, gate_ref, g_ref, y_hbm, o_ref, buf0, buf1, sem, *, tg, apply_norm):
    base = pl.program_id(0) * tg

    def row_copy(i, d_ref, buf):
        return pltpu.make_async_copy(y_hbm.at[pl.ds(d_ref[base + i], 1), :], buf.at[pl.ds(i, 1), :], sem)

    def start(i, c):
        row_copy(i, d0_ref, buf0).start()
        row_copy(i, d1_ref, buf1).start()
        return c

    def wait(i, c):
        row_copy(i, d0_ref, buf0).wait()
        row_copy(i, d1_ref, buf1).wait()
        return c

    lax.fori_loop(0, tg, start, 0)
    lax.fori_loop(0, tg, wait, 0)
    gate = gate_ref[...]
    x = x_ref[...] + gate[:, 0:1] * buf0[...] + gate[:, 1:2] * buf1[...]
    if apply_norm:
        x = x * lax.rsqrt(jnp.mean(x * x, axis=-1, keepdims=True) + NORM_EPS) * g_ref[...]
    o_ref[...] = x


def _combine(d0, d1, x, gate, g, ybuf, apply_norm, tg=256):
    m, d = x.shape
    row = lambda i, d0, d1: (i, 0)
    return pl.pallas_call(
        functools.partial(_combine_body, tg=tg, apply_norm=apply_norm),
        grid_spec=pltpu.PrefetchScalarGridSpec(
            num_scalar_prefetch=2,
            grid=(m // tg,),
            in_specs=[pl.BlockSpec((tg, d), row), pl.BlockSpec((tg, LANES), row),
                      pl.BlockSpec((1, d), lambda i, d0, d1: (0, 0)),
                      pl.BlockSpec(memory_space=pl.ANY)],
            out_specs=pl.BlockSpec((tg, d), row),
            scratch_shapes=[pltpu.VMEM((tg, d), F32), pltpu.VMEM((tg, d), F32), pltpu.SemaphoreType.DMA(())]),
        out_shape=jax.ShapeDtypeStruct((m, d), F32),
        compiler_params=_cparams(1),
        name="moe_combine",
    )(d0, d1, x, gate, g.reshape(1, d), ybuf)


def _moe_layout(top_e, tm):
    n_tok = top_e.shape[0]
    n_slot = n_tok * TOP_K
    flat_e = top_e.reshape(n_slot)
    onehot = (flat_e[:, None] == jnp.arange(N_EXPERTS, dtype=jnp.int32)[None, :]).astype(jnp.int32)
    csum = jnp.cumsum(onehot, axis=0)
    rank = jnp.sum((csum - onehot) * onehot, axis=1)
    counts = csum[-1]
    padded = (counts + tm - 1) // tm * tm
    pend = jnp.cumsum(padded)
    dest = (pend - padded)[flat_e] + rank
    n_blocks = -(-n_slot // tm) + N_EXPERTS
    tok = jnp.zeros((n_blocks * tm,), jnp.int32).at[dest].set(jnp.arange(n_slot, dtype=jnp.int32) // TOP_K)
    blk_e = jnp.minimum(jnp.searchsorted(pend, jnp.arange(n_blocks, dtype=jnp.int32) * tm, side="right"),
                        N_EXPERTS - 1).astype(jnp.int32)
    meta = jnp.concatenate([(pend[-1] // tm).astype(jnp.int32)[None], blk_e])
    dest = dest.reshape(n_tok, TOP_K).astype(jnp.int32)
    return tok, meta, dest[:, 0], dest[:, 1]


def kernel(x, norm_mix, w_in, shift_mu, decay_w0, decay_up, iclr_a0, iclr_up, outgate_up, k_k, k_a, r_k, lnx_w, lnx_b, vres_down, vres_mu, vres_up, vres_v0, pool_w, pool_scale, proj_a, proj_b, w_o, norm_ffn, ffn_gate, ffn_up, ffn_down, router, moe_gate, moe_up, moe_down, norm_final):
    bsz, seq, d = x.shape
    depth = w_in.shape[0]
    R = decay_w0.shape[1]
    n_heads = R // HEAD_SIZE
    dl, il, gl = decay_up.shape[1], iclr_up.shape[1], outgate_up.shape[1]
    vl = vres_up.shape[1]
    assert (dl, il, vl, gl) == (96, 96, 64, 256) and 3 * R + dl + il + gl == shift_mu.shape[1]
    m = bsz * seq
    x = x.reshape(m, d)
    row = lambda t: t.reshape(1, -1)

    head_of_lane = jnp.arange(R, dtype=jnp.int32) // HEAD_SIZE
    sel = (head_of_lane[:, None] == jnp.arange(LANES, dtype=jnp.int32)[None, :]).astype(BF16)
    sel_t = sel.T

    v_first = None
    for l in range(depth):
        w = w_in[l]
        c0 = 3 * R
        c1, c2, c3 = c0 + dl, c0 + dl + il, c0 + dl + il + gl
        vd_w = vres_down[l - 1] if l > 0 else jnp.zeros((d, vl), F32)
        vd_mu = vres_mu[l - 1] if l > 0 else jnp.zeros((vl,), F32)
        w_all = jnp.concatenate([w[:, :c0], w[:, c3:], w[:, c0:c2], vd_w, w[:, c2:c3]], axis=1).astype(BF16)
        mu = shift_mu[l]
        mu_rkv = row(mu[:c0])
        mu_lora = row(jnp.concatenate([mu[c0:c2], vd_mu, mu[c2:c3]]))
        pool_col = c0
        ga_col = c0 + R
        gb_col = ga_col + d
        zeros = lambda n: jnp.zeros((n, R), F32)
        dup = jnp.concatenate([decay_up[l], zeros(128 - dl)], axis=0)
        iup = jnp.concatenate([zeros(dl), iclr_up[l], zeros(256 - dl - il)], axis=0)
        gup = outgate_up[l].astype(BF16)

        h = _rmsnorm(x, norm_mix[l], BF16)
        p_all = _matmul(h, [w_all], tm=1024, tn=512, out_dtype=F32)

        consts = [mu_rkv, mu_lora, row(decay_w0[l]), row(iclr_a0[l]), row(k_k[l]), row(k_a[l]),
                  dup, iup, gup, sel, sel_t]
        vres = None
        if l > 0:
            vup = jnp.concatenate([zeros(128 - vl), vres_up[l - 1]], axis=0).astype(BF16)
            vres = (v_first, row(vres_v0[l - 1]), vup)
        r, lw, k, v, a, b, g = _rwkv_prep(p_all, seq, R, consts, vres)
        if l == 0:
            v_first = v

        to_heads = lambda t: t.reshape(bsz, seq, n_heads, HEAD_SIZE).transpose(0, 2, 1, 3).reshape(
            bsz * n_heads, seq, HEAD_SIZE)
        y = _wkv(*(to_heads(t) for t in (r, lw, k, v, a, b)))
        y = y.reshape(bsz, n_heads, seq, HEAD_SIZE).transpose(0, 2, 1, 3).reshape(m, R)

        ya, yb = _post_pool(y, r, k, v, g, p_all, pool_col // R,
                            [row(lnx_w[l]), row(lnx_b[l]), row(r_k[l]), sel, sel_t, pool_w[l], row(pool_scale[l])],
                            seq)
        merged = _merge(ya, yb, proj_a[l].astype(BF16), proj_b[l].astype(BF16), p_all, ga_col, gb_col)
        x = _matmul(merged, [w_o[l].astype(BF16)], tm=1024, tn=512, out_dtype=F32, residual=x)

        i = l // 2
        if l % 2 == 0:
            h2 = _rmsnorm(x, norm_ffn[l], BF16)
            hidden = _matmul(h2, [ffn_gate[i].astype(BF16), ffn_up[i].astype(BF16)], tm=1024, tn=512,
                             out_dtype=BF16)
            x = _matmul(hidden, [ffn_down[i].astype(BF16)], tm=512, tn=512, out_dtype=F32, residual=x)
            if l == depth - 1:
                x = _rmsnorm(x, norm_final, F32)
        else:
            tm_e = 512
            wr_pad = jnp.concatenate([router[i], jnp.zeros((d, LANES - N_EXPERTS), F32)], axis=1)
            h2, idx, gate = _router(x, norm_ffn[l], wr_pad)
            tok, meta, d0, d1 = _moe_layout(idx[:, :TOP_K], tm_e)
            xs = _gather_rows(tok, h2)
            hidden = _grouped_matmul(meta, xs, [moe_gate[i].astype(BF16), moe_up[i].astype(BF16)],
                                     tm=tm_e, tn=512, out_dtype=BF16)
            ybuf = _grouped_matmul(meta, hidden, [moe_down[i].astype(BF16)], tm=tm_e, tn=512, out_dtype=F32)
            x = _combine(d0, d1, x, gate, norm_final, ybuf, apply_norm=l == depth - 1)
    return x.reshape(bsz, seq, d)
```

```python
import functools

import jax
import jax.numpy as jnp
from jax import lax
from jax.experimental import pallas as pl
from jax.experimental.pallas import tpu as pltpu

F32 = jnp.float32
BF16 = jnp.bfloat16

HEAD_SIZE = 64
POOL_WINDOWS = (2, 4, 8, 16)
N_EXPERTS = 8
TOP_K = 2
NORM_EPS = 1e-6
GN_EPS = 64e-5
LANES = 128
VMEM_LIMIT_BYTES = 56 * 1024 * 1024
WKV_CHUNK = 64
POOL_HALO = 16
PREP_ROWS = 256


def _cparams(n_axes):
    return pltpu.CompilerParams(dimension_semantics=("arbitrary",) * n_axes,
                                vmem_limit_bytes=VMEM_LIMIT_BYTES)


def _dot(a, b):
    return jnp.dot(a.astype(BF16), b.astype(BF16), preferred_element_type=F32)


def _split(x):
    hi = x.astype(BF16)
    lo = (x - hi.astype(F32)).astype(BF16)
    return hi, lo


def _dot_x3(a, b):
    a_hi, a_lo = _split(a)
    b_hi, b_lo = _split(b)
    d = functools.partial(jnp.dot, preferred_element_type=F32)
    return d(a_hi, b_hi) + d(a_hi, b_lo) + d(a_lo, b_hi)


def _dot_lhs_split(a, b_bf16):
    a_hi, a_lo = _split(a)
    d = functools.partial(jnp.dot, preferred_element_type=F32)
    return d(a_hi, b_bf16) + d(a_lo, b_bf16)


def _dot_tn(a, b):
    return lax.dot_general(a.astype(BF16), b.astype(BF16), (((0,), (0,)), ((), ())),
                           preferred_element_type=F32)


def _dot_nt(a, b):
    return lax.dot_general(a.astype(BF16), b.astype(BF16), (((1,), (1,)), ((), ())),
                           preferred_element_type=F32)


def _sigmoid(x):
    return 1.0 / (1.0 + jnp.exp(-x))


def _rmsnorm_body(x_ref, g_ref, o_ref):
    x = x_ref[...]
    inv = lax.rsqrt(jnp.mean(x * x, axis=-1, keepdims=True) + NORM_EPS)
    o_ref[...] = (x * inv * g_ref[...]).astype(o_ref.dtype)


def _rmsnorm(x, g, out_dtype, tm=512):
    m, d = x.shape
    return pl.pallas_call(
        _rmsnorm_body,
        grid=(m // tm,),
        in_specs=[pl.BlockSpec((tm, d), lambda i: (i, 0)),
                  pl.BlockSpec((1, d), lambda i: (0, 0))],
        out_specs=pl.BlockSpec((tm, d), lambda i: (i, 0)),
        out_shape=jax.ShapeDtypeStruct((m, d), out_dtype),
        compiler_params=_cparams(1),
        name="rmsnorm",
    )(x, g.reshape(1, d))


def _mm_plain_body(a_ref, w_ref, o_ref):
    o_ref[...] = jnp.dot(a_ref[...], w_ref[...], preferred_element_type=F32).astype(o_ref.dtype)


def _mm_swiglu_body(a_ref, wg_ref, wu_ref, o_ref):
    a = a_ref[...]
    g = jnp.dot(a, wg_ref[...], preferred_element_type=F32)
    u = jnp.dot(a, wu_ref[...], preferred_element_type=F32)
    o_ref[...] = (g * _sigmoid(g) * u).astype(o_ref.dtype)


def _mm_residual_body(a_ref, w_ref, res_ref, o_ref):
    o_ref[...] = res_ref[...] + jnp.dot(a_ref[...], w_ref[...], preferred_element_type=F32)


def _matmul(a, ws, *, tm, tn, out_dtype, residual=None):
    m, k = a.shape
    n = ws[0].shape[1]
    a_spec = pl.BlockSpec((tm, k), lambda j, i: (i, 0))
    w_spec = pl.BlockSpec((k, tn), lambda j, i: (0, j))
    o_spec = pl.BlockSpec((tm, tn), lambda j, i: (i, j))
    if len(ws) == 2:
        body, in_specs, args = _mm_swiglu_body, [a_spec, w_spec, w_spec], (a, ws[0], ws[1])
    elif residual is not None:
        body, in_specs, args = _mm_residual_body, [a_spec, w_spec, o_spec], (a, ws[0], residual)
    else:
        body, in_specs, args = _mm_plain_body, [a_spec, w_spec], (a, ws[0])
    return pl.pallas_call(
        body,
        grid=(n // tn, m // tm),
        in_specs=in_specs,
        out_specs=o_spec,
        out_shape=jax.ShapeDtypeStruct((m, n), out_dtype),
        compiler_params=_cparams(2),
        name="matmul",
    )(*args)


def _head_sum(x, sel, sel_t):
    s = _dot_lhs_split(x, sel)
    return _dot_lhs_split(s, sel_t)


def _token_shift(p, carry_ref, mu):
    rows = p.shape[0]
    row = lax.broadcasted_iota(jnp.int32, p.shape, 0)
    prev = jnp.where(row == 0, carry_ref[...], pltpu.roll(p, 1, axis=0))
    carry_ref[...] = p[rows - 1:rows, :]
    return p + (prev - p) * mu


def _prep_body(*refs, tiles_per_seq, has_vres, emit_v, width):
    (rkv_ref, lora_ref, mu_rkv_ref, mu_lora_ref, w0_ref, a0_ref, kk_ref, ka_ref, rk_ref,
     dup_ref, iup_ref, gup_ref, sel_ref, selt_ref, tri_ref, ones_ref) = refs[:16]
    pos = 16
    if has_vres:
        vfirst_ref, v0_ref, vup_ref = refs[pos:pos + 3]
        pos += 3
    at_o, rt_o, bt_o, kt_o, bp_o, kp_o, vb_o, wc_o, bonus_o, g_o = refs[pos:pos + 10]
    pos += 10
    if emit_v:
        v_o = refs[pos]
        pos += 1
    carry_rkv, carry_lora = refs[pos:]
    R = width

    @pl.when(pl.program_id(0) % tiles_per_seq == 0)
    def _():
        carry_rkv[...] = jnp.zeros_like(carry_rkv)
        carry_lora[...] = jnp.zeros_like(carry_lora)

    z = _token_shift(rkv_ref[...], carry_rkv, mu_rkv_ref[...])
    zl = _token_shift(lora_ref[...], carry_lora, mu_lora_ref[...])
    r, k, v = z[:, :R], z[:, R:2 * R], z[:, 2 * R:]

    decay_logit = w0_ref[...] + _dot_x3(jnp.tanh(zl[:, :128]), dup_ref[...])
    neg = -decay_logit
    softplus = jnp.maximum(neg, 0.0) + jnp.log(1.0 + jnp.exp(-jnp.abs(neg)))
    lw = -jnp.exp(-softplus - 0.5)
    a = _sigmoid(a0_ref[...] + _dot_x3(zl[:, :256], iup_ref[...]))
    g_o[...] = _dot(_sigmoid(zl[:, 256:]), gup_ref[...])
    if has_vres:
        mix = _sigmoid(v0_ref[...] + _dot(zl[:, 128:256], vup_ref[...]))
        v = v + (vfirst_ref[...] - v) * mix
    if emit_v:
        v_o[...] = v

    sel, selt = sel_ref[...], selt_ref[...]
    kk = k * kk_ref[...]
    norm = jnp.sqrt(_head_sum(kk * kk, sel, selt))
    kk = kk / jnp.maximum(norm, 1e-12)
    k = k * (1.0 + (a - 1.0) * ka_ref[...])
    bonus_o[...] = _head_sum(r * k * rk_ref[...], sel, selt) * v

    d = functools.partial(jnp.dot, preferred_element_type=F32)
    lw_hi = lw.astype(BF16)
    rest = lw - lw_hi.astype(F32)
    lw_mid = rest.astype(BF16)
    lw_lo = (rest - lw_mid.astype(F32)).astype(BF16)
    tri, ones = tri_ref[...], ones_ref[...]
    cum = d(tri, lw_hi) + d(tri, lw_mid) + d(tri, lw_lo)
    total = d(ones, lw_hi) + d(ones, lw_mid) + d(ones, lw_lo)
    w_inv = jnp.exp(-cum)
    w_rem = jnp.exp(total - cum)
    b = kk * a
    wc_o[...] = jnp.exp(total)
    pairs = R // LANES
    for o_ref, val in ((at_o, -kk * jnp.exp(cum - lw)), (rt_o, r * jnp.exp(cum)), (bt_o, b * w_inv),
                       (kt_o, k * w_inv), (bp_o, b * w_rem), (kp_o, k * w_rem), (vb_o, v)):
        val = val.astype(BF16)
        for p in range(pairs):
            o_ref[0, p] = val[:, p * LANES:(p + 1) * LANES]


def _rwkv_prep(p_all, bsz, seq, width, consts, vres, emit_v):
    tm = PREP_ROWS
    m = p_all.shape[0]
    R = width
    lora_w = 512
    lora_blk = (p_all.shape[1] - lora_w) // lora_w
    tiles_per_seq = seq // tm
    row = lambda i: (i, 0)
    fixed = lambda i: (0, 0)
    in_specs = [pl.BlockSpec((tm, 3 * R), row),
                pl.BlockSpec((tm, lora_w), lambda i: (i, lora_blk))]
    args = [p_all, p_all]
    for c in consts:
        in_specs.append(pl.BlockSpec(c.shape, fixed))
        args.append(c)
    if vres is not None:
        vfirst, v0, vup = vres
        in_specs += [pl.BlockSpec((tm, R), row), pl.BlockSpec(v0.shape, fixed), pl.BlockSpec(vup.shape, fixed)]
        args += [vfirst, v0, vup]
    pairs = R // LANES
    pm_spec = pl.BlockSpec((1, pairs, tm, LANES), lambda i: (i // tiles_per_seq, 0, i % tiles_per_seq, 0))
    pm_shape = jax.ShapeDtypeStruct((bsz, pairs, seq, LANES), BF16)
    nat_spec = pl.BlockSpec((tm, R), row)
    nat_shape = jax.ShapeDtypeStruct((m, R), F32)
    n_nat = 4 if emit_v else 3
    return pl.pallas_call(
        functools.partial(_prep_body, tiles_per_seq=tiles_per_seq, has_vres=vres is not None, emit_v=emit_v,
                          width=R),
        grid=(m // tm,),
        in_specs=in_specs,
        out_specs=[pm_spec] * 7 + [nat_spec] * n_nat,
        out_shape=[pm_shape] * 7 + [nat_shape] * n_nat,
        scratch_shapes=[pltpu.VMEM((1, 3 * R), F32), pltpu.VMEM((1, lora_w), F32)],
        compiler_params=_cparams(1),
        name="rwkv_prep",
    )(*args)


def _wkv_body(at_ref, rt_ref, bt_ref, kt_ref, bp_ref, kp_ref, v_ref, wc_ref, y_ref, state_ref, *, pairs, chunks):
    C = WKV_CHUNK
    hs = HEAD_SIZE

    @pl.when(pl.program_id(1) == 0)
    def _():
        state_ref[...] = jnp.zeros_like(state_ref)

    def head_masks(rows, width):
        lane = lax.broadcasted_iota(jnp.int32, (rows, width), 1)
        first = (lane % LANES) < hs
        return first, jnp.logical_not(first)

    def stack_heads(x):
        m0, m1 = head_masks(*x.shape)
        zero = jnp.zeros_like(x)
        return jnp.concatenate([jnp.where(m0, x, zero), jnp.where(m1, x, zero)], axis=0)

    row = lax.broadcasted_iota(jnp.int32, (C, LANES), 0)
    col = lax.broadcasted_iota(jnp.int32, (C, LANES), 1) % hs
    strict, incl = row > col, row >= col
    sq_r = lax.broadcasted_iota(jnp.int32, (LANES, LANES), 0)
    sq_c = lax.broadcasted_iota(jnp.int32, (LANES, LANES), 1)
    same_head = (sq_r // hs) == (sq_c // hs)
    eye = sq_r == sq_c
    levels = C.bit_length() - 1
    dot = functools.partial(jnp.dot, preferred_element_type=F32)
    P = range(pairs)

    def step(c, carry):
        sl = pl.ds(pl.multiple_of(c * C, C), C)
        at = [at_ref[0, p, sl, :] for p in P]
        rt = [rt_ref[0, p, sl, :] for p in P]
        v = [v_ref[0, p, sl, :] for p in P]
        vs = [stack_heads(v[p]) for p in P]
        lhs = [jnp.concatenate([at[p], rt[p]], axis=0) for p in P]
        rhs = [jnp.concatenate([stack_heads(bt_ref[0, p, sl, :]), stack_heads(kt_ref[0, p, sl, :])], axis=0)
               for p in P]
        abk = [_dot_nt(lhs[p], rhs[p]) for p in P]
        n = [jnp.where(strict, abk[p][:C, :LANES], 0.0) for p in P]
        a_rb = [jnp.where(incl, abk[p][C:, :LANES], 0.0) for p in P]
        a_ak = [jnp.where(strict, abk[p][:C, LANES:], 0.0) for p in P]
        a_rk = [jnp.where(incl, abk[p][C:, LANES:], 0.0) for p in P]
        av = [dot(a_ak[p].astype(BF16), vs[p]) for p in P]
        x = [jnp.concatenate([at[p].astype(F32), av[p]], axis=1) for p in P]
        for lvl in range(levels):
            last = lvl + 1 == levels
            for p in P:
                z = x[p] if last else jnp.concatenate([n[p], x[p]], axis=1)
                out = dot(n[p].astype(BF16), stack_heads(z.astype(BF16)))
                if last:
                    x[p] = x[p] + out
                else:
                    n[p] = out[:, :LANES]
                    x[p] = x[p] + out[:, LANES:]
        xb = [x[p].astype(BF16) for p in P]
        zeros_c = jnp.zeros((C, LANES), BF16)
        bx = [_dot_tn(jnp.concatenate([bp_ref[0, p, sl, :], kp_ref[0, p, sl, :]], axis=0),
                      jnp.concatenate([xb[p], jnp.concatenate([zeros_c, v[p]], axis=1)], axis=0)) for p in P]
        zeros_2c = jnp.zeros((2 * C, LANES), BF16)
        rx = [dot(jnp.concatenate([a_rb[p], a_rk[p]], axis=1).astype(BF16),
                  jnp.concatenate([stack_heads(xb[p]), jnp.concatenate([zeros_2c, vs[p]], axis=1)], axis=0))
              for p in P]
        for p in P:
            state = state_ref[p]
            wc = wc_ref[pl.ds(pl.multiple_of(c * C, C), 1), p * LANES:(p + 1) * LANES]
            trans = jnp.where(same_head, bx[p][:, :LANES], 0.0) + jnp.where(eye, wc, 0.0)
            inject = jnp.where(same_head, bx[p][:, LANES:], 0.0)
            r_eff = rt[p].astype(F32) + rx[p][:, :LANES]
            y_ref[0, p, sl, :] = _dot(r_eff, state) + rx[p][:, LANES:]
            state_ref[p] = _dot_x3(trans, state) + inject
        return carry

    lax.fori_loop(0, chunks, step, 0)


def _wkv(at, rt, bt, kt, bp, kp, v, wc, pairs=4, span=512):
    bsz, n_pairs, seq, _ = at.shape
    groups = n_pairs // pairs
    spec = pl.BlockSpec((1, pairs, span, LANES), lambda g, t: (g // groups, g % groups, t, 0))
    wc_spec = pl.BlockSpec((span, pairs * LANES), lambda g, t: ((g // groups) * (seq // span) + t, g % groups))
    return pl.pallas_call(
        functools.partial(_wkv_body, pairs=pairs, chunks=span // WKV_CHUNK),
        grid=(bsz * groups, seq // span),
        in_specs=[spec] * 7 + [wc_spec],
        out_specs=spec,
        out_shape=jax.ShapeDtypeStruct((bsz, n_pairs, seq, LANES), F32),
        scratch_shapes=[pltpu.VMEM((pairs, LANES, LANES), F32)],
        compiler_params=_cparams(2),
        name="wkv7",
    )(at, rt, bt, kt, bp, kp, v, wc)


def _post_pool_body(y_ref, bonus_ref, g_ref, u_ref, lnw_ref, lnb_ref, sel_ref, selt_ref,
                    pw_ref, ps_ref, ya_o, yb_o, ext_ref, *, tiles_per_seq, tm, group_width):
    sel, selt = sel_ref[...], selt_ref[...]
    inv_n = 1.0 / HEAD_SIZE
    y = jnp.concatenate([y_ref[0, p] for p in range(y_ref.shape[1])], axis=1)
    mean = _head_sum(y, sel, selt) * inv_n
    yc = y - mean
    var = _head_sum(yc * yc, sel, selt) * inv_n
    yn = yc * lax.rsqrt(var + GN_EPS) * lnw_ref[...] + lnb_ref[...]
    ya_o[...] = ((yn + bonus_ref[...]) * g_ref[...]).astype(ya_o.dtype)

    seq_tile = pl.program_id(0) % tiles_per_seq

    @pl.when(seq_tile == 0)
    def _():
        ext_ref[0:POOL_HALO, :] = jnp.zeros((POOL_HALO, ext_ref.shape[1]), F32)

    u = u_ref[...]
    ext_ref[POOL_HALO:POOL_HALO + tm, :] = u
    t_pos = seq_tile * tm + lax.broadcasted_iota(jnp.int32, (tm, group_width), 0)
    for gi, w in enumerate(POOL_WINDOWS):
        lanes = slice(gi * group_width, (gi + 1) * group_width)
        acc = u[:, lanes]
        for j in range(1, w):
            acc = acc + ext_ref[POOL_HALO - j:POOL_HALO - j + tm, lanes]
        count = jnp.minimum(t_pos + 1, w).astype(F32)
        mixed = acc / count - u[:, lanes]
        yg = _dot_x3(mixed, pw_ref[gi])
        yb_o[:, lanes] = (yg * ps_ref[:, lanes]).astype(yb_o.dtype)
    ext_ref[0:POOL_HALO, :] = u[tm - POOL_HALO:, :]


def _post_pool(y, bonus, g, p_all, pool_blk, consts, seq, tm=256):
    m, R = bonus.shape
    lnw, lnb, sel, selt, pw, ps = consts
    tiles_per_seq = seq // tm
    row = lambda i: (i, 0)
    fixed2 = lambda i: (0, 0)
    in_specs = [pl.BlockSpec((1, y.shape[1], tm, LANES), lambda i: (i // tiles_per_seq, 0, i % tiles_per_seq, 0)),
                pl.BlockSpec((tm, R), row), pl.BlockSpec((tm, R), row),
                pl.BlockSpec((tm, R), lambda i: (i, pool_blk))]
    in_specs += [pl.BlockSpec(c.shape, fixed2) for c in (lnw, lnb, sel, selt)]
    in_specs += [pl.BlockSpec(pw.shape, lambda i: (0, 0, 0)), pl.BlockSpec(ps.shape, fixed2)]
    out = jax.ShapeDtypeStruct((m, R), BF16)
    return pl.pallas_call(
        functools.partial(_post_pool_body, tiles_per_seq=tiles_per_seq, tm=tm, group_width=R // len(POOL_WINDOWS)),
        grid=(m // tm,),
        in_specs=in_specs,
        out_specs=[pl.BlockSpec((tm, R), row)] * 2,
        out_shape=[out, out],
        scratch_shapes=[pltpu.VMEM((POOL_HALO + tm, R), F32)],
        compiler_params=_cparams(1),
        name="post_pool",
    )(y, bonus, g, p_all, lnw, lnb, sel, selt, pw, ps)


def _merge_body(ya_ref, yb_ref, pa_ref, pb_ref, ga_ref, gb_ref, o_ref):
    ma = jnp.dot(ya_ref[...], pa_ref[...], preferred_element_type=F32)
    mb = jnp.dot(yb_ref[...], pb_ref[...], preferred_element_type=F32)
    o_ref[...] = (_sigmoid(ga_ref[...]) * ma + _sigmoid(gb_ref[...]) * mb).astype(o_ref.dtype)


def _merge(ya, yb, pa, pb, p_all, ga_col, gb_col, tm=512, tn=512):
    m, R = ya.shape
    d = pa.shape[1]
    ga_blk, gb_blk = ga_col // tn, gb_col // tn
    return pl.pallas_call(
        _merge_body,
        grid=(d // tn, m // tm),
        in_specs=[pl.BlockSpec((tm, R), lambda j, i: (i, 0)),
                  pl.BlockSpec((tm, R), lambda j, i: (i, 0)),
                  pl.BlockSpec((R, tn), lambda j, i: (0, j)),
                  pl.BlockSpec((R, tn), lambda j, i: (0, j)),
                  pl.BlockSpec((tm, tn), lambda j, i: (i, ga_blk + j)),
                  pl.BlockSpec((tm, tn), lambda j, i: (i, gb_blk + j))],
        out_specs=pl.BlockSpec((tm, tn), lambda j, i: (i, j)),
        out_shape=jax.ShapeDtypeStruct((m, d), BF16),
        compiler_params=_cparams(2),
        name="merge",
    )(ya, yb, pa, pb, p_all, p_all)


def _router_body(x_ref, g_ref, wr_ref, h_o, idx_o, gate_o):
    x = x_ref[...]
    inv = lax.rsqrt(jnp.mean(x * x, axis=-1, keepdims=True) + NORM_EPS)
    h = x * inv * g_ref[...]
    h_o[...] = h
    logits = _dot_x3(h, wr_ref[...])
    lane = lax.broadcasted_iota(jnp.int32, logits.shape, 1)
    neg_inf = jnp.float32(-jnp.inf)
    l1 = jnp.where(lane < N_EXPERTS, logits, neg_inf)
    m1 = jnp.max(l1, axis=-1, keepdims=True)
    i1 = jnp.min(jnp.where(l1 == m1, lane, LANES), axis=-1, keepdims=True)
    l2 = jnp.where(lane == i1, neg_inf, l1)
    m2 = jnp.max(l2, axis=-1, keepdims=True)
    i2 = jnp.min(jnp.where(l2 == m2, lane, LANES), axis=-1, keepdims=True)
    e = jnp.exp(m2 - m1)
    g1 = 1.0 / (1.0 + e)
    idx_o[...] = jnp.where(lane == 0, i1, jnp.where(lane == 1, i2, 0))
    gate_o[...] = jnp.where(lane == 0, g1, jnp.where(lane == 1, e * g1, 0.0))


def _router(x, g, wr_pad, tm=256):
    m, d = x.shape
    row = lambda i: (i, 0)
    return pl.pallas_call(
        _router_body,
        grid=(m // tm,),
        in_specs=[pl.BlockSpec((tm, d), row), pl.BlockSpec((1, d), lambda i: (0, 0)),
                  pl.BlockSpec(wr_pad.shape, lambda i: (0, 0))],
        out_specs=[pl.BlockSpec((tm, d), row), pl.BlockSpec((tm, LANES), row), pl.BlockSpec((tm, LANES), row)],
        out_shape=[jax.ShapeDtypeStruct((m, d), F32), jax.ShapeDtypeStruct((m, LANES), jnp.int32),
                   jax.ShapeDtypeStruct((m, LANES), F32)],
        compiler_params=_cparams(1),
        name="router",
    )(x, g.reshape(1, d), wr_pad)


def _gather_body(tok_ref, h_hbm, o_ref, buf_ref, sem, *, tg):
    base = pl.program_id(0) * tg

    def row_copy(i):
        return pltpu.make_async_copy(h_hbm.at[pl.ds(tok_ref[base + i], 1), :], buf_ref.at[pl.ds(i, 1), :], sem)

    def start(i, c):
        row_copy(i).start()
        return c

    def wait(i, c):
        row_copy(i).wait()
        return c

    lax.fori_loop(0, tg, start, 0)
    lax.fori_loop(0, tg, wait, 0)
    o_ref[...] = buf_ref[...].astype(o_ref.dtype)


def _gather_rows(tok, h, tg=256):
    p = tok.shape[0]
    d = h.shape[1]
    return pl.pallas_call(
        functools.partial(_gather_body, tg=tg),
        grid_spec=pltpu.PrefetchScalarGridSpec(
            num_scalar_prefetch=1,
            grid=(p // tg,),
            in_specs=[pl.BlockSpec(memory_space=pl.ANY)],
            out_specs=pl.BlockSpec((tg, d), lambda i, tok: (i, 0)),
            scratch_shapes=[pltpu.VMEM((tg, d), F32), pltpu.SemaphoreType.DMA(())]),
        out_shape=jax.ShapeDtypeStruct((p, d), BF16),
        compiler_params=_cparams(1),
        name="moe_gather",
    )(tok, h)


def _grouped_body(meta_ref, *refs, body):
    @pl.when(pl.program_id(0) < meta_ref[0])
    def _():
        body(*refs)

    @pl.when(pl.program_id(0) >= meta_ref[0])
    def _():
        refs[-1][...] = jnp.zeros_like(refs[-1])


def _grouped_matmul(meta, a, ws, *, tm, tn, out_dtype):
    p, k = a.shape
    n = ws[0].shape[2]
    nj = n // tn

    def blk(b, meta):
        return jnp.minimum(b, meta[0] - 1)

    def col(b, j, meta):
        return jnp.where(b < meta[0], j, nj - 1)

    a_spec = pl.BlockSpec((tm, k), lambda b, j, meta: (blk(b, meta), 0))
    w_spec = pl.BlockSpec((None, k, tn), lambda b, j, meta: (meta[1 + blk(b, meta)], 0, col(b, j, meta)))
    o_spec = pl.BlockSpec((tm, tn), lambda b, j, meta: (b, j))
    body = _mm_swiglu_body if len(ws) == 2 else _mm_plain_body
    return pl.pallas_call(
        functools.partial(_grouped_body, body=body),
        grid_spec=pltpu.PrefetchScalarGridSpec(
            num_scalar_prefetch=1,
            grid=(p // tm, nj),
            in_specs=[a_spec] + [w_spec] * len(ws),
            out_specs=o_spec),
        out_shape=jax.ShapeDtypeStruct((p, n), out_dtype),
        compiler_params=_cparams(2),
        name="moe_matmul",
    )(meta, a, *ws)


def _combine_body(d0_ref, d1_ref, x_ref, gate_ref, g_ref, y_hbm, o_ref, buf0, buf1, sem, *, tg, apply_norm):
    base = pl.program_id(0) * tg

    def row_copy(i, d_ref, buf):
        return pltpu.make_async_copy(y_hbm.at[pl.ds(d_ref[base + i], 1), :], buf.at[pl.ds(i, 1), :], sem)

    def start(i, c):
        row_copy(i, d0_ref, buf0).start()
        row_copy(i, d1_ref, buf1).start()
        return c

    def wait(i, c):
        row_copy(i, d0_ref, buf0).wait()
        row_copy(i, d1_ref, buf1).wait()
        return c

    lax.fori_loop(0, tg, start, 0)
    lax.fori_loop(0, tg, wait, 0)
    gate = gate_ref[...]
    x = x_ref[...] + gate[:, 0:1] * buf0[...] + gate[:, 1:2] * buf1[...]
    if apply_norm:
        x = x * lax.rsqrt(jnp.mean(x * x, axis=-1, keepdims=True) + NORM_EPS) * g_ref[...]
    o_ref[...] = x


def _combine(d0, d1, x, gate, g, ybuf, apply_norm, tg=256):
    m, d = x.shape
    row = lambda i, d0, d1: (i, 0)
    return pl.pallas_call(
        functools.partial(_combine_body, tg=tg, apply_norm=apply_norm),
        grid_spec=pltpu.PrefetchScalarGridSpec(
            num_scalar_prefetch=2,
            grid=(m // tg,),
            in_specs=[pl.BlockSpec((tg, d), row), pl.BlockSpec((tg, LANES), row),
                      pl.BlockSpec((1, d), lambda i, d0, d1: (0, 0)),
                      pl.BlockSpec(memory_space=pl.ANY)],
            out_specs=pl.BlockSpec((tg, d), row),
            scratch_shapes=[pltpu.VMEM((tg, d), F32), pltpu.VMEM((tg, d), F32), pltpu.SemaphoreType.DMA(())]),
        out_shape=jax.ShapeDtypeStruct((m, d), F32),
        compiler_params=_cparams(1),
        name="moe_combine",
    )(d0, d1, x, gate, g.reshape(1, d), ybuf)


def _moe_layout(top_e, tm):
    n_tok = top_e.shape[0]
    n_slot = n_tok * TOP_K
    flat_e = top_e.reshape(n_slot)
    onehot = (flat_e[:, None] == jnp.arange(N_EXPERTS, dtype=jnp.int32)[None, :]).astype(jnp.int32)
    csum = jnp.cumsum(onehot, axis=0)
    rank = jnp.sum((csum - onehot) * onehot, axis=1)
    counts = csum[-1]
    padded = (counts + tm - 1) // tm * tm
    pend = jnp.cumsum(padded)
    dest = (pend - padded)[flat_e] + rank
    n_blocks = -(-n_slot // tm) + N_EXPERTS
    tok = jnp.zeros((n_blocks * tm,), jnp.int32).at[dest].set(jnp.arange(n_slot, dtype=jnp.int32) // TOP_K)
    blk_e = jnp.minimum(jnp.searchsorted(pend, jnp.arange(n_blocks, dtype=jnp.int32) * tm, side="right"),
                        N_EXPERTS - 1).astype(jnp.int32)
    meta = jnp.concatenate([(pend[-1] // tm).astype(jnp.int32)[None], blk_e])
    dest = dest.reshape(n_tok, TOP_K).astype(jnp.int32)
    return tok, meta, dest[:, 0], dest[:, 1]


def kernel(x, norm_mix, w_in, shift_mu, decay_w0, decay_up, iclr_a0, iclr_up, outgate_up, k_k, k_a, r_k, lnx_w, lnx_b, vres_down, vres_mu, vres_up, vres_v0, pool_w, pool_scale, proj_a, proj_b, w_o, norm_ffn, ffn_gate, ffn_up, ffn_down, router, moe_gate, moe_up, moe_down, norm_final):
    bsz, seq, d = x.shape
    depth = w_in.shape[0]
    R = decay_w0.shape[1]
    dl, il, gl = decay_up.shape[1], iclr_up.shape[1], outgate_up.shape[1]
    vl = vres_up.shape[1]
    assert (dl, il, vl, gl) == (96, 96, 64, 256) and 3 * R + dl + il + gl == shift_mu.shape[1]
    m = bsz * seq
    x = x.reshape(m, d)
    row = lambda t: t.reshape(1, -1)

    head_of_lane = jnp.arange(R, dtype=jnp.int32) // HEAD_SIZE
    sel = (head_of_lane[:, None] == jnp.arange(LANES, dtype=jnp.int32)[None, :]).astype(BF16)
    sel_t = sel.T
    t_idx = jnp.arange(PREP_ROWS, dtype=jnp.int32)
    same_chunk = (t_idx[:, None] // WKV_CHUNK) == (t_idx[None, :] // WKV_CHUNK)
    ones_bd = same_chunk.astype(BF16)
    tri_bd = (same_chunk & (t_idx[None, :] <= t_idx[:, None])).astype(BF16)

    v_first = None
    for l in range(depth):
        w = w_in[l]
        c0 = 3 * R
        c1, c2, c3 = c0 + dl, c0 + dl + il, c0 + dl + il + gl
        vd_w = vres_down[l - 1] if l > 0 else jnp.zeros((d, vl), F32)
        vd_mu = vres_mu[l - 1] if l > 0 else jnp.zeros((vl,), F32)
        w_all = jnp.concatenate([w[:, :c0], w[:, c3:], w[:, c0:c2], vd_w, w[:, c2:c3]], axis=1).astype(BF16)
        mu = shift_mu[l]
        mu_rkv = row(mu[:c0])
        mu_lora = row(jnp.concatenate([mu[c0:c2], vd_mu, mu[c2:c3]]))
        pool_col = c0
        ga_col = c0 + R
        gb_col = ga_col + d
        zeros = lambda n: jnp.zeros((n, R), F32)
        dup = jnp.concatenate([decay_up[l], zeros(128 - dl)], axis=0)
        iup = jnp.concatenate([zeros(dl), iclr_up[l], zeros(256 - dl - il)], axis=0)
        gup = outgate_up[l].astype(BF16)

        h = _rmsnorm(x, norm_mix[l], BF16)
        p_all = _matmul(h, [w_all], tm=1024, tn=512, out_dtype=F32)

        consts = [mu_rkv, mu_lora, row(decay_w0[l]), row(iclr_a0[l]), row(k_k[l]), row(k_a[l]), row(r_k[l]),
                  dup, iup, gup, sel, sel_t, tri_bd, ones_bd]
        vres = None
        if l > 0:
            vup = jnp.concatenate([zeros(128 - vl), vres_up[l - 1]], axis=0).astype(BF16)
            vres = (v_first, row(vres_v0[l - 1]), vup)
        emit_v = l == 0 and depth > 1
        outs = _rwkv_prep(p_all, bsz, seq, R, consts, vres, emit_v)
        wc, bonus, g = outs[7:10]
        if emit_v:
            v_first = outs[10]
        y = _wkv(*outs[:7], wc)

        ya, yb = _post_pool(y, bonus, g, p_all, pool_col // R,
                            [row(lnx_w[l]), row(lnx_b[l]), sel, sel_t, pool_w[l], row(pool_scale[l])], seq)
        merged = _merge(ya, yb, proj_a[l].astype(BF16), proj_b[l].astype(BF16), p_all, ga_col, gb_col)
        x = _matmul(merged, [w_o[l].astype(BF16)], tm=1024, tn=512, out_dtype=F32, residual=x)

        i = l // 2
        if l % 2 == 0:
            h2 = _rmsnorm(x, norm_ffn[l], BF16)
            hidden = _matmul(h2, [ffn_gate[i].astype(BF16), ffn_up[i].astype(BF16)], tm=1024, tn=512,
                             out_dtype=BF16)
            x = _matmul(hidden, [ffn_down[i].astype(BF16)], tm=512, tn=512, out_dtype=F32, residual=x)
            if l == depth - 1:
                x = _rmsnorm(x, norm_final, F32)
        else:
            tm_e = 512
            wr_pad = jnp.concatenate([router[i], jnp.zeros((d, LANES - N_EXPERTS), F32)], axis=1)
            h2, idx, gate = _router(x, norm_ffn[l], wr_pad)
            tok, meta, d0, d1 = _moe_layout(idx[:, :TOP_K], tm_e)
            xs = _gather_rows(tok, h2)
            hidden = _grouped_matmul(meta, xs, [moe_gate[i].astype(BF16), moe_up[i].astype(BF16)],
                                     tm=tm_e, tn=512, out_dtype=BF16)
            ybuf = _grouped_matmul(meta, hidden, [moe_down[i].astype(BF16)], tm=tm_e, tn=512, out_dtype=F32)
            x = _combine(d0, d1, x, gate, norm_final, ybuf, apply_norm=l == depth - 1)
    return x.reshape(bsz, seq, d)
```

```python
import functools

import jax
import jax.numpy as jnp
from jax import lax
from jax.experimental import pallas as pl
from jax.experimental.pallas import tpu as pltpu

F32 = jnp.float32
BF16 = jnp.bfloat16

HEAD_SIZE = 64
POOL_WINDOWS = (2, 4, 8, 16)
N_EXPERTS = 8
TOP_K = 2
NORM_EPS = 1e-6
GN_EPS = 64e-5
LANES = 128
VMEM_LIMIT_BYTES = 56 * 1024 * 1024
WKV_CHUNK = 64
POOL_HALO = 16
PREP_ROWS = 256


def _cparams(n_axes):
    return pltpu.CompilerParams(dimension_semantics=("arbitrary",) * n_axes,
                                vmem_limit_bytes=VMEM_LIMIT_BYTES)


def _dot(a, b):
    return jnp.dot(a.astype(BF16), b.astype(BF16), preferred_element_type=F32)


def _split(x):
    hi = x.astype(BF16)
    lo = (x - hi.astype(F32)).astype(BF16)
    return hi, lo


def _dot_x3(a, b):
    a_hi, a_lo = _split(a)
    b_hi, b_lo = _split(b)
    d = functools.partial(jnp.dot, preferred_element_type=F32)
    return d(a_hi, b_hi) + d(a_hi, b_lo) + d(a_lo, b_hi)


def _dot_lhs_split(a, b_bf16):
    a_hi, a_lo = _split(a)
    d = functools.partial(jnp.dot, preferred_element_type=F32)
    return d(a_hi, b_bf16) + d(a_lo, b_bf16)


def _dot_tn(a, b):
    return lax.dot_general(a.astype(BF16), b.astype(BF16), (((0,), (0,)), ((), ())),
                           preferred_element_type=F32)


def _dot_nt(a, b):
    return lax.dot_general(a.astype(BF16), b.astype(BF16), (((1,), (1,)), ((), ())),
                           preferred_element_type=F32)


def _sigmoid(x):
    return 1.0 / (1.0 + jnp.exp(-x))


def _rmsnorm_body(x_ref, g_ref, o_ref):
    x = x_ref[...]
    inv = lax.rsqrt(jnp.mean(x * x, axis=-1, keepdims=True) + NORM_EPS)
    o_ref[...] = (x * inv * g_ref[...]).astype(o_ref.dtype)


def _rmsnorm(x, g, out_dtype, tm=512):
    m, d = x.shape
    return pl.pallas_call(
        _rmsnorm_body,
        grid=(m // tm,),
        in_specs=[pl.BlockSpec((tm, d), lambda i: (i, 0)),
                  pl.BlockSpec((1, d), lambda i: (0, 0))],
        out_specs=pl.BlockSpec((tm, d), lambda i: (i, 0)),
        out_shape=jax.ShapeDtypeStruct((m, d), out_dtype),
        compiler_params=_cparams(1),
        name="rmsnorm",
    )(x, g.reshape(1, d))


def _cast_weight_tiles(w_refs, wb_refs):
    @pl.when(pl.program_id(1) == 0)
    def _():
        for w_ref, wb_ref in zip(w_refs, wb_refs):
            wb_ref[...] = w_ref[...].astype(BF16)


def _mm_plain_body(a_ref, w_ref, o_ref, wb_ref):
    _cast_weight_tiles([w_ref], [wb_ref])
    o_ref[...] = jnp.dot(a_ref[...], wb_ref[...], preferred_element_type=F32).astype(o_ref.dtype)


def _mm_swiglu_body(a_ref, wg_ref, wu_ref, o_ref, wgb_ref, wub_ref):
    _cast_weight_tiles([wg_ref, wu_ref], [wgb_ref, wub_ref])
    a = a_ref[...]
    g = jnp.dot(a, wgb_ref[...], preferred_element_type=F32)
    u = jnp.dot(a, wub_ref[...], preferred_element_type=F32)
    o_ref[...] = (g * _sigmoid(g) * u).astype(o_ref.dtype)


def _mm_residual_body(a_ref, w_ref, res_ref, o_ref, wb_ref):
    _cast_weight_tiles([w_ref], [wb_ref])
    o_ref[...] = res_ref[...] + jnp.dot(a_ref[...], wb_ref[...], preferred_element_type=F32)


def _matmul(a, ws, *, tm, tn, out_dtype, residual=None):
    m, k = a.shape
    n = ws[0].shape[1]
    a_spec = pl.BlockSpec((tm, k), lambda j, i: (i, 0))
    w_spec = pl.BlockSpec((k, tn), lambda j, i: (0, j))
    o_spec = pl.BlockSpec((tm, tn), lambda j, i: (i, j))
    if len(ws) == 2:
        body, in_specs, args = _mm_swiglu_body, [a_spec, w_spec, w_spec], (a, ws[0], ws[1])
    elif residual is not None:
        body, in_specs, args = _mm_residual_body, [a_spec, w_spec, o_spec], (a, ws[0], residual)
    else:
        body, in_specs, args = _mm_plain_body, [a_spec, w_spec], (a, ws[0])
    return pl.pallas_call(
        body,
        grid=(n // tn, m // tm),
        in_specs=in_specs,
        out_specs=o_spec,
        out_shape=jax.ShapeDtypeStruct((m, n), out_dtype),
        scratch_shapes=[pltpu.VMEM((k, tn), BF16)] * len(ws),
        compiler_params=_cparams(2),
        name="matmul",
    )(*args)


def _head_sum(x, sel, sel_t):
    s = _dot_lhs_split(x, sel)
    return _dot_lhs_split(s, sel_t)


def _token_shift(p, carry_ref, mu):
    rows = p.shape[0]
    row = lax.broadcasted_iota(jnp.int32, p.shape, 0)
    prev = jnp.where(row == 0, carry_ref[...], pltpu.roll(p, 1, axis=0))
    carry_ref[...] = p[rows - 1:rows, :]
    return p + (prev - p) * mu


def _prep_body(*refs, tiles_per_seq, has_vres, emit_v, width):
    (rkv_ref, lora_ref, mu_rkv_ref, mu_lora_ref, w0_ref, a0_ref, kk_ref, ka_ref, rk_ref,
     dup_ref, iup_ref, gup_ref, sel_ref, selt_ref, tri_ref, ones_ref) = refs[:16]
    pos = 16
    if has_vres:
        vfirst_ref, v0_ref, vup_ref = refs[pos:pos + 3]
        pos += 3
    at_o, rt_o, bt_o, kt_o, bp_o, kp_o, vb_o, wc_o, bonus_o, g_o = refs[pos:pos + 10]
    pos += 10
    if emit_v:
        v_o = refs[pos]
        pos += 1
    carry_rkv, carry_lora = refs[pos:]
    R = width

    @pl.when(pl.program_id(0) % tiles_per_seq == 0)
    def _():
        carry_rkv[...] = jnp.zeros_like(carry_rkv)
        carry_lora[...] = jnp.zeros_like(carry_lora)

    z = _token_shift(rkv_ref[...], carry_rkv, mu_rkv_ref[...])
    zl = _token_shift(lora_ref[...], carry_lora, mu_lora_ref[...])
    r, k, v = z[:, :R], z[:, R:2 * R], z[:, 2 * R:]

    decay_logit = w0_ref[...] + _dot_x3(jnp.tanh(zl[:, :128]), dup_ref[...])
    neg = -decay_logit
    softplus = jnp.maximum(neg, 0.0) + jnp.log(1.0 + jnp.exp(-jnp.abs(neg)))
    lw = -jnp.exp(-softplus - 0.5)
    a = _sigmoid(a0_ref[...] + _dot_x3(zl[:, :256], iup_ref[...]))
    g_o[...] = _dot(_sigmoid(zl[:, 256:]), gup_ref[...])
    if has_vres:
        mix = _sigmoid(v0_ref[...] + _dot(zl[:, 128:256], vup_ref[...]))
        v = v + (vfirst_ref[...] - v) * mix
    if emit_v:
        v_o[...] = v

    sel, selt = sel_ref[...], selt_ref[...]
    kk = k * kk_ref[...]
    norm = jnp.sqrt(_head_sum(kk * kk, sel, selt))
    kk = kk / jnp.maximum(norm, 1e-12)
    k = k * (1.0 + (a - 1.0) * ka_ref[...])
    bonus_o[...] = _head_sum(r * k * rk_ref[...], sel, selt) * v

    d = functools.partial(jnp.dot, preferred_element_type=F32)
    lw_hi = lw.astype(BF16)
    rest = lw - lw_hi.astype(F32)
    lw_mid = rest.astype(BF16)
    lw_lo = (rest - lw_mid.astype(F32)).astype(BF16)
    tri, ones = tri_ref[...], ones_ref[...]
    cum = d(tri, lw_hi) + d(tri, lw_mid) + d(tri, lw_lo)
    total = d(ones, lw_hi) + d(ones, lw_mid) + d(ones, lw_lo)
    w_inv = jnp.exp(-cum)
    w_rem = jnp.exp(total - cum)
    b = kk * a
    wc_o[...] = jnp.exp(total)
    pairs = R // LANES
    for o_ref, val in ((at_o, -kk * jnp.exp(cum - lw)), (rt_o, r * jnp.exp(cum)), (bt_o, b * w_inv),
                       (kt_o, k * w_inv), (bp_o, b * w_rem), (kp_o, k * w_rem), (vb_o, v)):
        val = val.astype(BF16)
        for p in range(pairs):
            o_ref[0, p] = val[:, p * LANES:(p + 1) * LANES]


def _rwkv_prep(p_all, bsz, seq, width, consts, vres, emit_v):
    tm = PREP_ROWS
    m = p_all.shape[0]
    R = width
    lora_w = 512
    lora_blk = (p_all.shape[1] - lora_w) // lora_w
    tiles_per_seq = seq // tm
    row = lambda i: (i, 0)
    fixed = lambda i: (0, 0)
    in_specs = [pl.BlockSpec((tm, 3 * R), row),
                pl.BlockSpec((tm, lora_w), lambda i: (i, lora_blk))]
    args = [p_all, p_all]
    for c in consts:
        in_specs.append(pl.BlockSpec(c.shape, fixed))
        args.append(c)
    if vres is not None:
        vfirst, v0, vup = vres
        in_specs += [pl.BlockSpec((tm, R), row), pl.BlockSpec(v0.shape, fixed), pl.BlockSpec(vup.shape, fixed)]
        args += [vfirst, v0, vup]
    pairs = R // LANES
    pm_spec = pl.BlockSpec((1, pairs, tm, LANES), lambda i: (i // tiles_per_seq, 0, i % tiles_per_seq, 0))
    pm_shape = jax.ShapeDtypeStruct((bsz, pairs, seq, LANES), BF16)
    nat_spec = pl.BlockSpec((tm, R), row)
    nat_shape = jax.ShapeDtypeStruct((m, R), F32)
    n_nat = 4 if emit_v else 3
    return pl.pallas_call(
        functools.partial(_prep_body, tiles_per_seq=tiles_per_seq, has_vres=vres is not None, emit_v=emit_v,
                          width=R),
        grid=(m // tm,),
        in_specs=in_specs,
        out_specs=[pm_spec] * 7 + [nat_spec] * n_nat,
        out_shape=[pm_shape] * 7 + [nat_shape] * n_nat,
        scratch_shapes=[pltpu.VMEM((1, 3 * R), F32), pltpu.VMEM((1, lora_w), F32)],
        compiler_params=_cparams(1),
        name="rwkv_prep",
    )(*args)


def _wkv_body(at_ref, rt_ref, bt_ref, kt_ref, bp_ref, kp_ref, v_ref, wc_ref, y_ref, state_ref, *, pairs, chunks):
    C = WKV_CHUNK
    hs = HEAD_SIZE

    @pl.when(pl.program_id(1) == 0)
    def _():
        state_ref[...] = jnp.zeros_like(state_ref)

    def head_masks(rows, width):
        lane = lax.broadcasted_iota(jnp.int32, (rows, width), 1)
        first = (lane % LANES) < hs
        return first, jnp.logical_not(first)

    def stack_heads(x):
        m0, m1 = head_masks(*x.shape)
        zero = jnp.zeros_like(x)
        return jnp.concatenate([jnp.where(m0, x, zero), jnp.where(m1, x, zero)], axis=0)

    row = lax.broadcasted_iota(jnp.int32, (C, LANES), 0)
    col = lax.broadcasted_iota(jnp.int32, (C, LANES), 1) % hs
    strict, incl = row > col, row >= col
    sq_r = lax.broadcasted_iota(jnp.int32, (LANES, LANES), 0)
    sq_c = lax.broadcasted_iota(jnp.int32, (LANES, LANES), 1)
    same_head = (sq_r // hs) == (sq_c // hs)
    eye = sq_r == sq_c
    levels = C.bit_length() - 1
    dot = functools.partial(jnp.dot, preferred_element_type=F32)
    P = range(pairs)

    def step(c, carry):
        sl = pl.ds(pl.multiple_of(c * C, C), C)
        at = [at_ref[0, p, sl, :] for p in P]
        rt = [rt_ref[0, p, sl, :] for p in P]
        v = [v_ref[0, p, sl, :] for p in P]
        vs = [stack_heads(v[p]) for p in P]
        lhs = [jnp.concatenate([at[p], rt[p]], axis=0) for p in P]
        rhs = [jnp.concatenate([stack_heads(bt_ref[0, p, sl, :]), stack_heads(kt_ref[0, p, sl, :])], axis=0)
               for p in P]
        abk = [_dot_nt(lhs[p], rhs[p]) for p in P]
        n = [jnp.where(strict, abk[p][:C, :LANES], 0.0) for p in P]
        a_rb = [jnp.where(incl, abk[p][C:, :LANES], 0.0) for p in P]
        a_ak = [jnp.where(strict, abk[p][:C, LANES:], 0.0) for p in P]
        a_rk = [jnp.where(incl, abk[p][C:, LANES:], 0.0) for p in P]
        av = [dot(a_ak[p].astype(BF16), vs[p]) for p in P]
        x = [jnp.concatenate([at[p].astype(F32), av[p]], axis=1) for p in P]
        for lvl in range(levels):
            last = lvl + 1 == levels
            for p in P:
                z = x[p] if last else jnp.concatenate([n[p], x[p]], axis=1)
                out = dot(n[p].astype(BF16), stack_heads(z.astype(BF16)))
                if last:
                    x[p] = x[p] + out
                else:
                    n[p] = out[:, :LANES]
                    x[p] = x[p] + out[:, LANES:]
        xb = [x[p].astype(BF16) for p in P]
        zeros_c = jnp.zeros((C, LANES), BF16)
        bx = [_dot_tn(jnp.concatenate([bp_ref[0, p, sl, :], kp_ref[0, p, sl, :]], axis=0),
                      jnp.concatenate([xb[p], jnp.concatenate([zeros_c, v[p]], axis=1)], axis=0)) for p in P]
        zeros_2c = jnp.zeros((2 * C, LANES), BF16)
        rx = [dot(jnp.concatenate([a_rb[p], a_rk[p]], axis=1).astype(BF16),
                  jnp.concatenate([stack_heads(xb[p]), jnp.concatenate([zeros_2c, vs[p]], axis=1)], axis=0))
              for p in P]
        for p in P:
            state = state_ref[p]
            wc = wc_ref[pl.ds(pl.multiple_of(c * C, C), 1), p * LANES:(p + 1) * LANES]
            trans = jnp.where(same_head, bx[p][:, :LANES], 0.0) + jnp.where(eye, wc, 0.0)
            inject = jnp.where(same_head, bx[p][:, LANES:], 0.0)
            r_eff = rt[p].astype(F32) + rx[p][:, :LANES]
            y_ref[0, p, sl, :] = _dot(r_eff, state) + rx[p][:, LANES:]
            state_ref[p] = _dot_x3(trans, state) + inject
        return carry

    lax.fori_loop(0, chunks, step, 0)


def _wkv(at, rt, bt, kt, bp, kp, v, wc, pairs=4, span=512):
    bsz, n_pairs, seq, _ = at.shape
    groups = n_pairs // pairs
    spec = pl.BlockSpec((1, pairs, span, LANES), lambda g, t: (g // groups, g % groups, t, 0))
    wc_spec = pl.BlockSpec((span, pairs * LANES), lambda g, t: ((g // groups) * (seq // span) + t, g % groups))
    return pl.pallas_call(
        functools.partial(_wkv_body, pairs=pairs, chunks=span // WKV_CHUNK),
        grid=(bsz * groups, seq // span),
        in_specs=[spec] * 7 + [wc_spec],
        out_specs=spec,
        out_shape=jax.ShapeDtypeStruct((bsz, n_pairs, seq, LANES), F32),
        scratch_shapes=[pltpu.VMEM((pairs, LANES, LANES), F32)],
        compiler_params=_cparams(2),
        name="wkv7",
    )(at, rt, bt, kt, bp, kp, v, wc)


def _post_pool_body(y_ref, bonus_ref, g_ref, u_ref, lnw_ref, lnb_ref, sel_ref, selt_ref,
                    pw_ref, ps_ref, ya_o, yb_o, ext_ref, *, tiles_per_seq, tm, group_width):
    sel, selt = sel_ref[...], selt_ref[...]
    inv_n = 1.0 / HEAD_SIZE
    y = jnp.concatenate([y_ref[0, p] for p in range(y_ref.shape[1])], axis=1)
    mean = _head_sum(y, sel, selt) * inv_n
    yc = y - mean
    var = _head_sum(yc * yc, sel, selt) * inv_n
    yn = yc * lax.rsqrt(var + GN_EPS) * lnw_ref[...] + lnb_ref[...]
    ya_o[...] = ((yn + bonus_ref[...]) * g_ref[...]).astype(ya_o.dtype)

    seq_tile = pl.program_id(0) % tiles_per_seq

    @pl.when(seq_tile == 0)
    def _():
        ext_ref[0:POOL_HALO, :] = jnp.zeros((POOL_HALO, ext_ref.shape[1]), F32)

    u = u_ref[...]
    ext_ref[POOL_HALO:POOL_HALO + tm, :] = u
    t_pos = seq_tile * tm + lax.broadcasted_iota(jnp.int32, (tm, group_width), 0)
    for gi, w in enumerate(POOL_WINDOWS):
        lanes = slice(gi * group_width, (gi + 1) * group_width)
        acc = u[:, lanes]
        for j in range(1, w):
            acc = acc + ext_ref[POOL_HALO - j:POOL_HALO - j + tm, lanes]
        count = jnp.minimum(t_pos + 1, w).astype(F32)
        mixed = acc / count - u[:, lanes]
        yg = _dot_x3(mixed, pw_ref[gi])
        yb_o[:, lanes] = (yg * ps_ref[:, lanes]).astype(yb_o.dtype)
    ext_ref[0:POOL_HALO, :] = u[tm - POOL_HALO:, :]


def _post_pool(y, bonus, g, p_all, pool_blk, consts, seq, tm=256):
    m, R = bonus.shape
    lnw, lnb, sel, selt, pw, ps = consts
    tiles_per_seq = seq // tm
    row = lambda i: (i, 0)
    fixed2 = lambda i: (0, 0)
    in_specs = [pl.BlockSpec((1, y.shape[1], tm, LANES), lambda i: (i // tiles_per_seq, 0, i % tiles_per_seq, 0)),
                pl.BlockSpec((tm, R), row), pl.BlockSpec((tm, R), row),
                pl.BlockSpec((tm, R), lambda i: (i, pool_blk))]
    in_specs += [pl.BlockSpec(c.shape, fixed2) for c in (lnw, lnb, sel, selt)]
    in_specs += [pl.BlockSpec(pw.shape, lambda i: (0, 0, 0)), pl.BlockSpec(ps.shape, fixed2)]
    out = jax.ShapeDtypeStruct((m, R), BF16)
    return pl.pallas_call(
        functools.partial(_post_pool_body, tiles_per_seq=tiles_per_seq, tm=tm, group_width=R // len(POOL_WINDOWS)),
        grid=(m // tm,),
        in_specs=in_specs,
        out_specs=[pl.BlockSpec((tm, R), row)] * 2,
        out_shape=[out, out],
        scratch_shapes=[pltpu.VMEM((POOL_HALO + tm, R), F32)],
        compiler_params=_cparams(1),
        name="post_pool",
    )(y, bonus, g, p_all, lnw, lnb, sel, selt, pw, ps)


def _merge_body(ya_ref, yb_ref, pa_ref, pb_ref, ga_ref, gb_ref, o_ref, pab_ref, pbb_ref):
    _cast_weight_tiles([pa_ref, pb_ref], [pab_ref, pbb_ref])
    ma = jnp.dot(ya_ref[...], pab_ref[...], preferred_element_type=F32)
    mb = jnp.dot(yb_ref[...], pbb_ref[...], preferred_element_type=F32)
    o_ref[...] = (_sigmoid(ga_ref[...]) * ma + _sigmoid(gb_ref[...]) * mb).astype(o_ref.dtype)


def _merge(ya, yb, pa, pb, p_all, ga_col, gb_col, tm=512, tn=512):
    m, R = ya.shape
    d = pa.shape[1]
    ga_blk, gb_blk = ga_col // tn, gb_col // tn
    return pl.pallas_call(
        _merge_body,
        grid=(d // tn, m // tm),
        in_specs=[pl.BlockSpec((tm, R), lambda j, i: (i, 0)),
                  pl.BlockSpec((tm, R), lambda j, i: (i, 0)),
                  pl.BlockSpec((R, tn), lambda j, i: (0, j)),
                  pl.BlockSpec((R, tn), lambda j, i: (0, j)),
                  pl.BlockSpec((tm, tn), lambda j, i: (i, ga_blk + j)),
                  pl.BlockSpec((tm, tn), lambda j, i: (i, gb_blk + j))],
        out_specs=pl.BlockSpec((tm, tn), lambda j, i: (i, j)),
        out_shape=jax.ShapeDtypeStruct((m, d), BF16),
        scratch_shapes=[pltpu.VMEM((R, tn), BF16)] * 2,
        compiler_params=_cparams(2),
        name="merge",
    )(ya, yb, pa, pb, p_all, p_all)


def _router_body(x_ref, g_ref, wr_ref, h_o, idx_o, gate_o):
    x = x_ref[...]
    inv = lax.rsqrt(jnp.mean(x * x, axis=-1, keepdims=True) + NORM_EPS)
    h = x * inv * g_ref[...]
    h_o[...] = h.reshape(h_o.shape)
    logits = _dot_x3(h, wr_ref[...])
    lane = lax.broadcasted_iota(jnp.int32, logits.shape, 1)
    neg_inf = jnp.float32(-jnp.inf)
    l1 = jnp.where(lane < N_EXPERTS, logits, neg_inf)
    m1 = jnp.max(l1, axis=-1, keepdims=True)
    i1 = jnp.min(jnp.where(l1 == m1, lane, LANES), axis=-1, keepdims=True)
    l2 = jnp.where(lane == i1, neg_inf, l1)
    m2 = jnp.max(l2, axis=-1, keepdims=True)
    i2 = jnp.min(jnp.where(l2 == m2, lane, LANES), axis=-1, keepdims=True)
    e = jnp.exp(m2 - m1)
    g1 = 1.0 / (1.0 + e)
    idx_o[...] = jnp.where(lane == 0, i1, jnp.where(lane == 1, i2, 0))
    gate_o[...] = jnp.where(lane == 0, g1, jnp.where(lane == 1, e * g1, 0.0))


def _router(x, g, wr_pad, tm=256):
    m, d = x.shape
    row = lambda i: (i, 0)
    return pl.pallas_call(
        _router_body,
        grid=(m // tm,),
        in_specs=[pl.BlockSpec((tm, d), row), pl.BlockSpec((1, d), lambda i: (0, 0)),
                  pl.BlockSpec(wr_pad.shape, lambda i: (0, 0))],
        out_specs=[pl.BlockSpec((tm, d // LANES, LANES), lambda i: (i, 0, 0)),
                   pl.BlockSpec((tm, LANES), row), pl.BlockSpec((tm, LANES), row)],
        out_shape=[jax.ShapeDtypeStruct((m, d // LANES, LANES), F32), jax.ShapeDtypeStruct((m, LANES), jnp.int32),
                   jax.ShapeDtypeStruct((m, LANES), F32)],
        compiler_params=_cparams(1),
        name="router",
    )(x, g.reshape(1, d), wr_pad)


def _gather_body(tok_ref, used_ref, h_hbm, o_ref, buf_ref, sems, *, tg, n_steps):
    i = pl.program_id(0)

    def row_copy(step, r):
        slot = step % 2
        return pltpu.make_async_copy(h_hbm.at[pl.ds(tok_ref[step * tg + r], 1)], buf_ref.at[slot, pl.ds(r, 1)],
                                     sems.at[slot])

    def fetch(step):
        @pl.when(jnp.logical_and(step < n_steps, step * tg < used_ref[0]))
        def _():
            def start(r, c):
                row_copy(step, r).start()
                return c
            lax.fori_loop(0, tg, start, 0)

    @pl.when(i == 0)
    def _():
        fetch(i)

    fetch(i + 1)

    @pl.when(i * tg < used_ref[0])
    def _():
        def wait(r, c):
            row_copy(i, r).wait()
            return c
        lax.fori_loop(0, tg, wait, 0)
        o_ref[...] = buf_ref[i % 2].reshape(o_ref.shape).astype(o_ref.dtype)

    @pl.when(i * tg >= used_ref[0])
    def _():
        o_ref[...] = jnp.zeros_like(o_ref)


def _gather_rows(tok, used_rows, h3, tg=256):
    p = tok.shape[0]
    _, nc, lanes = h3.shape
    n_steps = p // tg
    return pl.pallas_call(
        functools.partial(_gather_body, tg=tg, n_steps=n_steps),
        grid_spec=pltpu.PrefetchScalarGridSpec(
            num_scalar_prefetch=2,
            grid=(n_steps,),
            in_specs=[pl.BlockSpec(memory_space=pl.ANY)],
            out_specs=pl.BlockSpec((tg, nc * lanes), lambda i, tok, used: (i, 0)),
            scratch_shapes=[pltpu.VMEM((2, tg, nc, lanes), F32), pltpu.SemaphoreType.DMA((2,))]),
        out_shape=jax.ShapeDtypeStruct((p, nc * lanes), BF16),
        compiler_params=_cparams(1),
        name="moe_gather",
    )(tok, used_rows, h3)


MOE_ROWS = 1024
MOE_SUB = 256
MOE_FF_TILE = 256
PLAN_FIELDS = 7


def _experts_body(plan_ref, x_ref, wg_ref, wu_ref, wd_ref, o_ref, *, n_items):
    w = pl.program_id(0)
    j = pl.program_id(1)
    field = lambda f: plan_ref[f * n_items + w]
    lo, hi, zero, live = field(3), field(4), field(5), field(6)
    n_sub = MOE_ROWS // MOE_SUB

    @pl.when(jnp.logical_and(zero == 1, j == 0))
    def _():
        o_ref[...] = jnp.zeros_like(o_ref)

    @pl.when(live == 1)
    def _():
        wg = wg_ref[...].astype(BF16)
        wu = wu_ref[...].astype(BF16)
        wd = wd_ref[...].astype(BF16)

        def rows(sl):
            x = x_ref[sl, :]
            g = jnp.dot(x, wg, preferred_element_type=F32)
            u = jnp.dot(x, wu, preferred_element_type=F32)
            h = (g * _sigmoid(g) * u).astype(BF16)
            o_ref[sl, :] += jnp.dot(h, wd, preferred_element_type=F32)

        whole = jnp.logical_and(lo == 0, hi == n_sub)

        @pl.when(whole)
        def _():
            rows(slice(None))

        for s in range(n_sub):
            @pl.when(jnp.logical_and(jnp.logical_not(whole), jnp.logical_and(lo <= s, s < hi)))
            def _():
                rows(slice(s * MOE_SUB, (s + 1) * MOE_SUB))


def _experts(plan, xs, w_gate, w_up, w_down):
    p, d = xs.shape
    f = w_gate.shape[2]
    tn = MOE_FF_TILE
    nj = f // tn
    n_items = plan.shape[0] // PLAN_FIELDS
    fld = lambda plan, f_idx, w: plan[f_idx * n_items + w]
    col = lambda plan, w, j: jnp.where(fld(plan, 6, w) == 1, j, nj - 1)
    return pl.pallas_call(
        functools.partial(_experts_body, n_items=n_items),
        grid_spec=pltpu.PrefetchScalarGridSpec(
            num_scalar_prefetch=1,
            grid=(n_items, nj),
            in_specs=[pl.BlockSpec((MOE_ROWS, d), lambda w, j, plan: (fld(plan, 1, w), 0)),
                      pl.BlockSpec((None, d, tn), lambda w, j, plan: (fld(plan, 2, w), 0, col(plan, w, j))),
                      pl.BlockSpec((None, d, tn), lambda w, j, plan: (fld(plan, 2, w), 0, col(plan, w, j))),
                      pl.BlockSpec((None, tn, d), lambda w, j, plan: (fld(plan, 2, w), col(plan, w, j), 0))],
            out_specs=pl.BlockSpec((MOE_ROWS, d), lambda w, j, plan: (fld(plan, 0, w), 0))),
        out_shape=jax.ShapeDtypeStruct((p, d), F32),
        compiler_params=_cparams(2),
        name="moe_experts",
    )(plan, xs, w_gate, w_up, w_down)


def _combine_body(d0_ref, d1_ref, x_ref, gate_ref, g_ref, y_hbm, o_ref, buf0, buf1, sem, *, tg, apply_norm):
    base = pl.program_id(0) * tg

    def row_copy(i, d_ref, buf):
        return pltpu.make_async_copy(y_hbm.at[pl.ds(d_ref[base + i], 1), :], buf.at[pl.ds(i, 1), :], sem)

    def start(i, c):
        row_copy(i, d0_ref, buf0).start()
        row_copy(i, d1_ref, buf1).start()
        return c

    def wait(i, c):
        row_copy(i, d0_ref, buf0).wait()
        row_copy(i, d1_ref, buf1).wait()
        return c

    lax.fori_loop(0, tg, start, 0)
    lax.fori_loop(0, tg, wait, 0)
    gate = gate_ref[...]
    x = x_ref[...] + gate[:, 0:1] * buf0[...] + gate[:, 1:2] * buf1[...]
    if apply_norm:
        x = x * lax.rsqrt(jnp.mean(x * x, axis=-1, keepdims=True) + NORM_EPS) * g_ref[...]
    o_ref[...] = x


def _combine(d0, d1, x, gate, g, ybuf, apply_norm, tg=256):
    m, d = x.shape
    row = lambda i, d0, d1: (i, 0)
    return pl.pallas_call(
        functools.partial(_combine_body, tg=tg, apply_norm=apply_norm),
        grid_spec=pltpu.PrefetchScalarGridSpec(
            num_scalar_prefetch=2,
            grid=(m // tg,),
            in_specs=[pl.BlockSpec((tg, d), row), pl.BlockSpec((tg, LANES), row),
                      pl.BlockSpec((1, d), lambda i, d0, d1: (0, 0)),
                      pl.BlockSpec(memory_space=pl.ANY)],
            out_specs=pl.BlockSpec((tg, d), row),
            scratch_shapes=[pltpu.VMEM((tg, d), F32), pltpu.VMEM((tg, d), F32), pltpu.SemaphoreType.DMA(())]),
        out_shape=jax.ShapeDtypeStruct((m, d), F32),
        compiler_params=_cparams(1),
        name="moe_combine",
    )(d0, d1, x, gate, g.reshape(1, d), ybuf)


def _moe_plan(top_e):
    i32 = jnp.int32
    n_tok = top_e.shape[0]
    n_slot = n_tok * TOP_K
    flat_e = top_e.reshape(n_slot)
    onehot = (flat_e[:, None] == jnp.arange(N_EXPERTS, dtype=i32)[None, :]).astype(i32)
    csum = jnp.cumsum(onehot, axis=0)
    rank = jnp.sum((csum - onehot) * onehot, axis=1)
    counts = csum[-1]
    padded = (counts + MOE_SUB - 1) // MOE_SUB * MOE_SUB
    pend = jnp.cumsum(padded)
    pstart = pend - padded
    dest = pstart[flat_e] + rank
    n_blocks = -(-(n_slot + N_EXPERTS * MOE_SUB) // MOE_ROWS)
    tok = jnp.zeros((n_blocks * MOE_ROWS,), i32).at[dest].set(jnp.arange(n_slot, dtype=i32) // TOP_K)
    used_rows = pend[-1]
    used_blocks = (used_rows + MOE_ROWS - 1) // MOE_ROWS

    n_items = n_blocks + N_EXPERTS
    blk_lo = (jnp.arange(n_blocks, dtype=i32) * MOE_ROWS)[:, None]
    lo_row = jnp.maximum(pstart[None, :], blk_lo)
    hi_row = jnp.minimum(pend[None, :], blk_lo + MOE_ROWS)
    valid = (hi_row > lo_row).reshape(-1)
    n_live = jnp.sum(valid.astype(i32))
    pos = jnp.where(valid, jnp.cumsum(valid.astype(i32)) - 1, n_items)
    compact = lambda v: jnp.zeros((n_items,), i32).at[pos].set(v.reshape(-1).astype(i32), mode="drop")
    blk_c = compact(jnp.broadcast_to(jnp.arange(n_blocks, dtype=i32)[:, None], lo_row.shape))
    exp_c = compact(jnp.broadcast_to(jnp.arange(N_EXPERTS, dtype=i32)[None, :], lo_row.shape))
    lo_c = compact((lo_row - blk_lo) // MOE_SUB)
    hi_c = compact((hi_row - blk_lo) // MOE_SUB)
    first_c = compact(lo_row == blk_lo)

    w = jnp.arange(n_items, dtype=i32)
    live = w < n_live
    tail_blk = used_blocks + (w - n_live)
    last = n_live - 1
    out_blk = jnp.where(live, blk_c, jnp.minimum(tail_blk, n_blocks - 1))
    x_blk = jnp.where(live, blk_c, blk_c[last])
    expert = jnp.where(live, exp_c, exp_c[last])
    zero = jnp.where(live, first_c, (tail_blk < n_blocks).astype(i32))
    plan = jnp.concatenate([out_blk, x_blk, expert, lo_c, hi_c, zero, live.astype(i32)]).astype(i32)
    dest = dest.reshape(n_tok, TOP_K).astype(i32)
    return tok, used_rows.astype(i32).reshape(1), plan, dest[:, 0], dest[:, 1]


def kernel(x, norm_mix, w_in, shift_mu, decay_w0, decay_up, iclr_a0, iclr_up, outgate_up, k_k, k_a, r_k, lnx_w, lnx_b, vres_down, vres_mu, vres_up, vres_v0, pool_w, pool_scale, proj_a, proj_b, w_o, norm_ffn, ffn_gate, ffn_up, ffn_down, router, moe_gate, moe_up, moe_down, norm_final):
    bsz, seq, d = x.shape
    depth = w_in.shape[0]
    R = decay_w0.shape[1]
    dl, il, gl = decay_up.shape[1], iclr_up.shape[1], outgate_up.shape[1]
    vl = vres_up.shape[1]
    assert (dl, il, vl, gl) == (96, 96, 64, 256) and 3 * R + dl + il + gl == shift_mu.shape[1]
    m = bsz * seq
    x = x.reshape(m, d)
    row = lambda t: t.reshape(1, -1)

    head_of_lane = jnp.arange(R, dtype=jnp.int32) // HEAD_SIZE
    sel = (head_of_lane[:, None] == jnp.arange(LANES, dtype=jnp.int32)[None, :]).astype(BF16)
    sel_t = sel.T
    t_idx = jnp.arange(PREP_ROWS, dtype=jnp.int32)
    same_chunk = (t_idx[:, None] // WKV_CHUNK) == (t_idx[None, :] // WKV_CHUNK)
    ones_bd = same_chunk.astype(BF16)
    tri_bd = (same_chunk & (t_idx[None, :] <= t_idx[:, None])).astype(BF16)

    v_first = None
    for l in range(depth):
        w = w_in[l]
        c0 = 3 * R
        c1, c2, c3 = c0 + dl, c0 + dl + il, c0 + dl + il + gl
        vd_w = vres_down[l - 1] if l > 0 else jnp.zeros((d, vl), F32)
        vd_mu = vres_mu[l - 1] if l > 0 else jnp.zeros((vl,), F32)
        w_all = jnp.concatenate([w[:, :c0], w[:, c3:], w[:, c0:c2], vd_w, w[:, c2:c3]], axis=1)
        mu = shift_mu[l]
        mu_rkv = row(mu[:c0])
        mu_lora = row(jnp.concatenate([mu[c0:c2], vd_mu, mu[c2:c3]]))
        pool_col = c0
        ga_col = c0 + R
        gb_col = ga_col + d
        zeros = lambda n: jnp.zeros((n, R), F32)
        dup = jnp.concatenate([decay_up[l], zeros(128 - dl)], axis=0)
        iup = jnp.concatenate([zeros(dl), iclr_up[l], zeros(256 - dl - il)], axis=0)
        gup = outgate_up[l].astype(BF16)

        h = _rmsnorm(x, norm_mix[l], BF16)
        p_all = _matmul(h, [w_all], tm=1024, tn=512, out_dtype=F32)

        consts = [mu_rkv, mu_lora, row(decay_w0[l]), row(iclr_a0[l]), row(k_k[l]), row(k_a[l]), row(r_k[l]),
                  dup, iup, gup, sel, sel_t, tri_bd, ones_bd]
        vres = None
        if l > 0:
            vup = jnp.concatenate([zeros(128 - vl), vres_up[l - 1]], axis=0).astype(BF16)
            vres = (v_first, row(vres_v0[l - 1]), vup)
        emit_v = l == 0 and depth > 1
        outs = _rwkv_prep(p_all, bsz, seq, R, consts, vres, emit_v)
        wc, bonus, g = outs[7:10]
        if emit_v:
            v_first = outs[10]
        y = _wkv(*outs[:7], wc)

        ya, yb = _post_pool(y, bonus, g, p_all, pool_col // R,
                            [row(lnx_w[l]), row(lnx_b[l]), sel, sel_t, pool_w[l], row(pool_scale[l])], seq)
        merged = _merge(ya, yb, proj_a[l], proj_b[l], p_all, ga_col, gb_col)
        x = _matmul(merged, [w_o[l]], tm=1024, tn=512, out_dtype=F32, residual=x)

        i = l // 2
        if l % 2 == 0:
            h2 = _rmsnorm(x, norm_ffn[l], BF16)
            hidden = _matmul(h2, [ffn_gate[i], ffn_up[i]], tm=1024, tn=512, out_dtype=BF16)
            x = _matmul(hidden, [ffn_down[i]], tm=512, tn=512, out_dtype=F32, residual=x)
            if l == depth - 1:
                x = _rmsnorm(x, norm_final, F32)
        else:
            wr_pad = jnp.concatenate([router[i], jnp.zeros((d, LANES - N_EXPERTS), F32)], axis=1)
            h3, idx, gate = _router(x, norm_ffn[l], wr_pad)
            tok, used_rows, plan, d0, d1 = _moe_plan(idx[:, :TOP_K])
            xs = _gather_rows(tok, used_rows, h3)
            ybuf = _experts(plan, xs, moe_gate[i], moe_up[i], moe_down[i])
            x = _combine(d0, d1, x, gate, norm_final, ybuf, apply_norm=l == depth - 1)
    return x.reshape(bsz, seq, d)
```

```python
import functools

import jax
import jax.numpy as jnp
from jax import lax
from jax.experimental import pallas as pl
from jax.experimental.pallas import tpu as pltpu

F32 = jnp.float32
BF16 = jnp.bfloat16

HEAD_SIZE = 64
POOL_WINDOWS = (2, 4, 8, 16)
N_EXPERTS = 8
TOP_K = 2
NORM_EPS = 1e-6
GN_EPS = 64e-5
LANES = 128
VMEM_LIMIT_BYTES = 56 * 1024 * 1024
WKV_CHUNK = 64
POOL_HALO = 16
PREP_ROWS = 256


def _cparams(n_axes):
    return pltpu.CompilerParams(dimension_semantics=("arbitrary",) * n_axes,
                                vmem_limit_bytes=VMEM_LIMIT_BYTES)


def _dot(a, b):
    return jnp.dot(a.astype(BF16), b.astype(BF16), preferred_element_type=F32)


def _split(x):
    hi = x.astype(BF16)
    lo = (x - hi.astype(F32)).astype(BF16)
    return hi, lo


def _dot_x3(a, b):
    a_hi, a_lo = _split(a)
    b_hi, b_lo = _split(b)
    d = functools.partial(jnp.dot, preferred_element_type=F32)
    return d(a_hi, b_hi) + d(a_hi, b_lo) + d(a_lo, b_hi)


def _dot_lhs_split(a, b_bf16):
    a_hi, a_lo = _split(a)
    d = functools.partial(jnp.dot, preferred_element_type=F32)
    return d(a_hi, b_bf16) + d(a_lo, b_bf16)


def _dot_tn(a, b):
    return lax.dot_general(a.astype(BF16), b.astype(BF16), (((0,), (0,)), ((), ())),
                           preferred_element_type=F32)


def _dot_nt(a, b):
    return lax.dot_general(a.astype(BF16), b.astype(BF16), (((1,), (1,)), ((), ())),
                           preferred_element_type=F32)


def _sigmoid(x):
    return 1.0 / (1.0 + jnp.exp(-x))


def _rmsnorm_body(x_ref, g_ref, o_ref):
    x = x_ref[...]
    inv = lax.rsqrt(jnp.mean(x * x, axis=-1, keepdims=True) + NORM_EPS)
    o_ref[...] = (x * inv * g_ref[...]).astype(o_ref.dtype)


def _rmsnorm(x, g, out_dtype, tm=512):
    m, d = x.shape
    return pl.pallas_call(
        _rmsnorm_body,
        grid=(m // tm,),
        in_specs=[pl.BlockSpec((tm, d), lambda i: (i, 0)),
                  pl.BlockSpec((1, d), lambda i: (0, 0))],
        out_specs=pl.BlockSpec((tm, d), lambda i: (i, 0)),
        out_shape=jax.ShapeDtypeStruct((m, d), out_dtype),
        compiler_params=_cparams(1),
        name="rmsnorm",
    )(x, g.reshape(1, d))


def _cast_weight_tiles(w_refs, wb_refs):
    @pl.when(pl.program_id(1) == 0)
    def _():
        for w_ref, wb_ref in zip(w_refs, wb_refs):
            wb_ref[...] = w_ref[...].astype(BF16)


def _mm_plain_body(a_ref, w_ref, o_ref, wb_ref):
    _cast_weight_tiles([w_ref], [wb_ref])
    o_ref[...] = jnp.dot(a_ref[...], wb_ref[...], preferred_element_type=F32).astype(o_ref.dtype)


def _mm_swiglu_body(a_ref, wg_ref, wu_ref, o_ref, wgb_ref, wub_ref):
    _cast_weight_tiles([wg_ref, wu_ref], [wgb_ref, wub_ref])
    a = a_ref[...]
    g = jnp.dot(a, wgb_ref[...], preferred_element_type=F32)
    u = jnp.dot(a, wub_ref[...], preferred_element_type=F32)
    o_ref[...] = (g * _sigmoid(g) * u).astype(o_ref.dtype)


def _mm_residual_body(a_ref, w_ref, res_ref, o_ref, wb_ref):
    _cast_weight_tiles([w_ref], [wb_ref])
    o_ref[...] = res_ref[...] + jnp.dot(a_ref[...], wb_ref[...], preferred_element_type=F32)


def _matmul(a, ws, *, tm, tn, out_dtype, residual=None):
    m, k = a.shape
    n = ws[0].shape[1]
    a_spec = pl.BlockSpec((tm, k), lambda j, i: (i, 0))
    w_spec = pl.BlockSpec((k, tn), lambda j, i: (0, j))
    o_spec = pl.BlockSpec((tm, tn), lambda j, i: (i, j))
    if len(ws) == 2:
        body, in_specs, args = _mm_swiglu_body, [a_spec, w_spec, w_spec], (a, ws[0], ws[1])
    elif residual is not None:
        body, in_specs, args = _mm_residual_body, [a_spec, w_spec, o_spec], (a, ws[0], residual)
    else:
        body, in_specs, args = _mm_plain_body, [a_spec, w_spec], (a, ws[0])
    return pl.pallas_call(
        body,
        grid=(n // tn, m // tm),
        in_specs=in_specs,
        out_specs=o_spec,
        out_shape=jax.ShapeDtypeStruct((m, n), out_dtype),
        scratch_shapes=[pltpu.VMEM((k, tn), BF16)] * len(ws),
        compiler_params=_cparams(2),
        name="matmul",
    )(*args)


def _head_sum(x, sel, sel_t):
    s = _dot_lhs_split(x, sel)
    return _dot_lhs_split(s, sel_t)


def _token_shift(p, carry_ref, mu):
    rows = p.shape[0]
    row = lax.broadcasted_iota(jnp.int32, p.shape, 0)
    prev = jnp.where(row == 0, carry_ref[...], pltpu.roll(p, 1, axis=0))
    carry_ref[...] = p[rows - 1:rows, :]
    return p + (prev - p) * mu


def _prep_body(*refs, tiles_per_seq, has_vres, emit_v, width):
    (rkv_ref, lora_ref, mu_rkv_ref, mu_lora_ref, w0_ref, a0_ref, kk_ref, ka_ref, rk_ref,
     dup_ref, iup_ref, gup_ref, sel_ref, selt_ref, tri_ref, ones_ref) = refs[:16]
    pos = 16
    if has_vres:
        vfirst_ref, v0_ref, vup_ref = refs[pos:pos + 3]
        pos += 3
    at_o, rt_o, bt_o, kt_o, bp_o, kp_o, vb_o, wc_o, bonus_o, g_o = refs[pos:pos + 10]
    pos += 10
    if emit_v:
        v_o = refs[pos]
        pos += 1
    carry_rkv, carry_lora = refs[pos:]
    R = width

    @pl.when(pl.program_id(0) % tiles_per_seq == 0)
    def _():
        carry_rkv[...] = jnp.zeros_like(carry_rkv)
        carry_lora[...] = jnp.zeros_like(carry_lora)

    z = _token_shift(rkv_ref[...], carry_rkv, mu_rkv_ref[...])
    zl = _token_shift(lora_ref[...], carry_lora, mu_lora_ref[...])
    r, k, v = z[:, :R], z[:, R:2 * R], z[:, 2 * R:]

    decay_logit = w0_ref[...] + _dot_x3(jnp.tanh(zl[:, :128]), dup_ref[...])
    neg = -decay_logit
    softplus = jnp.maximum(neg, 0.0) + jnp.log(1.0 + jnp.exp(-jnp.abs(neg)))
    lw = -jnp.exp(-softplus - 0.5)
    a = _sigmoid(a0_ref[...] + _dot_x3(zl[:, :256], iup_ref[...]))
    g_o[...] = _dot(_sigmoid(zl[:, 256:]), gup_ref[...])
    if has_vres:
        mix = _sigmoid(v0_ref[...] + _dot(zl[:, 128:256], vup_ref[...]))
        v = v + (vfirst_ref[...] - v) * mix
    if emit_v:
        v_o[...] = v

    sel, selt = sel_ref[...], selt_ref[...]
    kk = k * kk_ref[...]
    norm = jnp.sqrt(_head_sum(kk * kk, sel, selt))
    kk = kk / jnp.maximum(norm, 1e-12)
    k = k * (1.0 + (a - 1.0) * ka_ref[...])
    bonus_o[...] = _head_sum(r * k * rk_ref[...], sel, selt) * v

    d = functools.partial(jnp.dot, preferred_element_type=F32)
    lw_hi = lw.astype(BF16)
    rest = lw - lw_hi.astype(F32)
    lw_mid = rest.astype(BF16)
    lw_lo = (rest - lw_mid.astype(F32)).astype(BF16)
    tri, ones = tri_ref[...], ones_ref[...]
    cum = d(tri, lw_hi) + d(tri, lw_mid) + d(tri, lw_lo)
    total = d(ones, lw_hi) + d(ones, lw_mid) + d(ones, lw_lo)
    w_inv = jnp.exp(-cum)
    w_rem = jnp.exp(total - cum)
    b = kk * a
    wc_o[...] = jnp.exp(total)
    pairs = R // LANES
    for o_ref, val in ((at_o, -kk * jnp.exp(cum - lw)), (rt_o, r * jnp.exp(cum)), (bt_o, b * w_inv),
                       (kt_o, k * w_inv), (bp_o, b * w_rem), (kp_o, k * w_rem), (vb_o, v)):
        val = val.astype(BF16)
        for p in range(pairs):
            o_ref[0, p] = val[:, p * LANES:(p + 1) * LANES]


def _rwkv_prep(p_all, bsz, seq, width, consts, vres, emit_v):
    tm = PREP_ROWS
    m = p_all.shape[0]
    R = width
    lora_w = 512
    lora_blk = (p_all.shape[1] - lora_w) // lora_w
    tiles_per_seq = seq // tm
    row = lambda i: (i, 0)
    fixed = lambda i: (0, 0)
    in_specs = [pl.BlockSpec((tm, 3 * R), row),
                pl.BlockSpec((tm, lora_w), lambda i: (i, lora_blk))]
    args = [p_all, p_all]
    for c in consts:
        in_specs.append(pl.BlockSpec(c.shape, fixed))
        args.append(c)
    if vres is not None:
        vfirst, v0, vup = vres
        in_specs += [pl.BlockSpec((tm, R), row), pl.BlockSpec(v0.shape, fixed), pl.BlockSpec(vup.shape, fixed)]
        args += [vfirst, v0, vup]
    pairs = R // LANES
    pm_spec = pl.BlockSpec((1, pairs, tm, LANES), lambda i: (i // tiles_per_seq, 0, i % tiles_per_seq, 0))
    pm_shape = jax.ShapeDtypeStruct((bsz, pairs, seq, LANES), BF16)
    nat_spec = pl.BlockSpec((tm, R), row)
    nat_shape = jax.ShapeDtypeStruct((m, R), F32)
    n_nat = 4 if emit_v else 3
    return pl.pallas_call(
        functools.partial(_prep_body, tiles_per_seq=tiles_per_seq, has_vres=vres is not None, emit_v=emit_v,
                          width=R),
        grid=(m // tm,),
        in_specs=in_specs,
        out_specs=[pm_spec] * 7 + [nat_spec] * n_nat,
        out_shape=[pm_shape] * 7 + [nat_shape] * n_nat,
        scratch_shapes=[pltpu.VMEM((1, 3 * R), F32), pltpu.VMEM((1, lora_w), F32)],
        compiler_params=_cparams(1),
        name="rwkv_prep",
    )(*args)


def _wkv_body(at_ref, rt_ref, bt_ref, kt_ref, bp_ref, kp_ref, v_ref, wc_ref, y_ref, state_ref, *, pairs, chunks):
    C = WKV_CHUNK
    hs = HEAD_SIZE

    @pl.when(pl.program_id(1) == 0)
    def _():
        state_ref[...] = jnp.zeros_like(state_ref)

    def head_masks(rows, width):
        lane = lax.broadcasted_iota(jnp.int32, (rows, width), 1)
        first = (lane % LANES) < hs
        return first, jnp.logical_not(first)

    def stack_heads(x):
        m0, m1 = head_masks(*x.shape)
        zero = jnp.zeros_like(x)
        return jnp.concatenate([jnp.where(m0, x, zero), jnp.where(m1, x, zero)], axis=0)

    row = lax.broadcasted_iota(jnp.int32, (C, LANES), 0)
    col = lax.broadcasted_iota(jnp.int32, (C, LANES), 1) % hs
    strict, incl = row > col, row >= col
    sq_r = lax.broadcasted_iota(jnp.int32, (LANES, LANES), 0)
    sq_c = lax.broadcasted_iota(jnp.int32, (LANES, LANES), 1)
    same_head = (sq_r // hs) == (sq_c // hs)
    eye = sq_r == sq_c
    levels = C.bit_length() - 1
    dot = functools.partial(jnp.dot, preferred_element_type=F32)
    P = range(pairs)

    def step(c, carry):
        sl = pl.ds(pl.multiple_of(c * C, C), C)
        at = [at_ref[0, p, sl, :] for p in P]
        rt = [rt_ref[0, p, sl, :] for p in P]
        v = [v_ref[0, p, sl, :] for p in P]
        vs = [stack_heads(v[p]) for p in P]
        lhs = [jnp.concatenate([at[p], rt[p]], axis=0) for p in P]
        rhs = [jnp.concatenate([stack_heads(bt_ref[0, p, sl, :]), stack_heads(kt_ref[0, p, sl, :])], axis=0)
               for p in P]
        abk = [_dot_nt(lhs[p], rhs[p]) for p in P]
        n = [jnp.where(strict, abk[p][:C, :LANES], 0.0) for p in P]
        a_rb = [jnp.where(incl, abk[p][C:, :LANES], 0.0) for p in P]
        a_ak = [jnp.where(strict, abk[p][:C, LANES:], 0.0) for p in P]
        a_rk = [jnp.where(incl, abk[p][C:, LANES:], 0.0) for p in P]
        av = [dot(a_ak[p].astype(BF16), vs[p]) for p in P]
        x = [jnp.concatenate([at[p].astype(F32), av[p]], axis=1) for p in P]
        for lvl in range(levels):
            last = lvl + 1 == levels
            for p in P:
                z = x[p] if last else jnp.concatenate([n[p], x[p]], axis=1)
                out = dot(n[p].astype(BF16), stack_heads(z.astype(BF16)))
                if last:
                    x[p] = x[p] + out
                else:
                    n[p] = out[:, :LANES]
                    x[p] = x[p] + out[:, LANES:]
        xb = [x[p].astype(BF16) for p in P]
        zeros_c = jnp.zeros((C, LANES), BF16)
        bx = [_dot_tn(jnp.concatenate([bp_ref[0, p, sl, :], kp_ref[0, p, sl, :]], axis=0),
                      jnp.concatenate([xb[p], jnp.concatenate([zeros_c, v[p]], axis=1)], axis=0)) for p in P]
        zeros_2c = jnp.zeros((2 * C, LANES), BF16)
        rx = [dot(jnp.concatenate([a_rb[p], a_rk[p]], axis=1).astype(BF16),
                  jnp.concatenate([stack_heads(xb[p]), jnp.concatenate([zeros_2c, vs[p]], axis=1)], axis=0))
              for p in P]
        for p in P:
            state = state_ref[p]
            wc = wc_ref[pl.ds(pl.multiple_of(c * C, C), 1), p * LANES:(p + 1) * LANES]
            trans = jnp.where(same_head, bx[p][:, :LANES], 0.0) + jnp.where(eye, wc, 0.0)
            inject = jnp.where(same_head, bx[p][:, LANES:], 0.0)
            r_eff = rt[p].astype(F32) + rx[p][:, :LANES]
            y_ref[0, p, sl, :] = _dot(r_eff, state) + rx[p][:, LANES:]
            state_ref[p] = _dot_x3(trans, state) + inject
        return carry

    lax.fori_loop(0, chunks, step, 0)


def _wkv(at, rt, bt, kt, bp, kp, v, wc, pairs=4, span=512):
    bsz, n_pairs, seq, _ = at.shape
    groups = n_pairs // pairs
    spec = pl.BlockSpec((1, pairs, span, LANES), lambda g, t: (g // groups, g % groups, t, 0))
    wc_spec = pl.BlockSpec((span, pairs * LANES), lambda g, t: ((g // groups) * (seq // span) + t, g % groups))
    return pl.pallas_call(
        functools.partial(_wkv_body, pairs=pairs, chunks=span // WKV_CHUNK),
        grid=(bsz * groups, seq // span),
        in_specs=[spec] * 7 + [wc_spec],
        out_specs=spec,
        out_shape=jax.ShapeDtypeStruct((bsz, n_pairs, seq, LANES), F32),
        scratch_shapes=[pltpu.VMEM((pairs, LANES, LANES), F32)],
        compiler_params=_cparams(2),
        name="wkv7",
    )(at, rt, bt, kt, bp, kp, v, wc)


def _post_pool_body(y_ref, bonus_ref, g_ref, u_ref, lnw_ref, lnb_ref, sel_ref, selt_ref,
                    pw_ref, ps_ref, ya_o, yb_o, ext_ref, *, tiles_per_seq, tm, group_width):
    sel, selt = sel_ref[...], selt_ref[...]
    inv_n = 1.0 / HEAD_SIZE
    y = jnp.concatenate([y_ref[0, p] for p in range(y_ref.shape[1])], axis=1)
    mean = _head_sum(y, sel, selt) * inv_n
    yc = y - mean
    var = _head_sum(yc * yc, sel, selt) * inv_n
    yn = yc * lax.rsqrt(var + GN_EPS) * lnw_ref[...] + lnb_ref[...]
    ya_o[...] = ((yn + bonus_ref[...]) * g_ref[...]).astype(ya_o.dtype)

    seq_tile = pl.program_id(0) % tiles_per_seq

    @pl.when(seq_tile == 0)
    def _():
        ext_ref[0:POOL_HALO, :] = jnp.zeros((POOL_HALO, ext_ref.shape[1]), F32)

    u = u_ref[...]
    ext_ref[POOL_HALO:POOL_HALO + tm, :] = u
    t_pos = seq_tile * tm + lax.broadcasted_iota(jnp.int32, (tm, group_width), 0)
    for gi, w in enumerate(POOL_WINDOWS):
        lanes = slice(gi * group_width, (gi + 1) * group_width)
        acc = u[:, lanes]
        for j in range(1, w):
            acc = acc + ext_ref[POOL_HALO - j:POOL_HALO - j + tm, lanes]
        count = jnp.minimum(t_pos + 1, w).astype(F32)
        mixed = acc / count - u[:, lanes]
        yg = _dot_x3(mixed, pw_ref[gi])
        yb_o[:, lanes] = (yg * ps_ref[:, lanes]).astype(yb_o.dtype)
    ext_ref[0:POOL_HALO, :] = u[tm - POOL_HALO:, :]


def _post_pool(y, bonus, g, p_all, pool_blk, consts, seq, tm=256):
    m, R = bonus.shape
    lnw, lnb, sel, selt, pw, ps = consts
    tiles_per_seq = seq // tm
    row = lambda i: (i, 0)
    fixed2 = lambda i: (0, 0)
    in_specs = [pl.BlockSpec((1, y.shape[1], tm, LANES), lambda i: (i // tiles_per_seq, 0, i % tiles_per_seq, 0)),
                pl.BlockSpec((tm, R), row), pl.BlockSpec((tm, R), row),
                pl.BlockSpec((tm, R), lambda i: (i, pool_blk))]
    in_specs += [pl.BlockSpec(c.shape, fixed2) for c in (lnw, lnb, sel, selt)]
    in_specs += [pl.BlockSpec(pw.shape, lambda i: (0, 0, 0)), pl.BlockSpec(ps.shape, fixed2)]
    out = jax.ShapeDtypeStruct((m, R), BF16)
    return pl.pallas_call(
        functools.partial(_post_pool_body, tiles_per_seq=tiles_per_seq, tm=tm, group_width=R // len(POOL_WINDOWS)),
        grid=(m // tm,),
        in_specs=in_specs,
        out_specs=[pl.BlockSpec((tm, R), row)] * 2,
        out_shape=[out, out],
        scratch_shapes=[pltpu.VMEM((POOL_HALO + tm, R), F32)],
        compiler_params=_cparams(1),
        name="post_pool",
    )(y, bonus, g, p_all, lnw, lnb, sel, selt, pw, ps)


def _merge_body(ya_ref, yb_ref, pa_ref, pb_ref, ga_ref, gb_ref, o_ref, pab_ref, pbb_ref):
    _cast_weight_tiles([pa_ref, pb_ref], [pab_ref, pbb_ref])
    ma = jnp.dot(ya_ref[...], pab_ref[...], preferred_element_type=F32)
    mb = jnp.dot(yb_ref[...], pbb_ref[...], preferred_element_type=F32)
    o_ref[...] = (_sigmoid(ga_ref[...]) * ma + _sigmoid(gb_ref[...]) * mb).astype(o_ref.dtype)


def _merge(ya, yb, pa, pb, p_all, ga_col, gb_col, tm=512, tn=512):
    m, R = ya.shape
    d = pa.shape[1]
    ga_blk, gb_blk = ga_col // tn, gb_col // tn
    return pl.pallas_call(
        _merge_body,
        grid=(d // tn, m // tm),
        in_specs=[pl.BlockSpec((tm, R), lambda j, i: (i, 0)),
                  pl.BlockSpec((tm, R), lambda j, i: (i, 0)),
                  pl.BlockSpec((R, tn), lambda j, i: (0, j)),
                  pl.BlockSpec((R, tn), lambda j, i: (0, j)),
                  pl.BlockSpec((tm, tn), lambda j, i: (i, ga_blk + j)),
                  pl.BlockSpec((tm, tn), lambda j, i: (i, gb_blk + j))],
        out_specs=pl.BlockSpec((tm, tn), lambda j, i: (i, j)),
        out_shape=jax.ShapeDtypeStruct((m, d), BF16),
        scratch_shapes=[pltpu.VMEM((R, tn), BF16)] * 2,
        compiler_params=_cparams(2),
        name="merge",
    )(ya, yb, pa, pb, p_all, p_all)


def _router_body(x_ref, g_ref, wr_ref, h_o, idx_o, gate_o):
    x = x_ref[...]
    inv = lax.rsqrt(jnp.mean(x * x, axis=-1, keepdims=True) + NORM_EPS)
    h = x * inv * g_ref[...]
    h_o[...] = h.reshape(h_o.shape)
    logits = _dot_x3(h, wr_ref[...])
    lane = lax.broadcasted_iota(jnp.int32, logits.shape, 1)
    neg_inf = jnp.float32(-jnp.inf)
    l1 = jnp.where(lane < N_EXPERTS, logits, neg_inf)
    m1 = jnp.max(l1, axis=-1, keepdims=True)
    i1 = jnp.min(jnp.where(l1 == m1, lane, LANES), axis=-1, keepdims=True)
    l2 = jnp.where(lane == i1, neg_inf, l1)
    m2 = jnp.max(l2, axis=-1, keepdims=True)
    i2 = jnp.min(jnp.where(l2 == m2, lane, LANES), axis=-1, keepdims=True)
    e = jnp.exp(m2 - m1)
    g1 = 1.0 / (1.0 + e)
    idx_o[...] = jnp.where(lane == 0, i1, jnp.where(lane == 1, i2, 0))
    gate_o[...] = jnp.where(lane == 0, g1, jnp.where(lane == 1, e * g1, 0.0))


def _router(x, g, wr_pad, tm=256):
    m, d = x.shape
    row = lambda i: (i, 0)
    return pl.pallas_call(
        _router_body,
        grid=(m // tm,),
        in_specs=[pl.BlockSpec((tm, d), row), pl.BlockSpec((1, d), lambda i: (0, 0)),
                  pl.BlockSpec(wr_pad.shape, lambda i: (0, 0))],
        out_specs=[pl.BlockSpec((tm, d // LANES, LANES), lambda i: (i, 0, 0)),
                   pl.BlockSpec((tm, LANES), row), pl.BlockSpec((tm, LANES), row)],
        out_shape=[jax.ShapeDtypeStruct((m, d // LANES, LANES), F32), jax.ShapeDtypeStruct((m, LANES), jnp.int32),
                   jax.ShapeDtypeStruct((m, LANES), F32)],
        compiler_params=_cparams(1),
        name="router",
    )(x, g.reshape(1, d), wr_pad)


def _gather_body(tok_ref, used_ref, h_hbm, o_ref, buf_ref, sems, *, tg, n_steps):
    i = pl.program_id(0)

    def row_copy(step, r):
        slot = step % 2
        return pltpu.make_async_copy(h_hbm.at[pl.ds(tok_ref[step * tg + r], 1)], buf_ref.at[slot, pl.ds(r, 1)],
                                     sems.at[slot])

    def fetch(step):
        @pl.when(jnp.logical_and(step < n_steps, step * tg < used_ref[0]))
        def _():
            def start(q, c):
                for u in range(DMA_UNROLL):
                    row_copy(step, q * DMA_UNROLL + u).start()
                return c
            lax.fori_loop(0, tg // DMA_UNROLL, start, 0)

    @pl.when(i == 0)
    def _():
        fetch(i)

    fetch(i + 1)

    @pl.when(i * tg < used_ref[0])
    def _():
        pltpu.make_async_copy(h_hbm.at[pl.ds(0, tg)], buf_ref.at[i % 2], sems.at[i % 2]).wait()
        o_ref[...] = buf_ref[i % 2].reshape(o_ref.shape).astype(o_ref.dtype)

    @pl.when(i * tg >= used_ref[0])
    def _():
        o_ref[...] = jnp.zeros_like(o_ref)


def _gather_rows(tok, used_rows, h3, tg=256):
    p = tok.shape[0]
    _, nc, lanes = h3.shape
    n_steps = p // tg
    return pl.pallas_call(
        functools.partial(_gather_body, tg=tg, n_steps=n_steps),
        grid_spec=pltpu.PrefetchScalarGridSpec(
            num_scalar_prefetch=2,
            grid=(n_steps,),
            in_specs=[pl.BlockSpec(memory_space=pl.ANY)],
            out_specs=pl.BlockSpec((tg, nc * lanes), lambda i, tok, used: (i, 0)),
            scratch_shapes=[pltpu.VMEM((2, tg, nc, lanes), F32), pltpu.SemaphoreType.DMA((2,))]),
        out_shape=jax.ShapeDtypeStruct((p, nc * lanes), BF16),
        compiler_params=_cparams(1),
        name="moe_gather",
    )(tok, used_rows, h3)


DMA_UNROLL = 8
MOE_ROWS = 2048
MOE_SUB = 256
MOE_FF_TILE = 256
PLAN_FIELDS = 7


def _experts_body(plan_ref, x_ref, wg_ref, wu_ref, wd_ref, o_ref, *, n_items):
    w = pl.program_id(0)
    j = pl.program_id(1)
    field = lambda f: plan_ref[f * n_items + w]
    lo, hi, zero, live = field(3), field(4), field(5), field(6)
    n_sub = MOE_ROWS // MOE_SUB

    @pl.when(jnp.logical_and(zero == 1, j == 0))
    def _():
        o_ref[...] = jnp.zeros_like(o_ref)

    @pl.when(live == 1)
    def _():
        wg = wg_ref[...].astype(BF16)
        wu = wu_ref[...].astype(BF16)
        wd = wd_ref[...].astype(BF16)

        def rows(sl):
            x = x_ref[sl, :]
            g = jnp.dot(x, wg, preferred_element_type=F32)
            u = jnp.dot(x, wu, preferred_element_type=F32)
            h = (g * _sigmoid(g) * u).astype(BF16)
            o_ref[sl, :] += jnp.dot(h, wd, preferred_element_type=F32)

        whole = jnp.logical_and(lo == 0, hi == n_sub)

        @pl.when(whole)
        def _():
            rows(slice(None))

        for s in range(n_sub):
            @pl.when(jnp.logical_and(jnp.logical_not(whole), jnp.logical_and(lo <= s, s < hi)))
            def _():
                rows(slice(s * MOE_SUB, (s + 1) * MOE_SUB))


def _experts(plan, xs, w_gate, w_up, w_down):
    p, d = xs.shape
    f = w_gate.shape[2]
    tn = MOE_FF_TILE
    nj = f // tn
    n_items = plan.shape[0] // PLAN_FIELDS
    fld = lambda plan, f_idx, w: plan[f_idx * n_items + w]
    col = lambda plan, w, j: jnp.where(fld(plan, 6, w) == 1, j, nj - 1)
    return pl.pallas_call(
        functools.partial(_experts_body, n_items=n_items),
        grid_spec=pltpu.PrefetchScalarGridSpec(
            num_scalar_prefetch=1,
            grid=(n_items, nj),
            in_specs=[pl.BlockSpec((MOE_ROWS, d), lambda w, j, plan: (fld(plan, 1, w), 0),
                                   pipeline_mode=pl.Buffered(1)),
                      pl.BlockSpec((None, d, tn), lambda w, j, plan: (fld(plan, 2, w), 0, col(plan, w, j))),
                      pl.BlockSpec((None, d, tn), lambda w, j, plan: (fld(plan, 2, w), 0, col(plan, w, j))),
                      pl.BlockSpec((None, tn, d), lambda w, j, plan: (fld(plan, 2, w), col(plan, w, j), 0))],
            out_specs=pl.BlockSpec((MOE_ROWS, d), lambda w, j, plan: (fld(plan, 0, w), 0),
                                   pipeline_mode=pl.Buffered(1))),
        out_shape=jax.ShapeDtypeStruct((p, d), F32),
        compiler_params=_cparams(2),
        name="moe_experts",
    )(plan, xs, w_gate, w_up, w_down)


def _combine_body(d0_ref, d1_ref, x_ref, gate_ref, g_ref, y_hbm, o_ref, buf0, buf1, sem, *, tg, apply_norm):
    base = pl.program_id(0) * tg

    def row_copy(i, d_ref, buf):
        return pltpu.make_async_copy(y_hbm.at[pl.ds(d_ref[base + i], 1), :], buf.at[pl.ds(i, 1), :], sem)

    def start(q, c):
        for u in range(DMA_UNROLL):
            row_copy(q * DMA_UNROLL + u, d0_ref, buf0).start()
            row_copy(q * DMA_UNROLL + u, d1_ref, buf1).start()
        return c

    lax.fori_loop(0, tg // DMA_UNROLL, start, 0)
    for buf in (buf0, buf1):
        pltpu.make_async_copy(y_hbm.at[pl.ds(0, tg), :], buf, sem).wait()
    gate = gate_ref[...]
    x = x_ref[...] + gate[:, 0:1] * buf0[...] + gate[:, 1:2] * buf1[...]
    if apply_norm:
        x = x * lax.rsqrt(jnp.mean(x * x, axis=-1, keepdims=True) + NORM_EPS) * g_ref[...]
    o_ref[...] = x


def _combine(d0, d1, x, gate, g, ybuf, apply_norm, tg=256):
    m, d = x.shape
    row = lambda i, d0, d1: (i, 0)
    return pl.pallas_call(
        functools.partial(_combine_body, tg=tg, apply_norm=apply_norm),
        grid_spec=pltpu.PrefetchScalarGridSpec(
            num_scalar_prefetch=2,
            grid=(m // tg,),
            in_specs=[pl.BlockSpec((tg, d), row), pl.BlockSpec((tg, LANES), row),
                      pl.BlockSpec((1, d), lambda i, d0, d1: (0, 0)),
                      pl.BlockSpec(memory_space=pl.ANY)],
            out_specs=pl.BlockSpec((tg, d), row),
            scratch_shapes=[pltpu.VMEM((tg, d), F32), pltpu.VMEM((tg, d), F32), pltpu.SemaphoreType.DMA(())]),
        out_shape=jax.ShapeDtypeStruct((m, d), F32),
        compiler_params=_cparams(1),
        name="moe_combine",
    )(d0, d1, x, gate, g.reshape(1, d), ybuf)


def _moe_plan(top_e):
    i32 = jnp.int32
    n_tok = top_e.shape[0]
    n_slot = n_tok * TOP_K
    flat_e = top_e.reshape(n_slot)
    onehot = (flat_e[:, None] == jnp.arange(N_EXPERTS, dtype=i32)[None, :]).astype(i32)
    csum = jnp.cumsum(onehot, axis=0)
    rank = jnp.sum((csum - onehot) * onehot, axis=1)
    counts = csum[-1]
    padded = (counts + MOE_SUB - 1) // MOE_SUB * MOE_SUB
    pend = jnp.cumsum(padded)
    pstart = pend - padded
    dest = pstart[flat_e] + rank
    n_blocks = -(-(n_slot + N_EXPERTS * MOE_SUB) // MOE_ROWS)
    tok = jnp.zeros((n_blocks * MOE_ROWS,), i32).at[dest].set(jnp.arange(n_slot, dtype=i32) // TOP_K)
    used_rows = pend[-1]
    used_blocks = (used_rows + MOE_ROWS - 1) // MOE_ROWS

    n_items = n_blocks + N_EXPERTS
    blk_lo = (jnp.arange(n_blocks, dtype=i32) * MOE_ROWS)[:, None]
    lo_row = jnp.maximum(pstart[None, :], blk_lo)
    hi_row = jnp.minimum(pend[None, :], blk_lo + MOE_ROWS)
    valid = (hi_row > lo_row).reshape(-1)
    n_live = jnp.sum(valid.astype(i32))
    pos = jnp.where(valid, jnp.cumsum(valid.astype(i32)) - 1, n_items)
    compact = lambda v: jnp.zeros((n_items,), i32).at[pos].set(v.reshape(-1).astype(i32), mode="drop")
    blk_c = compact(jnp.broadcast_to(jnp.arange(n_blocks, dtype=i32)[:, None], lo_row.shape))
    exp_c = compact(jnp.broadcast_to(jnp.arange(N_EXPERTS, dtype=i32)[None, :], lo_row.shape))
    lo_c = compact((lo_row - blk_lo) // MOE_SUB)
    hi_c = compact((hi_row - blk_lo) // MOE_SUB)
    first_c = compact(lo_row == blk_lo)

    w = jnp.arange(n_items, dtype=i32)
    live = w < n_live
    tail_blk = used_blocks + (w - n_live)
    last = n_live - 1
    out_blk = jnp.where(live, blk_c, jnp.minimum(tail_blk, n_blocks - 1))
    x_blk = jnp.where(live, blk_c, blk_c[last])
    expert = jnp.where(live, exp_c, exp_c[last])
    zero = jnp.where(live, first_c, (tail_blk < n_blocks).astype(i32))
    plan = jnp.concatenate([out_blk, x_blk, expert, lo_c, hi_c, zero, live.astype(i32)]).astype(i32)
    dest = dest.reshape(n_tok, TOP_K).astype(i32)
    return tok, used_rows.astype(i32).reshape(1), plan, dest[:, 0], dest[:, 1]


def kernel(x, norm_mix, w_in, shift_mu, decay_w0, decay_up, iclr_a0, iclr_up, outgate_up, k_k, k_a, r_k, lnx_w, lnx_b, vres_down, vres_mu, vres_up, vres_v0, pool_w, pool_scale, proj_a, proj_b, w_o, norm_ffn, ffn_gate, ffn_up, ffn_down, router, moe_gate, moe_up, moe_down, norm_final):
    bsz, seq, d = x.shape
    depth = w_in.shape[0]
    R = decay_w0.shape[1]
    dl, il, gl = decay_up.shape[1], iclr_up.shape[1], outgate_up.shape[1]
    vl = vres_up.shape[1]
    assert (dl, il, vl, gl) == (96, 96, 64, 256) and 3 * R + dl + il + gl == shift_mu.shape[1]
    m = bsz * seq
    x = x.reshape(m, d)
    row = lambda t: t.reshape(1, -1)

    head_of_lane = jnp.arange(R, dtype=jnp.int32) // HEAD_SIZE
    sel = (head_of_lane[:, None] == jnp.arange(LANES, dtype=jnp.int32)[None, :]).astype(BF16)
    sel_t = sel.T
    t_idx = jnp.arange(PREP_ROWS, dtype=jnp.int32)
    same_chunk = (t_idx[:, None] // WKV_CHUNK) == (t_idx[None, :] // WKV_CHUNK)
    ones_bd = same_chunk.astype(BF16)
    tri_bd = (same_chunk & (t_idx[None, :] <= t_idx[:, None])).astype(BF16)

    v_first = None
    for l in range(depth):
        w = w_in[l]
        c0 = 3 * R
        c1, c2, c3 = c0 + dl, c0 + dl + il, c0 + dl + il + gl
        vd_w = vres_down[l - 1] if l > 0 else jnp.zeros((d, vl), F32)
        vd_mu = vres_mu[l - 1] if l > 0 else jnp.zeros((vl,), F32)
        w_all = jnp.concatenate([w[:, :c0], w[:, c3:], w[:, c0:c2], vd_w, w[:, c2:c3]], axis=1)
        mu = shift_mu[l]
        mu_rkv = row(mu[:c0])
        mu_lora = row(jnp.concatenate([mu[c0:c2], vd_mu, mu[c2:c3]]))
        pool_col = c0
        ga_col = c0 + R
        gb_col = ga_col + d
        zeros = lambda n: jnp.zeros((n, R), F32)
        dup = jnp.concatenate([decay_up[l], zeros(128 - dl)], axis=0)
        iup = jnp.concatenate([zeros(dl), iclr_up[l], zeros(256 - dl - il)], axis=0)
        gup = outgate_up[l].astype(BF16)

        h = _rmsnorm(x, norm_mix[l], BF16)
        p_all = _matmul(h, [w_all], tm=1024, tn=512, out_dtype=F32)

        consts = [mu_rkv, mu_lora, row(decay_w0[l]), row(iclr_a0[l]), row(k_k[l]), row(k_a[l]), row(r_k[l]),
                  dup, iup, gup, sel, sel_t, tri_bd, ones_bd]
        vres = None
        if l > 0:
            vup = jnp.concatenate([zeros(128 - vl), vres_up[l - 1]], axis=0).astype(BF16)
            vres = (v_first, row(vres_v0[l - 1]), vup)
        emit_v = l == 0 and depth > 1
        outs = _rwkv_prep(p_all, bsz, seq, R, consts, vres, emit_v)
        wc, bonus, g = outs[7:10]
        if emit_v:
            v_first = outs[10]
        y = _wkv(*outs[:7], wc)

        ya, yb = _post_pool(y, bonus, g, p_all, pool_col // R,
                            [row(lnx_w[l]), row(lnx_b[l]), sel, sel_t, pool_w[l], row(pool_scale[l])], seq)
        merged = _merge(ya, yb, proj_a[l], proj_b[l], p_all, ga_col, gb_col)
        x = _matmul(merged, [w_o[l]], tm=1024, tn=512, out_dtype=F32, residual=x)

        i = l // 2
        if l % 2 == 0:
            h2 = _rmsnorm(x, norm_ffn[l], BF16)
            hidden = _matmul(h2, [ffn_gate[i], ffn_up[i]], tm=1024, tn=512, out_dtype=BF16)
            x = _matmul(hidden, [ffn_down[i]], tm=512, tn=512, out_dtype=F32, residual=x)
            if l == depth - 1:
                x = _rmsnorm(x, norm_final, F32)
        else:
            wr_pad = jnp.concatenate([router[i], jnp.zeros((d, LANES - N_EXPERTS), F32)], axis=1)
            h3, idx, gate = _router(x, norm_ffn[l], wr_pad)
            tok, used_rows, plan, d0, d1 = _moe_plan(idx[:, :TOP_K])
            xs = _gather_rows(tok, used_rows, h3)
            ybuf = _experts(plan, xs, moe_gate[i], moe_up[i], moe_down[i])
            x = _combine(d0, d1, x, gate, norm_final, ybuf, apply_norm=l == depth - 1)
    return x.reshape(bsz, seq, d)
```

```python
import functools

import jax
import jax.numpy as jnp
from jax import lax
from jax.experimental import pallas as pl
from jax.experimental.pallas import tpu as pltpu

F32 = jnp.float32
BF16 = jnp.bfloat16

HEAD_SIZE = 64
POOL_WINDOWS = (2, 4, 8, 16)
N_EXPERTS = 8
TOP_K = 2
NORM_EPS = 1e-6
GN_EPS = 64e-5
LANES = 128
VMEM_LIMIT_BYTES = 56 * 1024 * 1024
WKV_CHUNK = 64
POOL_HALO = 16
PREP_ROWS = 256


def _cparams(n_axes):
    return pltpu.CompilerParams(dimension_semantics=("arbitrary",) * n_axes,
                                vmem_limit_bytes=VMEM_LIMIT_BYTES)


def _dot(a, b):
    return jnp.dot(a.astype(BF16), b.astype(BF16), preferred_element_type=F32)


def _split(x):
    hi = x.astype(BF16)
    lo = (x - hi.astype(F32)).astype(BF16)
    return hi, lo


def _dot_x3(a, b):
    a_hi, a_lo = _split(a)
    b_hi, b_lo = _split(b)
    d = functools.partial(jnp.dot, preferred_element_type=F32)
    return d(a_hi, b_hi) + d(a_hi, b_lo) + d(a_lo, b_hi)


def _dot_lhs_split(a, b_bf16):
    a_hi, a_lo = _split(a)
    d = functools.partial(jnp.dot, preferred_element_type=F32)
    return d(a_hi, b_bf16) + d(a_lo, b_bf16)


def _dot_tn(a, b):
    return lax.dot_general(a.astype(BF16), b.astype(BF16), (((0,), (0,)), ((), ())),
                           preferred_element_type=F32)


def _dot_nt(a, b):
    return lax.dot_general(a.astype(BF16), b.astype(BF16), (((1,), (1,)), ((), ())),
                           preferred_element_type=F32)


def _sigmoid(x):
    return 1.0 / (1.0 + jnp.exp(-x))


def _rmsnorm_body(x_ref, g_ref, o_ref):
    x = x_ref[...]
    inv = lax.rsqrt(jnp.mean(x * x, axis=-1, keepdims=True) + NORM_EPS)
    o_ref[...] = (x * inv * g_ref[...]).astype(o_ref.dtype)


def _rmsnorm(x, g, out_dtype, tm=512):
    m, d = x.shape
    return pl.pallas_call(
        _rmsnorm_body,
        grid=(m // tm,),
        in_specs=[pl.BlockSpec((tm, d), lambda i: (i, 0)),
                  pl.BlockSpec((1, d), lambda i: (0, 0))],
        out_specs=pl.BlockSpec((tm, d), lambda i: (i, 0)),
        out_shape=jax.ShapeDtypeStruct((m, d), out_dtype),
        compiler_params=_cparams(1),
        name="rmsnorm",
    )(x, g.reshape(1, d))


def _cast_weight_tiles(w_refs, wb_refs):
    @pl.when(pl.program_id(1) == 0)
    def _():
        for w_ref, wb_ref in zip(w_refs, wb_refs):
            wb_ref[...] = w_ref[...].astype(BF16)


def _mm_plain_body(a_ref, w_ref, o_ref, wb_ref):
    _cast_weight_tiles([w_ref], [wb_ref])
    o_ref[...] = jnp.dot(a_ref[...], wb_ref[...], preferred_element_type=F32).astype(o_ref.dtype)


def _mm_swiglu_body(a_ref, wg_ref, wu_ref, o_ref, wgb_ref, wub_ref):
    _cast_weight_tiles([wg_ref, wu_ref], [wgb_ref, wub_ref])
    a = a_ref[...]
    g = jnp.dot(a, wgb_ref[...], preferred_element_type=F32)
    u = jnp.dot(a, wub_ref[...], preferred_element_type=F32)
    o_ref[...] = (g * _sigmoid(g) * u).astype(o_ref.dtype)


def _mm_residual_body(a_ref, w_ref, res_ref, o_ref, wb_ref):
    _cast_weight_tiles([w_ref], [wb_ref])
    o_ref[...] = res_ref[...] + jnp.dot(a_ref[...], wb_ref[...], preferred_element_type=F32)


def _layer_weight_spec(w, layer, k, tn):
    if w.ndim == 2:
        return pl.BlockSpec((k, tn), lambda j, i: (0, j))
    return pl.BlockSpec((None, k, tn), lambda j, i: (layer, 0, j))


def _matmul(a, ws, *, tm, tn, out_dtype, residual=None, layer=0):
    m, k = a.shape
    n = ws[0].shape[-1]
    a_spec = pl.BlockSpec((tm, k), lambda j, i: (i, 0))
    w_spec = _layer_weight_spec(ws[0], layer, k, tn)
    o_spec = pl.BlockSpec((tm, tn), lambda j, i: (i, j))
    if len(ws) == 2:
        body, in_specs, args = _mm_swiglu_body, [a_spec, w_spec, w_spec], (a, ws[0], ws[1])
    elif residual is not None:
        body, in_specs, args = _mm_residual_body, [a_spec, w_spec, o_spec], (a, ws[0], residual)
    else:
        body, in_specs, args = _mm_plain_body, [a_spec, w_spec], (a, ws[0])
    return pl.pallas_call(
        body,
        grid=(n // tn, m // tm),
        in_specs=in_specs,
        out_specs=o_spec,
        out_shape=jax.ShapeDtypeStruct((m, n), out_dtype),
        scratch_shapes=[pltpu.VMEM((k, tn), BF16)] * len(ws),
        compiler_params=_cparams(2),
        name="matmul",
    )(*args)


def _head_sum(x, sel, sel_t):
    s = _dot_lhs_split(x, sel)
    return _dot_lhs_split(s, sel_t)


def _token_shift(p, carry_ref, mu):
    rows = p.shape[0]
    row = lax.broadcasted_iota(jnp.int32, p.shape, 0)
    prev = jnp.where(row == 0, carry_ref[...], pltpu.roll(p, 1, axis=0))
    carry_ref[...] = p[rows - 1:rows, :]
    return p + (prev - p) * mu


def _prep_body(*refs, tiles_per_seq, has_vres, emit_v, width):
    (rkv_ref, lora_ref, mu_rkv_ref, mu_lora_ref, w0_ref, a0_ref, kk_ref, ka_ref, rk_ref,
     dup_ref, iup_ref, gup_ref, sel_ref, selt_ref, tri_ref, ones_ref) = refs[:16]
    pos = 16
    if has_vres:
        vfirst_ref, v0_ref, vup_ref = refs[pos:pos + 3]
        pos += 3
    at_o, rt_o, bt_o, kt_o, bp_o, kp_o, vb_o, wc_o, bonus_o, g_o = refs[pos:pos + 10]
    pos += 10
    if emit_v:
        v_o = refs[pos]
        pos += 1
    carry_rkv, carry_lora = refs[pos:]
    R = width

    @pl.when(pl.program_id(0) % tiles_per_seq == 0)
    def _():
        carry_rkv[...] = jnp.zeros_like(carry_rkv)
        carry_lora[...] = jnp.zeros_like(carry_lora)

    z = _token_shift(rkv_ref[...], carry_rkv, mu_rkv_ref[...])
    zl = _token_shift(lora_ref[...], carry_lora, mu_lora_ref[...])
    r, k, v = z[:, :R], z[:, R:2 * R], z[:, 2 * R:]

    decay_logit = w0_ref[...] + _dot_x3(jnp.tanh(zl[:, :128]), dup_ref[...])
    neg = -decay_logit
    softplus = jnp.maximum(neg, 0.0) + jnp.log(1.0 + jnp.exp(-jnp.abs(neg)))
    lw = -jnp.exp(-softplus - 0.5)
    a = _sigmoid(a0_ref[...] + _dot_x3(zl[:, :256], iup_ref[...]))
    g_o[...] = _dot(_sigmoid(zl[:, 256:]), gup_ref[...])
    if has_vres:
        mix = _sigmoid(v0_ref[...] + _dot(zl[:, 128:256], vup_ref[...]))
        v = v + (vfirst_ref[...] - v) * mix
    if emit_v:
        v_o[...] = v

    sel, selt = sel_ref[...], selt_ref[...]
    kk = k * kk_ref[...]
    norm = jnp.sqrt(_head_sum(kk * kk, sel, selt))
    kk = kk / jnp.maximum(norm, 1e-12)
    k = k * (1.0 + (a - 1.0) * ka_ref[...])
    bonus_o[...] = _head_sum(r * k * rk_ref[...], sel, selt) * v

    d = functools.partial(jnp.dot, preferred_element_type=F32)
    lw_hi = lw.astype(BF16)
    rest = lw - lw_hi.astype(F32)
    lw_mid = rest.astype(BF16)
    lw_lo = (rest - lw_mid.astype(F32)).astype(BF16)
    tri, ones = tri_ref[...], ones_ref[...]
    cum = d(tri, lw_hi) + d(tri, lw_mid) + d(tri, lw_lo)
    total = d(ones, lw_hi) + d(ones, lw_mid) + d(ones, lw_lo)
    w_inv = jnp.exp(-cum)
    w_rem = jnp.exp(total - cum)
    b = kk * a
    wc_o[...] = jnp.exp(total)
    pairs = R // LANES
    for o_ref, val in ((at_o, -kk * jnp.exp(cum - lw)), (rt_o, r * jnp.exp(cum)), (bt_o, b * w_inv),
                       (kt_o, k * w_inv), (bp_o, b * w_rem), (kp_o, k * w_rem), (vb_o, v)):
        val = val.astype(BF16)
        for p in range(pairs):
            o_ref[0, p] = val[:, p * LANES:(p + 1) * LANES]


def _rwkv_prep(p_all, bsz, seq, width, consts, vres, emit_v):
    tm = PREP_ROWS
    m = p_all.shape[0]
    R = width
    lora_w = 512
    lora_blk = (p_all.shape[1] - lora_w) // lora_w
    tiles_per_seq = seq // tm
    row = lambda i: (i, 0)
    fixed = lambda i: (0, 0)
    in_specs = [pl.BlockSpec((tm, 3 * R), row),
                pl.BlockSpec((tm, lora_w), lambda i: (i, lora_blk))]
    args = [p_all, p_all]
    for c in consts:
        in_specs.append(pl.BlockSpec(c.shape, fixed))
        args.append(c)
    if vres is not None:
        vfirst, v0, vup = vres
        in_specs += [pl.BlockSpec((tm, R), row), pl.BlockSpec(v0.shape, fixed), pl.BlockSpec(vup.shape, fixed)]
        args += [vfirst, v0, vup]
    pairs = R // LANES
    pm_spec = pl.BlockSpec((1, pairs, tm, LANES), lambda i: (i // tiles_per_seq, 0, i % tiles_per_seq, 0))
    pm_shape = jax.ShapeDtypeStruct((bsz, pairs, seq, LANES), BF16)
    nat_spec = pl.BlockSpec((tm, R), row)
    nat_shape = jax.ShapeDtypeStruct((m, R), F32)
    n_nat = 4 if emit_v else 3
    return pl.pallas_call(
        functools.partial(_prep_body, tiles_per_seq=tiles_per_seq, has_vres=vres is not None, emit_v=emit_v,
                          width=R),
        grid=(m // tm,),
        in_specs=in_specs,
        out_specs=[pm_spec] * 7 + [nat_spec] * n_nat,
        out_shape=[pm_shape] * 7 + [nat_shape] * n_nat,
        scratch_shapes=[pltpu.VMEM((1, 3 * R), F32), pltpu.VMEM((1, lora_w), F32)],
        compiler_params=_cparams(1),
        name="rwkv_prep",
    )(*args)


def _wkv_body(at_ref, rt_ref, bt_ref, kt_ref, bp_ref, kp_ref, v_ref, wc_ref, y_ref, state_ref, *, pairs, chunks):
    C = WKV_CHUNK
    hs = HEAD_SIZE

    @pl.when(pl.program_id(1) == 0)
    def _():
        state_ref[...] = jnp.zeros_like(state_ref)

    def head_masks(rows, width):
        lane = lax.broadcasted_iota(jnp.int32, (rows, width), 1)
        first = (lane % LANES) < hs
        return first, jnp.logical_not(first)

    def stack_heads(x):
        m0, m1 = head_masks(*x.shape)
        zero = jnp.zeros_like(x)
        return jnp.concatenate([jnp.where(m0, x, zero), jnp.where(m1, x, zero)], axis=0)

    row = lax.broadcasted_iota(jnp.int32, (C, LANES), 0)
    col = lax.broadcasted_iota(jnp.int32, (C, LANES), 1) % hs
    strict, incl = row > col, row >= col
    sq_r = lax.broadcasted_iota(jnp.int32, (LANES, LANES), 0)
    sq_c = lax.broadcasted_iota(jnp.int32, (LANES, LANES), 1)
    same_head = (sq_r // hs) == (sq_c // hs)
    eye = sq_r == sq_c
    levels = C.bit_length() - 1
    dot = functools.partial(jnp.dot, preferred_element_type=F32)
    P = range(pairs)

    def step(c, carry):
        sl = pl.ds(pl.multiple_of(c * C, C), C)
        at = [at_ref[0, p, sl, :] for p in P]
        rt = [rt_ref[0, p, sl, :] for p in P]
        v = [v_ref[0, p, sl, :] for p in P]
        vs = [stack_heads(v[p]) for p in P]
        lhs = [jnp.concatenate([at[p], rt[p]], axis=0) for p in P]
        rhs = [jnp.concatenate([stack_heads(bt_ref[0, p, sl, :]), stack_heads(kt_ref[0, p, sl, :])], axis=0)
               for p in P]
        abk = [_dot_nt(lhs[p], rhs[p]) for p in P]
        n = [jnp.where(strict, abk[p][:C, :LANES], 0.0) for p in P]
        a_rb = [jnp.where(incl, abk[p][C:, :LANES], 0.0) for p in P]
        a_ak = [jnp.where(strict, abk[p][:C, LANES:], 0.0) for p in P]
        a_rk = [jnp.where(incl, abk[p][C:, LANES:], 0.0) for p in P]
        av = [dot(a_ak[p].astype(BF16), vs[p]) for p in P]
        x = [jnp.concatenate([at[p].astype(F32), av[p]], axis=1) for p in P]
        for lvl in range(levels):
            last = lvl + 1 == levels
            for p in P:
                z = x[p] if last else jnp.concatenate([n[p], x[p]], axis=1)
                out = dot(n[p].astype(BF16), stack_heads(z.astype(BF16)))
                if last:
                    x[p] = x[p] + out
                else:
                    n[p] = out[:, :LANES]
                    x[p] = x[p] + out[:, LANES:]
        xb = [x[p].astype(BF16) for p in P]
        zeros_c = jnp.zeros((C, LANES), BF16)
        bx = [_dot_tn(jnp.concatenate([bp_ref[0, p, sl, :], kp_ref[0, p, sl, :]], axis=0),
                      jnp.concatenate([xb[p], jnp.concatenate([zeros_c, v[p]], axis=1)], axis=0)) for p in P]
        zeros_2c = jnp.zeros((2 * C, LANES), BF16)
        rx = [dot(jnp.concatenate([a_rb[p], a_rk[p]], axis=1).astype(BF16),
                  jnp.concatenate([stack_heads(xb[p]), jnp.concatenate([zeros_2c, vs[p]], axis=1)], axis=0))
              for p in P]
        for p in P:
            state = state_ref[p]
            wc = wc_ref[pl.ds(pl.multiple_of(c * C, C), 1), p * LANES:(p + 1) * LANES]
            trans = jnp.where(same_head, bx[p][:, :LANES], 0.0) + jnp.where(eye, wc, 0.0)
            inject = jnp.where(same_head, bx[p][:, LANES:], 0.0)
            r_eff = rt[p].astype(F32) + rx[p][:, :LANES]
            y_ref[0, p, sl, :] = _dot(r_eff, state) + rx[p][:, LANES:]
            state_ref[p] = _dot_x3(trans, state) + inject
        return carry

    lax.fori_loop(0, chunks, step, 0)


def _wkv(at, rt, bt, kt, bp, kp, v, wc, pairs=8, span=512):
    bsz, n_pairs, seq, _ = at.shape
    groups = n_pairs // pairs
    spec = pl.BlockSpec((1, pairs, span, LANES), lambda g, t: (g // groups, g % groups, t, 0))
    wc_spec = pl.BlockSpec((span, pairs * LANES), lambda g, t: ((g // groups) * (seq // span) + t, g % groups))
    return pl.pallas_call(
        functools.partial(_wkv_body, pairs=pairs, chunks=span // WKV_CHUNK),
        grid=(bsz * groups, seq // span),
        in_specs=[spec] * 7 + [wc_spec],
        out_specs=spec,
        out_shape=jax.ShapeDtypeStruct((bsz, n_pairs, seq, LANES), F32),
        scratch_shapes=[pltpu.VMEM((pairs, LANES, LANES), F32)],
        compiler_params=_cparams(2),
        name="wkv7",
    )(at, rt, bt, kt, bp, kp, v, wc)


def _post_pool_body(y_ref, bonus_ref, g_ref, u_ref, lnw_ref, lnb_ref, sel_ref, selt_ref,
                    pw_ref, ps_ref, ya_o, yb_o, ext_ref, *, tiles_per_seq, tm, group_width):
    sel, selt = sel_ref[...], selt_ref[...]
    inv_n = 1.0 / HEAD_SIZE
    y = jnp.concatenate([y_ref[0, p] for p in range(y_ref.shape[1])], axis=1)
    mean = _head_sum(y, sel, selt) * inv_n
    yc = y - mean
    var = _head_sum(yc * yc, sel, selt) * inv_n
    yn = yc * lax.rsqrt(var + GN_EPS) * lnw_ref[...] + lnb_ref[...]
    ya_o[...] = ((yn + bonus_ref[...]) * g_ref[...]).astype(ya_o.dtype)

    seq_tile = pl.program_id(0) % tiles_per_seq

    @pl.when(seq_tile == 0)
    def _():
        ext_ref[0:POOL_HALO, :] = jnp.zeros((POOL_HALO, ext_ref.shape[1]), F32)

    u = u_ref[...]
    ext_ref[POOL_HALO:POOL_HALO + tm, :] = u
    t_pos = seq_tile * tm + lax.broadcasted_iota(jnp.int32, (tm, group_width), 0)
    for gi, w in enumerate(POOL_WINDOWS):
        lanes = slice(gi * group_width, (gi + 1) * group_width)
        acc = u[:, lanes]
        for j in range(1, w):
            acc = acc + ext_ref[POOL_HALO - j:POOL_HALO - j + tm, lanes]
        count = jnp.minimum(t_pos + 1, w).astype(F32)
        mixed = acc / count - u[:, lanes]
        yg = _dot_x3(mixed, pw_ref[gi])
        yb_o[:, lanes] = (yg * ps_ref[:, lanes]).astype(yb_o.dtype)
    ext_ref[0:POOL_HALO, :] = u[tm - POOL_HALO:, :]


def _post_pool(y, bonus, g, p_all, pool_blk, consts, seq, tm=256):
    m, R = bonus.shape
    lnw, lnb, sel, selt, pw, ps = consts
    tiles_per_seq = seq // tm
    row = lambda i: (i, 0)
    fixed2 = lambda i: (0, 0)
    in_specs = [pl.BlockSpec((1, y.shape[1], tm, LANES), lambda i: (i // tiles_per_seq, 0, i % tiles_per_seq, 0)),
                pl.BlockSpec((tm, R), row), pl.BlockSpec((tm, R), row),
                pl.BlockSpec((tm, R), lambda i: (i, pool_blk))]
    in_specs += [pl.BlockSpec(c.shape, fixed2) for c in (lnw, lnb, sel, selt)]
    in_specs += [pl.BlockSpec(pw.shape, lambda i: (0, 0, 0)), pl.BlockSpec(ps.shape, fixed2)]
    out = jax.ShapeDtypeStruct((m, R), BF16)
    return pl.pallas_call(
        functools.partial(_post_pool_body, tiles_per_seq=tiles_per_seq, tm=tm, group_width=R // len(POOL_WINDOWS)),
        grid=(m // tm,),
        in_specs=in_specs,
        out_specs=[pl.BlockSpec((tm, R), row)] * 2,
        out_shape=[out, out],
        scratch_shapes=[pltpu.VMEM((POOL_HALO + tm, R), F32)],
        compiler_params=_cparams(1),
        name="post_pool",
    )(y, bonus, g, p_all, lnw, lnb, sel, selt, pw, ps)


def _merge_body(ya_ref, yb_ref, pa_ref, pb_ref, ga_ref, gb_ref, o_ref, pab_ref, pbb_ref):
    _cast_weight_tiles([pa_ref, pb_ref], [pab_ref, pbb_ref])
    ma = jnp.dot(ya_ref[...], pab_ref[...], preferred_element_type=F32)
    mb = jnp.dot(yb_ref[...], pbb_ref[...], preferred_element_type=F32)
    o_ref[...] = (_sigmoid(ga_ref[...]) * ma + _sigmoid(gb_ref[...]) * mb).astype(o_ref.dtype)


def _merge(ya, yb, pa, pb, layer, p_all, ga_col, gb_col, tm=512, tn=512):
    m, R = ya.shape
    d = pa.shape[-1]
    ga_blk, gb_blk = ga_col // tn, gb_col // tn
    return pl.pallas_call(
        _merge_body,
        grid=(d // tn, m // tm),
        in_specs=[pl.BlockSpec((tm, R), lambda j, i: (i, 0)),
                  pl.BlockSpec((tm, R), lambda j, i: (i, 0)),
                  _layer_weight_spec(pa, layer, R, tn),
                  _layer_weight_spec(pb, layer, R, tn),
                  pl.BlockSpec((tm, tn), lambda j, i: (i, ga_blk + j)),
                  pl.BlockSpec((tm, tn), lambda j, i: (i, gb_blk + j))],
        out_specs=pl.BlockSpec((tm, tn), lambda j, i: (i, j)),
        out_shape=jax.ShapeDtypeStruct((m, d), BF16),
        scratch_shapes=[pltpu.VMEM((R, tn), BF16)] * 2,
        compiler_params=_cparams(2),
        name="merge",
    )(ya, yb, pa, pb, p_all, p_all)


def _router_body(x_ref, g_ref, wr_ref, h_o, idx_o, gate_o):
    x = x_ref[...]
    inv = lax.rsqrt(jnp.mean(x * x, axis=-1, keepdims=True) + NORM_EPS)
    h = x * inv * g_ref[...]
    h_o[...] = h.reshape(h_o.shape)
    logits = _dot_x3(h, wr_ref[...])
    lane = lax.broadcasted_iota(jnp.int32, logits.shape, 1)
    neg_inf = jnp.float32(-jnp.inf)
    l1 = jnp.where(lane < N_EXPERTS, logits, neg_inf)
    m1 = jnp.max(l1, axis=-1, keepdims=True)
    i1 = jnp.min(jnp.where(l1 == m1, lane, LANES), axis=-1, keepdims=True)
    l2 = jnp.where(lane == i1, neg_inf, l1)
    m2 = jnp.max(l2, axis=-1, keepdims=True)
    i2 = jnp.min(jnp.where(l2 == m2, lane, LANES), axis=-1, keepdims=True)
    e = jnp.exp(m2 - m1)
    g1 = 1.0 / (1.0 + e)
    idx_o[...] = jnp.where(lane == 0, i1, jnp.where(lane == 1, i2, 0))
    gate_o[...] = jnp.where(lane == 0, g1, jnp.where(lane == 1, e * g1, 0.0))


def _router(x, g, wr_pad, tm=256):
    m, d = x.shape
    row = lambda i: (i, 0)
    return pl.pallas_call(
        _router_body,
        grid=(m // tm,),
        in_specs=[pl.BlockSpec((tm, d), row), pl.BlockSpec((1, d), lambda i: (0, 0)),
                  pl.BlockSpec(wr_pad.shape, lambda i: (0, 0))],
        out_specs=[pl.BlockSpec((tm, d // LANES, LANES), lambda i: (i, 0, 0)),
                   pl.BlockSpec((tm, LANES), row), pl.BlockSpec((tm, LANES), row)],
        out_shape=[jax.ShapeDtypeStruct((m, d // LANES, LANES), F32), jax.ShapeDtypeStruct((m, LANES), jnp.int32),
                   jax.ShapeDtypeStruct((m, LANES), F32)],
        compiler_params=_cparams(1),
        name="router",
    )(x, g.reshape(1, d), wr_pad)


def _gather_body(tok_ref, used_ref, h_hbm, o_ref, buf_ref, sems, *, tg, n_steps):
    i = pl.program_id(0)

    def row_copy(step, r):
        slot = step % 2
        return pltpu.make_async_copy(h_hbm.at[pl.ds(tok_ref[step * tg + r], 1)], buf_ref.at[slot, pl.ds(r, 1)],
                                     sems.at[slot])

    def fetch(step):
        @pl.when(jnp.logical_and(step < n_steps, step * tg < used_ref[0]))
        def _():
            def start(q, c):
                for u in range(DMA_UNROLL):
                    row_copy(step, q * DMA_UNROLL + u).start()
                return c
            lax.fori_loop(0, tg // DMA_UNROLL, start, 0)

    @pl.when(i == 0)
    def _():
        fetch(i)

    fetch(i + 1)

    @pl.when(i * tg < used_ref[0])
    def _():
        pltpu.make_async_copy(h_hbm.at[pl.ds(0, tg)], buf_ref.at[i % 2], sems.at[i % 2]).wait()
        o_ref[...] = buf_ref[i % 2].reshape(o_ref.shape).astype(o_ref.dtype)

    @pl.when(i * tg >= used_ref[0])
    def _():
        o_ref[...] = jnp.zeros_like(o_ref)


def _gather_rows(tok, used_rows, h3, tg=256):
    p = tok.shape[0]
    _, nc, lanes = h3.shape
    n_steps = p // tg
    return pl.pallas_call(
        functools.partial(_gather_body, tg=tg, n_steps=n_steps),
        grid_spec=pltpu.PrefetchScalarGridSpec(
            num_scalar_prefetch=2,
            grid=(n_steps,),
            in_specs=[pl.BlockSpec(memory_space=pl.ANY)],
            out_specs=pl.BlockSpec((tg, nc * lanes), lambda i, tok, used: (i, 0)),
            scratch_shapes=[pltpu.VMEM((2, tg, nc, lanes), F32), pltpu.SemaphoreType.DMA((2,))]),
        out_shape=jax.ShapeDtypeStruct((p, nc * lanes), BF16),
        compiler_params=_cparams(1),
        name="moe_gather",
    )(tok, used_rows, h3)


DMA_UNROLL = 8
MOE_ROWS = 1024
MOE_SUB = 256
MOE_FF_TILE = 256
PLAN_FIELDS = 7


def _experts_body(plan_ref, x_ref, wg_ref, wu_ref, wd_ref, o_ref, wgb_ref, wub_ref, wdb_ref, *, n_items):
    w = pl.program_id(0)
    j = pl.program_id(1)
    field = lambda f: plan_ref[f * n_items + w]
    lo, hi, zero, live = field(3), field(4), field(5), field(6)
    n_sub = MOE_ROWS // MOE_SUB

    @pl.when(jnp.logical_and(zero == 1, j == 0))
    def _():
        o_ref[...] = jnp.zeros_like(o_ref)

    def rows(sl, wg, wu, wd):
        x = x_ref[sl, :]
        g = jnp.dot(x, wg, preferred_element_type=F32)
        u = jnp.dot(x, wu, preferred_element_type=F32)
        h = (g * _sigmoid(g) * u).astype(BF16)
        o_ref[sl, :] += jnp.dot(h, wd, preferred_element_type=F32)

    whole = jnp.logical_and(live == 1, jnp.logical_and(lo == 0, hi == n_sub))
    partial = jnp.logical_and(live == 1, jnp.logical_not(whole))

    @pl.when(whole)
    def _():
        rows(slice(None), wg_ref[...].astype(BF16), wu_ref[...].astype(BF16), wd_ref[...].astype(BF16))

    @pl.when(partial)
    def _():
        wgb_ref[...] = wg_ref[...].astype(BF16)
        wub_ref[...] = wu_ref[...].astype(BF16)
        wdb_ref[...] = wd_ref[...].astype(BF16)

    for s in range(n_sub):
        @pl.when(jnp.logical_and(partial, jnp.logical_and(lo <= s, s < hi)))
        def _():
            rows(slice(s * MOE_SUB, (s + 1) * MOE_SUB), wgb_ref[...], wub_ref[...], wdb_ref[...])


def _experts(plan, xs, w_gate, w_up, w_down):
    p, d = xs.shape
    f = w_gate.shape[2]
    tn = MOE_FF_TILE
    nj = f // tn
    n_items = plan.shape[0] // PLAN_FIELDS
    fld = lambda plan, f_idx, w: plan[f_idx * n_items + w]
    col = lambda plan, w, j: jnp.where(fld(plan, 6, w) == 1, j, nj - 1)
    return pl.pallas_call(
        functools.partial(_experts_body, n_items=n_items),
        grid_spec=pltpu.PrefetchScalarGridSpec(
            num_scalar_prefetch=1,
            grid=(n_items, nj),
            in_specs=[pl.BlockSpec((MOE_ROWS, d), lambda w, j, plan: (fld(plan, 1, w), 0)),
                      pl.BlockSpec((None, d, tn), lambda w, j, plan: (fld(plan, 2, w), 0, col(plan, w, j))),
                      pl.BlockSpec((None, d, tn), lambda w, j, plan: (fld(plan, 2, w), 0, col(plan, w, j))),
                      pl.BlockSpec((None, tn, d), lambda w, j, plan: (fld(plan, 2, w), col(plan, w, j), 0))],
            out_specs=pl.BlockSpec((MOE_ROWS, d), lambda w, j, plan: (fld(plan, 0, w), 0)),
            scratch_shapes=[pltpu.VMEM((d, tn), BF16), pltpu.VMEM((d, tn), BF16), pltpu.VMEM((tn, d), BF16)]),
        out_shape=jax.ShapeDtypeStruct((p, d), F32),
        compiler_params=_cparams(2),
        name="moe_experts",
    )(plan, xs, w_gate, w_up, w_down)


def _combine_body(d0_ref, d1_ref, x_ref, gate_ref, g_ref, y_hbm, o_ref, buf0, buf1, sem, *, tg, apply_norm):
    base = pl.program_id(0) * tg

    def row_copy(i, d_ref, buf):
        return pltpu.make_async_copy(y_hbm.at[pl.ds(d_ref[base + i], 1), :], buf.at[pl.ds(i, 1), :], sem)

    def start(q, c):
        for u in range(DMA_UNROLL):
            row_copy(q * DMA_UNROLL + u, d0_ref, buf0).start()
            row_copy(q * DMA_UNROLL + u, d1_ref, buf1).start()
        return c

    lax.fori_loop(0, tg // DMA_UNROLL, start, 0)
    for buf in (buf0, buf1):
        pltpu.make_async_copy(y_hbm.at[pl.ds(0, tg), :], buf, sem).wait()
    gate = gate_ref[...]
    x = x_ref[...] + gate[:, 0:1] * buf0[...] + gate[:, 1:2] * buf1[...]
    if apply_norm:
        x = x * lax.rsqrt(jnp.mean(x * x, axis=-1, keepdims=True) + NORM_EPS) * g_ref[...]
    o_ref[...] = x


def _combine(d0, d1, x, gate, g, ybuf, apply_norm, tg=256):
    m, d = x.shape
    row = lambda i, d0, d1: (i, 0)
    return pl.pallas_call(
        functools.partial(_combine_body, tg=tg, apply_norm=apply_norm),
        grid_spec=pltpu.PrefetchScalarGridSpec(
            num_scalar_prefetch=2,
            grid=(m // tg,),
            in_specs=[pl.BlockSpec((tg, d), row), pl.BlockSpec((tg, LANES), row),
                      pl.BlockSpec((1, d), lambda i, d0, d1: (0, 0)),
                      pl.BlockSpec(memory_space=pl.ANY)],
            out_specs=pl.BlockSpec((tg, d), row),
            scratch_shapes=[pltpu.VMEM((tg, d), F32), pltpu.VMEM((tg, d), F32), pltpu.SemaphoreType.DMA(())]),
        out_shape=jax.ShapeDtypeStruct((m, d), F32),
        compiler_params=_cparams(1),
        name="moe_combine",
    )(d0, d1, x, gate, g.reshape(1, d), ybuf)


def _moe_plan(top_e):
    i32 = jnp.int32
    n_tok = top_e.shape[0]
    n_slot = n_tok * TOP_K
    flat_e = top_e.reshape(n_slot)
    onehot = (flat_e[:, None] == jnp.arange(N_EXPERTS, dtype=i32)[None, :]).astype(i32)
    csum = jnp.cumsum(onehot, axis=0)
    rank = jnp.sum((csum - onehot) * onehot, axis=1)
    counts = csum[-1]
    padded = (counts + MOE_SUB - 1) // MOE_SUB * MOE_SUB
    pend = jnp.cumsum(padded)
    pstart = pend - padded
    dest = pstart[flat_e] + rank
    n_blocks = -(-(n_slot + N_EXPERTS * MOE_SUB) // MOE_ROWS)
    tok = jnp.zeros((n_blocks * MOE_ROWS,), i32).at[dest].set(jnp.arange(n_slot, dtype=i32) // TOP_K)
    used_rows = pend[-1]
    used_blocks = (used_rows + MOE_ROWS - 1) // MOE_ROWS

    n_items = n_blocks + N_EXPERTS
    blk_lo = (jnp.arange(n_blocks, dtype=i32) * MOE_ROWS)[:, None]
    lo_row = jnp.maximum(pstart[None, :], blk_lo)
    hi_row = jnp.minimum(pend[None, :], blk_lo + MOE_ROWS)
    valid = (hi_row > lo_row).reshape(-1)
    n_live = jnp.sum(valid.astype(i32))
    pos = jnp.where(valid, jnp.cumsum(valid.astype(i32)) - 1, n_items)
    compact = lambda v: jnp.zeros((n_items,), i32).at[pos].set(v.reshape(-1).astype(i32), mode="drop")
    blk_c = compact(jnp.broadcast_to(jnp.arange(n_blocks, dtype=i32)[:, None], lo_row.shape))
    exp_c = compact(jnp.broadcast_to(jnp.arange(N_EXPERTS, dtype=i32)[None, :], lo_row.shape))
    lo_c = compact((lo_row - blk_lo) // MOE_SUB)
    hi_c = compact((hi_row - blk_lo) // MOE_SUB)
    first_c = compact(lo_row == blk_lo)

    w = jnp.arange(n_items, dtype=i32)
    live = w < n_live
    tail_blk = used_blocks + (w - n_live)
    last = n_live - 1
    out_blk = jnp.where(live, blk_c, jnp.minimum(tail_blk, n_blocks - 1))
    x_blk = jnp.where(live, blk_c, blk_c[last])
    expert = jnp.where(live, exp_c, exp_c[last])
    zero = jnp.where(live, first_c, (tail_blk < n_blocks).astype(i32))
    plan = jnp.concatenate([out_blk, x_blk, expert, lo_c, hi_c, zero, live.astype(i32)]).astype(i32)
    dest = dest.reshape(n_tok, TOP_K).astype(i32)
    return tok, used_rows.astype(i32).reshape(1), plan, dest[:, 0], dest[:, 1]


def kernel(x, norm_mix, w_in, shift_mu, decay_w0, decay_up, iclr_a0, iclr_up, outgate_up, k_k, k_a, r_k, lnx_w, lnx_b, vres_down, vres_mu, vres_up, vres_v0, pool_w, pool_scale, proj_a, proj_b, w_o, norm_ffn, ffn_gate, ffn_up, ffn_down, router, moe_gate, moe_up, moe_down, norm_final):
    bsz, seq, d = x.shape
    depth = w_in.shape[0]
    R = decay_w0.shape[1]
    dl, il, gl = decay_up.shape[1], iclr_up.shape[1], outgate_up.shape[1]
    vl = vres_up.shape[1]
    assert (dl, il, vl, gl) == (96, 96, 64, 256) and 3 * R + dl + il + gl == shift_mu.shape[1]
    m = bsz * seq
    x = x.reshape(m, d)
    row = lambda t: t.reshape(1, -1)

    head_of_lane = jnp.arange(R, dtype=jnp.int32) // HEAD_SIZE
    sel = (head_of_lane[:, None] == jnp.arange(LANES, dtype=jnp.int32)[None, :]).astype(BF16)
    sel_t = sel.T
    t_idx = jnp.arange(PREP_ROWS, dtype=jnp.int32)
    same_chunk = (t_idx[:, None] // WKV_CHUNK) == (t_idx[None, :] // WKV_CHUNK)
    ones_bd = same_chunk.astype(BF16)
    tri_bd = (same_chunk & (t_idx[None, :] <= t_idx[:, None])).astype(BF16)

    v_first = None
    for l in range(depth):
        w = w_in[l]
        c0 = 3 * R
        c1, c2, c3 = c0 + dl, c0 + dl + il, c0 + dl + il + gl
        vd_w = vres_down[l - 1] if l > 0 else jnp.zeros((d, vl), F32)
        vd_mu = vres_mu[l - 1] if l > 0 else jnp.zeros((vl,), F32)
        w_all = jnp.concatenate([w[:, :c0], w[:, c3:], w[:, c0:c2], vd_w, w[:, c2:c3]], axis=1)
        mu = shift_mu[l]
        mu_rkv = row(mu[:c0])
        mu_lora = row(jnp.concatenate([mu[c0:c2], vd_mu, mu[c2:c3]]))
        pool_col = c0
        ga_col = c0 + R
        gb_col = ga_col + d
        zeros = lambda n: jnp.zeros((n, R), F32)
        dup = jnp.concatenate([decay_up[l], zeros(128 - dl)], axis=0)
        iup = jnp.concatenate([zeros(dl), iclr_up[l], zeros(256 - dl - il)], axis=0)
        gup = outgate_up[l].astype(BF16)

        h = _rmsnorm(x, norm_mix[l], BF16)
        p_all = _matmul(h, [w_all], tm=1024, tn=512, out_dtype=F32)

        consts = [mu_rkv, mu_lora, row(decay_w0[l]), row(iclr_a0[l]), row(k_k[l]), row(k_a[l]), row(r_k[l]),
                  dup, iup, gup, sel, sel_t, tri_bd, ones_bd]
        vres = None
        if l > 0:
            vup = jnp.concatenate([zeros(128 - vl), vres_up[l - 1]], axis=0).astype(BF16)
            vres = (v_first, row(vres_v0[l - 1]), vup)
        emit_v = l == 0 and depth > 1
        outs = _rwkv_prep(p_all, bsz, seq, R, consts, vres, emit_v)
        wc, bonus, g = outs[7:10]
        if emit_v:
            v_first = outs[10]
        y = _wkv(*outs[:7], wc)

        ya, yb = _post_pool(y, bonus, g, p_all, pool_col // R,
                            [row(lnx_w[l]), row(lnx_b[l]), sel, sel_t, pool_w[l], row(pool_scale[l])], seq)
        merged = _merge(ya, yb, proj_a, proj_b, l, p_all, ga_col, gb_col)
        x = _matmul(merged, [w_o], tm=1024, tn=512, out_dtype=F32, residual=x, layer=l)

        i = l // 2
        if l % 2 == 0:
            h2 = _rmsnorm(x, norm_ffn[l], BF16)
            hidden = _matmul(h2, [ffn_gate, ffn_up], tm=1024, tn=512, out_dtype=BF16, layer=i)
            x = _matmul(hidden, [ffn_down], tm=512, tn=512, out_dtype=F32, residual=x, layer=i)
            if l == depth - 1:
                x = _rmsnorm(x, norm_final, F32)
        else:
            wr_pad = jnp.concatenate([router[i], jnp.zeros((d, LANES - N_EXPERTS), F32)], axis=1)
            h3, idx, gate = _router(x, norm_ffn[l], wr_pad)
            tok, used_rows, plan, d0, d1 = _moe_plan(idx[:, :TOP_K])
            xs = _gather_rows(tok, used_rows, h3)
            ybuf = _experts(plan, xs, moe_gate[i], moe_up[i], moe_down[i])
            x = _combine(d0, d1, x, gate, norm_final, ybuf, apply_norm=l == depth - 1)
    return x.reshape(bsz, seq, d)
```

```python
import functools

import jax
import jax.numpy as jnp
from jax import lax
from jax.experimental import pallas as pl
from jax.experimental.pallas import tpu as pltpu

F32 = jnp.float32
BF16 = jnp.bfloat16

HEAD_SIZE = 64
POOL_WINDOWS = (2, 4, 8, 16)
N_EXPERTS = 8
TOP_K = 2
NORM_EPS = 1e-6
GN_EPS = 64e-5
LANES = 128
VMEM_LIMIT_BYTES = 56 * 1024 * 1024
WKV_CHUNK = 64
POOL_HALO = 16
PREP_ROWS = 256


def _cparams(n_axes):
    return pltpu.CompilerParams(dimension_semantics=("arbitrary",) * n_axes,
                                vmem_limit_bytes=VMEM_LIMIT_BYTES)


def _dot(a, b):
    return jnp.dot(a.astype(BF16), b.astype(BF16), preferred_element_type=F32)


def _split(x):
    hi = x.astype(BF16)
    lo = (x - hi.astype(F32)).astype(BF16)
    return hi, lo


def _dot_x3(a, b):
    a_hi, a_lo = _split(a)
    b_hi, b_lo = _split(b)
    d = functools.partial(jnp.dot, preferred_element_type=F32)
    return d(a_hi, b_hi) + d(a_hi, b_lo) + d(a_lo, b_hi)


def _dot_lhs_split(a, b_bf16):
    a_hi, a_lo = _split(a)
    d = functools.partial(jnp.dot, preferred_element_type=F32)
    return d(a_hi, b_bf16) + d(a_lo, b_bf16)


def _dot_tn(a, b):
    return lax.dot_general(a.astype(BF16), b.astype(BF16), (((0,), (0,)), ((), ())),
                           preferred_element_type=F32)


def _dot_nt(a, b):
    return lax.dot_general(a.astype(BF16), b.astype(BF16), (((1,), (1,)), ((), ())),
                           preferred_element_type=F32)


def _sigmoid(x):
    return 1.0 / (1.0 + jnp.exp(-x))


def _rmsnorm_body(x_ref, g_ref, o_ref):
    x = x_ref[...]
    inv = lax.rsqrt(jnp.mean(x * x, axis=-1, keepdims=True) + NORM_EPS)
    o_ref[...] = (x * inv * g_ref[...]).astype(o_ref.dtype)


def _rmsnorm(x, g, out_dtype, tm=512):
    m, d = x.shape
    return pl.pallas_call(
        _rmsnorm_body,
        grid=(m // tm,),
        in_specs=[pl.BlockSpec((tm, d), lambda i: (i, 0)),
                  pl.BlockSpec((1, d), lambda i: (0, 0))],
        out_specs=pl.BlockSpec((tm, d), lambda i: (i, 0)),
        out_shape=jax.ShapeDtypeStruct((m, d), out_dtype),
        compiler_params=_cparams(1),
        name="rmsnorm",
    )(x, g.reshape(1, d))


def _cast_weight_tiles(w_refs, wb_refs):
    @pl.when(pl.program_id(1) == 0)
    def _():
        for w_ref, wb_ref in zip(w_refs, wb_refs):
            wb_ref[...] = w_ref[...].astype(BF16)


def _mm_plain_body(a_ref, w_ref, o_ref, wb_ref):
    _cast_weight_tiles([w_ref], [wb_ref])
    o_ref[...] = jnp.dot(a_ref[...], wb_ref[...], preferred_element_type=F32).astype(o_ref.dtype)


def _mm_swiglu_body(a_ref, wg_ref, wu_ref, o_ref, wgb_ref, wub_ref):
    _cast_weight_tiles([wg_ref, wu_ref], [wgb_ref, wub_ref])
    a = a_ref[...]
    g = jnp.dot(a, wgb_ref[...], preferred_element_type=F32)
    u = jnp.dot(a, wub_ref[...], preferred_element_type=F32)
    o_ref[...] = (g * _sigmoid(g) * u).astype(o_ref.dtype)


def _mm_residual_body(a_ref, w_ref, res_ref, o_ref, wb_ref):
    _cast_weight_tiles([w_ref], [wb_ref])
    o_ref[...] = res_ref[...] + jnp.dot(a_ref[...], wb_ref[...], preferred_element_type=F32)


def _layer_weight_spec(w, layer, k, tn):
    if w.ndim == 2:
        return pl.BlockSpec((k, tn), lambda j, i: (0, j))
    return pl.BlockSpec((None, k, tn), lambda j, i: (layer, 0, j))


def _matmul(a, ws, *, tm, tn, out_dtype, residual=None, layer=0):
    m, k = a.shape
    n = ws[0].shape[-1]
    a_spec = pl.BlockSpec((tm, k), lambda j, i: (i, 0))
    w_spec = _layer_weight_spec(ws[0], layer, k, tn)
    o_spec = pl.BlockSpec((tm, tn), lambda j, i: (i, j))
    if len(ws) == 2:
        body, in_specs, args = _mm_swiglu_body, [a_spec, w_spec, w_spec], (a, ws[0], ws[1])
    elif residual is not None:
        body, in_specs, args = _mm_residual_body, [a_spec, w_spec, o_spec], (a, ws[0], residual)
    else:
        body, in_specs, args = _mm_plain_body, [a_spec, w_spec], (a, ws[0])
    return pl.pallas_call(
        body,
        grid=(n // tn, m // tm),
        in_specs=in_specs,
        out_specs=o_spec,
        out_shape=jax.ShapeDtypeStruct((m, n), out_dtype),
        scratch_shapes=[pltpu.VMEM((k, tn), BF16)] * len(ws),
        compiler_params=_cparams(2),
        name="matmul",
    )(*args)


def _norm_mm_body(*refs, n_w):
    x_ref, g_ref = refs[:2]
    w_refs = refs[2:2 + n_w]
    o_ref, hb_ref = refs[2 + n_w:]

    @pl.when(pl.program_id(1) == 0)
    def _():
        x = x_ref[...]
        inv = lax.rsqrt(jnp.mean(x * x, axis=-1, keepdims=True) + NORM_EPS)
        hb_ref[...] = (x * inv * g_ref[...]).astype(BF16)

    h = hb_ref[...]
    if n_w == 2:
        g = jnp.dot(h, w_refs[0][...].astype(BF16), preferred_element_type=F32)
        u = jnp.dot(h, w_refs[1][...].astype(BF16), preferred_element_type=F32)
        o_ref[...] = (g * _sigmoid(g) * u).astype(o_ref.dtype)
    else:
        o_ref[...] = jnp.dot(h, w_refs[0][...].astype(BF16), preferred_element_type=F32).astype(o_ref.dtype)


def _norm_matmul(x, gain, ws, *, tm, tn, out_dtype, layer=0):
    m, k = x.shape
    n = ws[0].shape[-1]
    if ws[0].ndim == 2:
        w_spec = pl.BlockSpec((k, tn), lambda i, j: (0, j))
    else:
        w_spec = pl.BlockSpec((None, k, tn), lambda i, j: (layer, 0, j))
    return pl.pallas_call(
        functools.partial(_norm_mm_body, n_w=len(ws)),
        grid=(m // tm, n // tn),
        in_specs=[pl.BlockSpec((tm, k), lambda i, j: (i, 0)), pl.BlockSpec((1, k), lambda i, j: (0, 0))]
        + [w_spec] * len(ws),
        out_specs=pl.BlockSpec((tm, tn), lambda i, j: (i, j)),
        out_shape=jax.ShapeDtypeStruct((m, n), out_dtype),
        scratch_shapes=[pltpu.VMEM((tm, k), BF16)],
        compiler_params=_cparams(2),
        name="norm_matmul",
    )(x, gain.reshape(1, k), *ws)


def _regroup_body(w_ref, vd_ref, o_ref, *, c0, c2, c3):
    w = w_ref[...]
    n_in = w.shape[1]
    tail = n_in - c3
    vl = vd_ref.shape[1]
    o_ref[:, :c0] = w[:, :c0].astype(BF16)
    o_ref[:, c0:c0 + tail] = w[:, c3:].astype(BF16)
    o_ref[:, c0 + tail:c0 + tail + (c2 - c0)] = w[:, c0:c2].astype(BF16)
    o_ref[:, c0 + tail + (c2 - c0):c0 + tail + (c2 - c0) + vl] = vd_ref[...].astype(BF16)
    o_ref[:, c0 + tail + (c2 - c0) + vl:] = w[:, c2:c3].astype(BF16)


def _regroup_in_proj(w_in, layer, vd_w, c0, c2, c3, tr=256):
    _, k, n_in = w_in.shape
    vl = vd_w.shape[1]
    n_out = n_in + vl
    return pl.pallas_call(
        functools.partial(_regroup_body, c0=c0, c2=c2, c3=c3),
        grid=(k // tr,),
        in_specs=[pl.BlockSpec((None, tr, n_in), lambda i: (layer, i, 0)),
                  pl.BlockSpec((tr, vl), lambda i: (i, 0))],
        out_specs=pl.BlockSpec((tr, n_out), lambda i: (i, 0)),
        out_shape=jax.ShapeDtypeStruct((k, n_out), BF16),
        compiler_params=_cparams(1),
        name="regroup_in_proj",
    )(w_in, vd_w)


def _head_sum(x, sel, sel_t):
    s = _dot_lhs_split(x, sel)
    return _dot_lhs_split(s, sel_t)


def _token_shift(p, carry_ref, mu):
    rows = p.shape[0]
    row = lax.broadcasted_iota(jnp.int32, p.shape, 0)
    prev = jnp.where(row == 0, carry_ref[...], pltpu.roll(p, 1, axis=0))
    carry_ref[...] = p[rows - 1:rows, :]
    return p + (prev - p) * mu


def _prep_body(*refs, tiles_per_seq, has_vres, emit_v, width):
    (rkv_ref, lora_ref, mu_rkv_ref, mu_lora_ref, w0_ref, a0_ref, kk_ref, ka_ref, rk_ref,
     dup_ref, iup_ref, gup_ref, sel_ref, selt_ref, tri_ref, ones_ref) = refs[:16]
    pos = 16
    if has_vres:
        vfirst_ref, v0_ref, vup_ref = refs[pos:pos + 3]
        pos += 3
    at_o, rt_o, bt_o, kt_o, bp_o, kp_o, vb_o, wc_o, bonus_o, g_o = refs[pos:pos + 10]
    pos += 10
    if emit_v:
        v_o = refs[pos]
        pos += 1
    carry_rkv, carry_lora = refs[pos:]
    R = width

    @pl.when(pl.program_id(0) % tiles_per_seq == 0)
    def _():
        carry_rkv[...] = jnp.zeros_like(carry_rkv)
        carry_lora[...] = jnp.zeros_like(carry_lora)

    z = _token_shift(rkv_ref[...], carry_rkv, mu_rkv_ref[...])
    zl = _token_shift(lora_ref[...], carry_lora, mu_lora_ref[...])
    r, k, v = z[:, :R], z[:, R:2 * R], z[:, 2 * R:]

    decay_logit = w0_ref[...] + _dot_x3(jnp.tanh(zl[:, :128]), dup_ref[...])
    neg = -decay_logit
    softplus = jnp.maximum(neg, 0.0) + jnp.log(1.0 + jnp.exp(-jnp.abs(neg)))
    lw = -jnp.exp(-softplus - 0.5)
    a = _sigmoid(a0_ref[...] + _dot_x3(zl[:, :256], iup_ref[...]))
    g_o[...] = _dot(_sigmoid(zl[:, 256:]), gup_ref[...])
    if has_vres:
        mix = _sigmoid(v0_ref[...] + _dot(zl[:, 128:256], vup_ref[...]))
        v = v + (vfirst_ref[...] - v) * mix
    if emit_v:
        v_o[...] = v

    sel, selt = sel_ref[...], selt_ref[...]
    kk = k * kk_ref[...]
    norm = jnp.sqrt(_head_sum(kk * kk, sel, selt))
    kk = kk / jnp.maximum(norm, 1e-12)
    k = k * (1.0 + (a - 1.0) * ka_ref[...])
    bonus_o[...] = _head_sum(r * k * rk_ref[...], sel, selt) * v

    d = functools.partial(jnp.dot, preferred_element_type=F32)
    lw_hi = lw.astype(BF16)
    rest = lw - lw_hi.astype(F32)
    lw_mid = rest.astype(BF16)
    lw_lo = (rest - lw_mid.astype(F32)).astype(BF16)
    tri, ones = tri_ref[...], ones_ref[...]
    cum = d(tri, lw_hi) + d(tri, lw_mid) + d(tri, lw_lo)
    total = d(ones, lw_hi) + d(ones, lw_mid) + d(ones, lw_lo)
    w_inv = jnp.exp(-cum)
    w_rem = jnp.exp(total - cum)
    b = kk * a
    wc_o[...] = jnp.exp(total)
    pairs = R // LANES
    for o_ref, val in ((at_o, -kk * jnp.exp(cum - lw)), (rt_o, r * jnp.exp(cum)), (bt_o, b * w_inv),
                       (kt_o, k * w_inv), (bp_o, b * w_rem), (kp_o, k * w_rem), (vb_o, v)):
        val = val.astype(BF16)
        for p in range(pairs):
            o_ref[0, p] = val[:, p * LANES:(p + 1) * LANES]


def _rwkv_prep(p_all, bsz, seq, width, consts, vres, emit_v):
    tm = PREP_ROWS
    m = p_all.shape[0]
    R = width
    lora_w = 512
    lora_blk = (p_all.shape[1] - lora_w) // lora_w
    tiles_per_seq = seq // tm
    row = lambda i: (i, 0)
    fixed = lambda i: (0, 0)
    in_specs = [pl.BlockSpec((tm, 3 * R), row),
                pl.BlockSpec((tm, lora_w), lambda i: (i, lora_blk))]
    args = [p_all, p_all]
    for c in consts:
        in_specs.append(pl.BlockSpec(c.shape, fixed))
        args.append(c)
    if vres is not None:
        vfirst, v0, vup = vres
        in_specs += [pl.BlockSpec((tm, R), row), pl.BlockSpec(v0.shape, fixed), pl.BlockSpec(vup.shape, fixed)]
        args += [vfirst, v0, vup]
    pairs = R // LANES
    pm_spec = pl.BlockSpec((1, pairs, tm, LANES), lambda i: (i // tiles_per_seq, 0, i % tiles_per_seq, 0))
    pm_shape = jax.ShapeDtypeStruct((bsz, pairs, seq, LANES), BF16)
    nat_spec = pl.BlockSpec((tm, R), row)
    nat_shape = jax.ShapeDtypeStruct((m, R), F32)
    n_nat = 4 if emit_v else 3
    return pl.pallas_call(
        functools.partial(_prep_body, tiles_per_seq=tiles_per_seq, has_vres=vres is not None, emit_v=emit_v,
                          width=R),
        grid=(m // tm,),
        in_specs=in_specs,
        out_specs=[pm_spec] * 7 + [nat_spec] * n_nat,
        out_shape=[pm_shape] * 7 + [nat_shape] * n_nat,
        scratch_shapes=[pltpu.VMEM((1, 3 * R), F32), pltpu.VMEM((1, lora_w), F32)],
        compiler_params=_cparams(1),
        name="rwkv_prep",
    )(*args)


def _wkv_body(at_ref, rt_ref, bt_ref, kt_ref, bp_ref, kp_ref, v_ref, wc_ref, y_ref, state_ref, *, pairs, chunks):
    C = WKV_CHUNK
    hs = HEAD_SIZE

    @pl.when(pl.program_id(1) == 0)
    def _():
        state_ref[...] = jnp.zeros_like(state_ref)

    def head_masks(rows, width):
        lane = lax.broadcasted_iota(jnp.int32, (rows, width), 1)
        first = (lane % LANES) < hs
        return first, jnp.logical_not(first)

    def stack_heads(x):
        m0, m1 = head_masks(*x.shape)
        zero = jnp.zeros_like(x)
        return jnp.concatenate([jnp.where(m0, x, zero), jnp.where(m1, x, zero)], axis=0)

    row = lax.broadcasted_iota(jnp.int32, (C, LANES), 0)
    col = lax.broadcasted_iota(jnp.int32, (C, LANES), 1) % hs
    strict, incl = row > col, row >= col
    sq_r = lax.broadcasted_iota(jnp.int32, (LANES, LANES), 0)
    sq_c = lax.broadcasted_iota(jnp.int32, (LANES, LANES), 1)
    same_head = (sq_r // hs) == (sq_c // hs)
    eye = sq_r == sq_c
    levels = C.bit_length() - 1
    dot = functools.partial(jnp.dot, preferred_element_type=F32)
    P = range(pairs)

    def step(c, carry):
        sl = pl.ds(pl.multiple_of(c * C, C), C)
        at = [at_ref[0, p, sl, :] for p in P]
        rt = [rt_ref[0, p, sl, :] for p in P]
        v = [v_ref[0, p, sl, :] for p in P]
        vs = [stack_heads(v[p]) for p in P]
        lhs = [jnp.concatenate([at[p], rt[p]], axis=0) for p in P]
        rhs = [jnp.concatenate([stack_heads(bt_ref[0, p, sl, :]), stack_heads(kt_ref[0, p, sl, :])], axis=0)
               for p in P]
        abk = [_dot_nt(lhs[p], rhs[p]) for p in P]
        n = [jnp.where(strict, abk[p][:C, :LANES], 0.0) for p in P]
        a_rb = [jnp.where(incl, abk[p][C:, :LANES], 0.0) for p in P]
        a_ak = [jnp.where(strict, abk[p][:C, LANES:], 0.0) for p in P]
        a_rk = [jnp.where(incl, abk[p][C:, LANES:], 0.0) for p in P]
        av = [dot(a_ak[p].astype(BF16), vs[p]) for p in P]
        x = [jnp.concatenate([at[p].astype(F32), av[p]], axis=1) for p in P]
        for lvl in range(levels):
            last = lvl + 1 == levels
            for p in P:
                z = x[p] if last else jnp.concatenate([n[p], x[p]], axis=1)
                out = dot(n[p].astype(BF16), stack_heads(z.astype(BF16)))
                if last:
                    x[p] = x[p] + out
                else:
                    n[p] = out[:, :LANES]
                    x[p] = x[p] + out[:, LANES:]
        xb = [x[p].astype(BF16) for p in P]
        zeros_c = jnp.zeros((C, LANES), BF16)
        bx = [_dot_tn(jnp.concatenate([bp_ref[0, p, sl, :], kp_ref[0, p, sl, :]], axis=0),
                      jnp.concatenate([xb[p], jnp.concatenate([zeros_c, v[p]], axis=1)], axis=0)) for p in P]
        zeros_2c = jnp.zeros((2 * C, LANES), BF16)
        rx = [dot(jnp.concatenate([a_rb[p], a_rk[p]], axis=1).astype(BF16),
                  jnp.concatenate([stack_heads(xb[p]), jnp.concatenate([zeros_2c, vs[p]], axis=1)], axis=0))
              for p in P]
        for p in P:
            state = state_ref[p]
            wc = wc_ref[pl.ds(pl.multiple_of(c * C, C), 1), p * LANES:(p + 1) * LANES]
            trans = jnp.where(same_head, bx[p][:, :LANES], 0.0) + jnp.where(eye, wc, 0.0)
            inject = jnp.where(same_head, bx[p][:, LANES:], 0.0)
            r_eff = rt[p].astype(F32) + rx[p][:, :LANES]
            y_ref[0, p, sl, :] = _dot(r_eff, state) + rx[p][:, LANES:]
            state_ref[p] = _dot_x3(trans, state) + inject
        return carry

    lax.fori_loop(0, chunks, step, 0)


def _wkv(at, rt, bt, kt, bp, kp, v, wc, pairs=8, span=512):
    bsz, n_pairs, seq, _ = at.shape
    groups = n_pairs // pairs
    spec = pl.BlockSpec((1, pairs, span, LANES), lambda g, t: (g // groups, g % groups, t, 0))
    wc_spec = pl.BlockSpec((span, pairs * LANES), lambda g, t: ((g // groups) * (seq // span) + t, g % groups))
    return pl.pallas_call(
        functools.partial(_wkv_body, pairs=pairs, chunks=span // WKV_CHUNK),
        grid=(bsz * groups, seq // span),
        in_specs=[spec] * 7 + [wc_spec],
        out_specs=spec,
        out_shape=jax.ShapeDtypeStruct((bsz, n_pairs, seq, LANES), F32),
        scratch_shapes=[pltpu.VMEM((pairs, LANES, LANES), F32)],
        compiler_params=_cparams(2),
        name="wkv7",
    )(at, rt, bt, kt, bp, kp, v, wc)


def _post_pool_body(y_ref, bonus_ref, g_ref, u_ref, lnw_ref, lnb_ref, sel_ref, selt_ref,
                    pw_ref, ps_ref, ya_o, yb_o, ext_ref, *, tiles_per_seq, tm, group_width):
    sel, selt = sel_ref[...], selt_ref[...]
    inv_n = 1.0 / HEAD_SIZE
    y = jnp.concatenate([y_ref[0, p] for p in range(y_ref.shape[1])], axis=1)
    mean = _head_sum(y, sel, selt) * inv_n
    yc = y - mean
    var = _head_sum(yc * yc, sel, selt) * inv_n
    yn = yc * lax.rsqrt(var + GN_EPS) * lnw_ref[...] + lnb_ref[...]
    ya_o[...] = ((yn + bonus_ref[...]) * g_ref[...]).astype(ya_o.dtype)

    seq_tile = pl.program_id(0) % tiles_per_seq

    @pl.when(seq_tile == 0)
    def _():
        ext_ref[0:POOL_HALO, :] = jnp.zeros((POOL_HALO, ext_ref.shape[1]), F32)

    u = u_ref[...]
    ext_ref[POOL_HALO:POOL_HALO + tm, :] = u
    t_pos = seq_tile * tm + lax.broadcasted_iota(jnp.int32, (tm, group_width), 0)
    for gi, w in enumerate(POOL_WINDOWS):
        lanes = slice(gi * group_width, (gi + 1) * group_width)
        acc = u[:, lanes]
        for j in range(1, w):
            acc = acc + ext_ref[POOL_HALO - j:POOL_HALO - j + tm, lanes]
        count = jnp.minimum(t_pos + 1, w).astype(F32)
        mixed = acc / count - u[:, lanes]
        yg = _dot_x3(mixed, pw_ref[gi])
        yb_o[:, lanes] = (yg * ps_ref[:, lanes]).astype(yb_o.dtype)
    ext_ref[0:POOL_HALO, :] = u[tm - POOL_HALO:, :]


def _post_pool(y, bonus, g, p_all, pool_blk, consts, seq, tm=256):
    m, R = bonus.shape
    lnw, lnb, sel, selt, pw, ps = consts
    tiles_per_seq = seq // tm
    row = lambda i: (i, 0)
    fixed2 = lambda i: (0, 0)
    in_specs = [pl.BlockSpec((1, y.shape[1], tm, LANES), lambda i: (i // tiles_per_seq, 0, i % tiles_per_seq, 0)),
                pl.BlockSpec((tm, R), row), pl.BlockSpec((tm, R), row),
                pl.BlockSpec((tm, R), lambda i: (i, pool_blk))]
    in_specs += [pl.BlockSpec(c.shape, fixed2) for c in (lnw, lnb, sel, selt)]
    in_specs += [pl.BlockSpec(pw.shape, lambda i: (0, 0, 0)), pl.BlockSpec(ps.shape, fixed2)]
    out = jax.ShapeDtypeStruct((m, R), BF16)
    return pl.pallas_call(
        functools.partial(_post_pool_body, tiles_per_seq=tiles_per_seq, tm=tm, group_width=R // len(POOL_WINDOWS)),
        grid=(m // tm,),
        in_specs=in_specs,
        out_specs=[pl.BlockSpec((tm, R), row)] * 2,
        out_shape=[out, out],
        scratch_shapes=[pltpu.VMEM((POOL_HALO + tm, R), F32)],
        compiler_params=_cparams(1),
        name="post_pool",
    )(y, bonus, g, p_all, lnw, lnb, sel, selt, pw, ps)


def _merge_body(ya_ref, yb_ref, pa_ref, pb_ref, ga_ref, gb_ref, o_ref, pab_ref, pbb_ref):
    _cast_weight_tiles([pa_ref, pb_ref], [pab_ref, pbb_ref])
    ma = jnp.dot(ya_ref[...], pab_ref[...], preferred_element_type=F32)
    mb = jnp.dot(yb_ref[...], pbb_ref[...], preferred_element_type=F32)
    o_ref[...] = (_sigmoid(ga_ref[...]) * ma + _sigmoid(gb_ref[...]) * mb).astype(o_ref.dtype)


def _merge(ya, yb, pa, pb, layer, p_all, ga_col, gb_col, tm=512, tn=512):
    m, R = ya.shape
    d = pa.shape[-1]
    ga_blk, gb_blk = ga_col // tn, gb_col // tn
    return pl.pallas_call(
        _merge_body,
        grid=(d // tn, m // tm),
        in_specs=[pl.BlockSpec((tm, R), lambda j, i: (i, 0)),
                  pl.BlockSpec((tm, R), lambda j, i: (i, 0)),
                  _layer_weight_spec(pa, layer, R, tn),
                  _layer_weight_spec(pb, layer, R, tn),
                  pl.BlockSpec((tm, tn), lambda j, i: (i, ga_blk + j)),
                  pl.BlockSpec((tm, tn), lambda j, i: (i, gb_blk + j))],
        out_specs=pl.BlockSpec((tm, tn), lambda j, i: (i, j)),
        out_shape=jax.ShapeDtypeStruct((m, d), BF16),
        scratch_shapes=[pltpu.VMEM((R, tn), BF16)] * 2,
        compiler_params=_cparams(2),
        name="merge",
    )(ya, yb, pa, pb, p_all, p_all)


def _router_body(x_ref, g_ref, wr_ref, tri_ref, h_o, idx_o, gate_o, cnt_o, carry_ref):
    @pl.when(pl.program_id(0) == 0)
    def _():
        carry_ref[...] = jnp.zeros_like(carry_ref)

    x = x_ref[...]
    inv = lax.rsqrt(jnp.mean(x * x, axis=-1, keepdims=True) + NORM_EPS)
    h = x * inv * g_ref[...]
    h_o[...] = h.reshape(h_o.shape)
    logits = _dot_x3(h, wr_ref[...])
    lane = lax.broadcasted_iota(jnp.int32, logits.shape, 1)
    neg_inf = jnp.float32(-jnp.inf)
    l1 = jnp.where(lane < N_EXPERTS, logits, neg_inf)
    m1 = jnp.max(l1, axis=-1, keepdims=True)
    i1 = jnp.min(jnp.where(l1 == m1, lane, LANES), axis=-1, keepdims=True)
    l2 = jnp.where(lane == i1, neg_inf, l1)
    m2 = jnp.max(l2, axis=-1, keepdims=True)
    i2 = jnp.min(jnp.where(l2 == m2, lane, LANES), axis=-1, keepdims=True)
    e = jnp.exp(m2 - m1)
    g1 = 1.0 / (1.0 + e)
    gate_o[...] = jnp.where(lane == 0, g1, jnp.where(lane == 1, e * g1, 0.0))

    first, second = lane == i1, lane == i2
    picked = jnp.where(jnp.logical_or(first, second), 1.0, 0.0)
    before = carry_ref[...] + jnp.dot(tri_ref[...], picked.astype(BF16), preferred_element_type=F32)
    rank1 = jnp.sum(jnp.where(first, before, 0.0), axis=-1, keepdims=True).astype(jnp.int32)
    rank2 = jnp.sum(jnp.where(second, before, 0.0), axis=-1, keepdims=True).astype(jnp.int32)
    idx_o[...] = jnp.where(lane == 0, i1, jnp.where(lane == 1, i2, jnp.where(lane == 2, rank1,
                                                                              jnp.where(lane == 3, rank2, 0))))
    carry_ref[...] += jnp.sum(picked, axis=0, keepdims=True)
    cnt_o[...] = jnp.broadcast_to(carry_ref[...], cnt_o.shape).astype(jnp.int32)


def _router(x, g, wr_pad, tm=256):
    m, d = x.shape
    row = lambda i: (i, 0)
    t_idx = jnp.arange(tm, dtype=jnp.int32)
    tri_strict = (t_idx[None, :] < t_idx[:, None]).astype(BF16)
    return pl.pallas_call(
        _router_body,
        grid=(m // tm,),
        in_specs=[pl.BlockSpec((tm, d), row), pl.BlockSpec((1, d), lambda i: (0, 0)),
                  pl.BlockSpec(wr_pad.shape, lambda i: (0, 0)), pl.BlockSpec((tm, tm), lambda i: (0, 0))],
        out_specs=[pl.BlockSpec((tm, d // LANES, LANES), lambda i: (i, 0, 0)),
                   pl.BlockSpec((tm, LANES), row), pl.BlockSpec((tm, LANES), row),
                   pl.BlockSpec((8, LANES), lambda i: (0, 0))],
        out_shape=[jax.ShapeDtypeStruct((m, d // LANES, LANES), F32), jax.ShapeDtypeStruct((m, LANES), jnp.int32),
                   jax.ShapeDtypeStruct((m, LANES), F32), jax.ShapeDtypeStruct((8, LANES), jnp.int32)],
        scratch_shapes=[pltpu.VMEM((1, LANES), F32)],
        compiler_params=_cparams(1),
        name="router",
    )(x, g.reshape(1, d), wr_pad, tri_strict)


def _gather_body(tok_ref, used_ref, h_hbm, o_ref, buf_ref, sems, *, tg, n_steps):
    i = pl.program_id(0)

    def row_copy(step, r):
        slot = step % 2
        return pltpu.make_async_copy(h_hbm.at[pl.ds(tok_ref[step * tg + r], 1)], buf_ref.at[slot, pl.ds(r, 1)],
                                     sems.at[slot])

    def fetch(step):
        @pl.when(jnp.logical_and(step < n_steps, step * tg < used_ref[0]))
        def _():
            def start(q, c):
                for u in range(DMA_UNROLL):
                    row_copy(step, q * DMA_UNROLL + u).start()
                return c
            lax.fori_loop(0, tg // DMA_UNROLL, start, 0)

    @pl.when(i == 0)
    def _():
        fetch(i)

    fetch(i + 1)

    @pl.when(i * tg < used_ref[0])
    def _():
        pltpu.make_async_copy(h_hbm.at[pl.ds(0, tg)], buf_ref.at[i % 2], sems.at[i % 2]).wait()
        o_ref[...] = buf_ref[i % 2].reshape(o_ref.shape).astype(o_ref.dtype)

    @pl.when(i * tg >= used_ref[0])
    def _():
        o_ref[...] = jnp.zeros_like(o_ref)


def _gather_rows(tok, used_rows, h3, tg=256):
    p = tok.shape[0]
    _, nc, lanes = h3.shape
    n_steps = p // tg
    return pl.pallas_call(
        functools.partial(_gather_body, tg=tg, n_steps=n_steps),
        grid_spec=pltpu.PrefetchScalarGridSpec(
            num_scalar_prefetch=2,
            grid=(n_steps,),
            in_specs=[pl.BlockSpec(memory_space=pl.ANY)],
            out_specs=pl.BlockSpec((tg, nc * lanes), lambda i, tok, used: (i, 0)),
            scratch_shapes=[pltpu.VMEM((2, tg, nc, lanes), F32), pltpu.SemaphoreType.DMA((2,))]),
        out_shape=jax.ShapeDtypeStruct((p, nc * lanes), BF16),
        compiler_params=_cparams(1),
        name="moe_gather",
    )(tok, used_rows, h3)


DMA_UNROLL = 8
MOE_ROWS = 1024
MOE_SUB = 256
MOE_FF_TILE = 256
PLAN_FIELDS = 7


def _experts_body(plan_ref, x_ref, wg_ref, wu_ref, wd_ref, o_ref, wgb_ref, wub_ref, wdb_ref, *, n_items):
    w = pl.program_id(0)
    j = pl.program_id(1)
    field = lambda f: plan_ref[f * n_items + w]
    lo, hi, zero, live = field(3), field(4), field(5), field(6)
    n_sub = MOE_ROWS // MOE_SUB

    @pl.when(jnp.logical_and(zero == 1, j == 0))
    def _():
        o_ref[...] = jnp.zeros_like(o_ref)

    def rows(sl, wg, wu, wd):
        x = x_ref[sl, :]
        g = jnp.dot(x, wg, preferred_element_type=F32)
        u = jnp.dot(x, wu, preferred_element_type=F32)
        h = (g * _sigmoid(g) * u).astype(BF16)
        o_ref[sl, :] += jnp.dot(h, wd, preferred_element_type=F32)

    whole = jnp.logical_and(live == 1, jnp.logical_and(lo == 0, hi == n_sub))
    partial = jnp.logical_and(live == 1, jnp.logical_not(whole))

    @pl.when(whole)
    def _():
        rows(slice(None), wg_ref[...].astype(BF16), wu_ref[...].astype(BF16), wd_ref[...].astype(BF16))

    @pl.when(partial)
    def _():
        wgb_ref[...] = wg_ref[...].astype(BF16)
        wub_ref[...] = wu_ref[...].astype(BF16)
        wdb_ref[...] = wd_ref[...].astype(BF16)

    for s in range(n_sub):
        @pl.when(jnp.logical_and(partial, jnp.logical_and(lo <= s, s < hi)))
        def _():
            rows(slice(s * MOE_SUB, (s + 1) * MOE_SUB), wgb_ref[...], wub_ref[...], wdb_ref[...])


def _experts(plan, xs, w_gate, w_up, w_down):
    p, d = xs.shape
    f = w_gate.shape[2]
    tn = MOE_FF_TILE
    nj = f // tn
    n_items = plan.shape[0] // PLAN_FIELDS
    fld = lambda plan, f_idx, w: plan[f_idx * n_items + w]
    col = lambda plan, w, j: jnp.where(fld(plan, 6, w) == 1, j, nj - 1)
    return pl.pallas_call(
        functools.partial(_experts_body, n_items=n_items),
        grid_spec=pltpu.PrefetchScalarGridSpec(
            num_scalar_prefetch=1,
            grid=(n_items, nj),
            in_specs=[pl.BlockSpec((MOE_ROWS, d), lambda w, j, plan: (fld(plan, 1, w), 0)),
                      pl.BlockSpec((None, d, tn), lambda w, j, plan: (fld(plan, 2, w), 0, col(plan, w, j))),
                      pl.BlockSpec((None, d, tn), lambda w, j, plan: (fld(plan, 2, w), 0, col(plan, w, j))),
                      pl.BlockSpec((None, tn, d), lambda w, j, plan: (fld(plan, 2, w), col(plan, w, j), 0))],
            out_specs=pl.BlockSpec((MOE_ROWS, d), lambda w, j, plan: (fld(plan, 0, w), 0)),
            scratch_shapes=[pltpu.VMEM((d, tn), BF16), pltpu.VMEM((d, tn), BF16), pltpu.VMEM((tn, d), BF16)]),
        out_shape=jax.ShapeDtypeStruct((p, d), F32),
        compiler_params=_cparams(2),
        name="moe_experts",
    )(plan, xs, w_gate, w_up, w_down)


def _combine_body(d0_ref, d1_ref, x_ref, gate_ref, g_ref, y_hbm, o_ref, buf0, buf1, sem, *, tg, apply_norm):
    base = pl.program_id(0) * tg

    def row_copy(i, d_ref, buf):
        return pltpu.make_async_copy(y_hbm.at[pl.ds(d_ref[base + i], 1), :], buf.at[pl.ds(i, 1), :], sem)

    def start(q, c):
        for u in range(DMA_UNROLL):
            row_copy(q * DMA_UNROLL + u, d0_ref, buf0).start()
            row_copy(q * DMA_UNROLL + u, d1_ref, buf1).start()
        return c

    lax.fori_loop(0, tg // DMA_UNROLL, start, 0)
    for buf in (buf0, buf1):
        pltpu.make_async_copy(y_hbm.at[pl.ds(0, tg), :], buf, sem).wait()
    gate = gate_ref[...]
    x = x_ref[...] + gate[:, 0:1] * buf0[...] + gate[:, 1:2] * buf1[...]
    if apply_norm:
        x = x * lax.rsqrt(jnp.mean(x * x, axis=-1, keepdims=True) + NORM_EPS) * g_ref[...]
    o_ref[...] = x


def _combine(d0, d1, x, gate, g, ybuf, apply_norm, tg=256):
    m, d = x.shape
    row = lambda i, d0, d1: (i, 0)
    return pl.pallas_call(
        functools.partial(_combine_body, tg=tg, apply_norm=apply_norm),
        grid_spec=pltpu.PrefetchScalarGridSpec(
            num_scalar_prefetch=2,
            grid=(m // tg,),
            in_specs=[pl.BlockSpec((tg, d), row), pl.BlockSpec((tg, LANES), row),
                      pl.BlockSpec((1, d), lambda i, d0, d1: (0, 0)),
                      pl.BlockSpec(memory_space=pl.ANY)],
            out_specs=pl.BlockSpec((tg, d), row),
            scratch_shapes=[pltpu.VMEM((tg, d), F32), pltpu.VMEM((tg, d), F32), pltpu.SemaphoreType.DMA(())]),
        out_shape=jax.ShapeDtypeStruct((m, d), F32),
        compiler_params=_cparams(1),
        name="moe_combine",
    )(d0, d1, x, gate, g.reshape(1, d), ybuf)


def _moe_plan(top_e, rank, counts):
    i32 = jnp.int32
    n_tok = top_e.shape[0]
    n_slot = n_tok * TOP_K
    flat_e = top_e.reshape(n_slot)
    rank = rank.reshape(n_slot)
    padded = (counts + MOE_SUB - 1) // MOE_SUB * MOE_SUB
    pend = jnp.cumsum(padded)
    pstart = pend - padded
    dest = pstart[flat_e] + rank
    n_blocks = -(-(n_slot + N_EXPERTS * MOE_SUB) // MOE_ROWS)
    tok = jnp.zeros((n_blocks * MOE_ROWS,), i32).at[dest].set(jnp.arange(n_slot, dtype=i32) // TOP_K)
    used_rows = pend[-1]
    used_blocks = (used_rows + MOE_ROWS - 1) // MOE_ROWS

    n_items = n_blocks + N_EXPERTS
    blk_lo = (jnp.arange(n_blocks, dtype=i32) * MOE_ROWS)[:, None]
    lo_row = jnp.maximum(pstart[None, :], blk_lo)
    hi_row = jnp.minimum(pend[None, :], blk_lo + MOE_ROWS)
    valid = (hi_row > lo_row).reshape(-1)
    n_live = jnp.sum(valid.astype(i32))
    pos = jnp.where(valid, jnp.cumsum(valid.astype(i32)) - 1, n_items)
    compact = lambda v: jnp.zeros((n_items,), i32).at[pos].set(v.reshape(-1).astype(i32), mode="drop")
    blk_c = compact(jnp.broadcast_to(jnp.arange(n_blocks, dtype=i32)[:, None], lo_row.shape))
    exp_c = compact(jnp.broadcast_to(jnp.arange(N_EXPERTS, dtype=i32)[None, :], lo_row.shape))
    lo_c = compact((lo_row - blk_lo) // MOE_SUB)
    hi_c = compact((hi_row - blk_lo) // MOE_SUB)
    first_c = compact(lo_row == blk_lo)

    w = jnp.arange(n_items, dtype=i32)
    live = w < n_live
    tail_blk = used_blocks + (w - n_live)
    last = n_live - 1
    out_blk = jnp.where(live, blk_c, jnp.minimum(tail_blk, n_blocks - 1))
    x_blk = jnp.where(live, blk_c, blk_c[last])
    expert = jnp.where(live, exp_c, exp_c[last])
    zero = jnp.where(live, first_c, (tail_blk < n_blocks).astype(i32))
    plan = jnp.concatenate([out_blk, x_blk, expert, lo_c, hi_c, zero, live.astype(i32)]).astype(i32)
    dest = dest.reshape(n_tok, TOP_K).astype(i32)
    return tok, used_rows.astype(i32).reshape(1), plan, dest[:, 0], dest[:, 1]


def kernel(x, norm_mix, w_in, shift_mu, decay_w0, decay_up, iclr_a0, iclr_up, outgate_up, k_k, k_a, r_k, lnx_w, lnx_b, vres_down, vres_mu, vres_up, vres_v0, pool_w, pool_scale, proj_a, proj_b, w_o, norm_ffn, ffn_gate, ffn_up, ffn_down, router, moe_gate, moe_up, moe_down, norm_final):
    bsz, seq, d = x.shape
    depth = w_in.shape[0]
    R = decay_w0.shape[1]
    dl, il, gl = decay_up.shape[1], iclr_up.shape[1], outgate_up.shape[1]
    vl = vres_up.shape[1]
    assert (dl, il, vl, gl) == (96, 96, 64, 256) and 3 * R + dl + il + gl == shift_mu.shape[1]
    m = bsz * seq
    x = x.reshape(m, d)
    row = lambda t: t.reshape(1, -1)

    head_of_lane = jnp.arange(R, dtype=jnp.int32) // HEAD_SIZE
    sel = (head_of_lane[:, None] == jnp.arange(LANES, dtype=jnp.int32)[None, :]).astype(BF16)
    sel_t = sel.T
    t_idx = jnp.arange(PREP_ROWS, dtype=jnp.int32)
    same_chunk = (t_idx[:, None] // WKV_CHUNK) == (t_idx[None, :] // WKV_CHUNK)
    ones_bd = same_chunk.astype(BF16)
    tri_bd = (same_chunk & (t_idx[None, :] <= t_idx[:, None])).astype(BF16)

    v_first = None
    for l in range(depth):
        c0 = 3 * R
        c2, c3 = c0 + dl + il, c0 + dl + il + gl
        vd_w = vres_down[l - 1] if l > 0 else jnp.zeros((d, vl), F32)
        vd_mu = vres_mu[l - 1] if l > 0 else jnp.zeros((vl,), F32)
        w_all = _regroup_in_proj(w_in, l, vd_w, c0, c2, c3)
        mu = shift_mu[l]
        mu_rkv = row(mu[:c0])
        mu_lora = row(jnp.concatenate([mu[c0:c2], vd_mu, mu[c2:c3]]))
        pool_col = c0
        ga_col = c0 + R
        gb_col = ga_col + d
        zeros = lambda n: jnp.zeros((n, R), F32)
        dup = jnp.concatenate([decay_up[l], zeros(128 - dl)], axis=0)
        iup = jnp.concatenate([zeros(dl), iclr_up[l], zeros(256 - dl - il)], axis=0)
        gup = outgate_up[l].astype(BF16)

        p_all = _norm_matmul(x, norm_mix[l], [w_all], tm=1024, tn=512, out_dtype=F32)

        consts = [mu_rkv, mu_lora, row(decay_w0[l]), row(iclr_a0[l]), row(k_k[l]), row(k_a[l]), row(r_k[l]),
                  dup, iup, gup, sel, sel_t, tri_bd, ones_bd]
        vres = None
        if l > 0:
            vup = jnp.concatenate([zeros(128 - vl), vres_up[l - 1]], axis=0).astype(BF16)
            vres = (v_first, row(vres_v0[l - 1]), vup)
        emit_v = l == 0 and depth > 1
        outs = _rwkv_prep(p_all, bsz, seq, R, consts, vres, emit_v)
        wc, bonus, g = outs[7:10]
        if emit_v:
            v_first = outs[10]
        y = _wkv(*outs[:7], wc)

        ya, yb = _post_pool(y, bonus, g, p_all, pool_col // R,
                            [row(lnx_w[l]), row(lnx_b[l]), sel, sel_t, pool_w[l], row(pool_scale[l])], seq)
        merged = _merge(ya, yb, proj_a, proj_b, l, p_all, ga_col, gb_col)
        x = _matmul(merged, [w_o], tm=1024, tn=512, out_dtype=F32, residual=x, layer=l)

        i = l // 2
        if l % 2 == 0:
            hidden = _norm_matmul(x, norm_ffn[l], [ffn_gate, ffn_up], tm=1024, tn=512, out_dtype=BF16, layer=i)
            x = _matmul(hidden, [ffn_down], tm=512, tn=512, out_dtype=F32, residual=x, layer=i)
            if l == depth - 1:
                x = _rmsnorm(x, norm_final, F32)
        else:
            wr_pad = jnp.concatenate([router[i], jnp.zeros((d, LANES - N_EXPERTS), F32)], axis=1)
            h3, idx, gate, counts = _router(x, norm_ffn[l], wr_pad)
            tok, used_rows, plan, d0, d1 = _moe_plan(idx[:, :TOP_K], idx[:, TOP_K:2 * TOP_K], counts[0, :N_EXPERTS])
            xs = _gather_rows(tok, used_rows, h3)
            ybuf = _experts(plan, xs, moe_gate[i], moe_up[i], moe_down[i])
            x = _combine(d0, d1, x, gate, norm_final, ybuf, apply_norm=l == depth - 1)
    return x.reshape(bsz, seq, d)
```

```python
import functools

import jax
import jax.numpy as jnp
from jax import lax
from jax.experimental import pallas as pl
from jax.experimental.pallas import tpu as pltpu

F32 = jnp.float32
BF16 = jnp.bfloat16

HEAD_SIZE = 64
POOL_WINDOWS = (2, 4, 8, 16)
N_EXPERTS = 8
TOP_K = 2
NORM_EPS = 1e-6
GN_EPS = 64e-5
LANES = 128
VMEM_LIMIT_BYTES = 56 * 1024 * 1024
WKV_CHUNK = 64
POOL_HALO = 16
PREP_ROWS = 256


def _cparams(n_axes):
    return pltpu.CompilerParams(dimension_semantics=("arbitrary",) * n_axes,
                                vmem_limit_bytes=VMEM_LIMIT_BYTES)


def _dot(a, b):
    return jnp.dot(a.astype(BF16), b.astype(BF16), preferred_element_type=F32)


def _split(x):
    hi = x.astype(BF16)
    lo = (x - hi.astype(F32)).astype(BF16)
    return hi, lo


def _dot_x3(a, b):
    a_hi, a_lo = _split(a)
    b_hi, b_lo = _split(b)
    d = functools.partial(jnp.dot, preferred_element_type=F32)
    return d(a_hi, b_hi) + d(a_hi, b_lo) + d(a_lo, b_hi)


def _dot_lhs_split(a, b_bf16):
    a_hi, a_lo = _split(a)
    d = functools.partial(jnp.dot, preferred_element_type=F32)
    return d(a_hi, b_bf16) + d(a_lo, b_bf16)


def _dot_tn(a, b):
    return lax.dot_general(a.astype(BF16), b.astype(BF16), (((0,), (0,)), ((), ())),
                           preferred_element_type=F32)


def _dot_nt(a, b):
    return lax.dot_general(a.astype(BF16), b.astype(BF16), (((1,), (1,)), ((), ())),
                           preferred_element_type=F32)


def _sigmoid(x):
    return 1.0 / (1.0 + jnp.exp(-x))


def _rmsnorm_body(x_ref, g_ref, o_ref):
    x = x_ref[...]
    inv = lax.rsqrt(jnp.mean(x * x, axis=-1, keepdims=True) + NORM_EPS)
    o_ref[...] = (x * inv * g_ref[...]).astype(o_ref.dtype)


def _rmsnorm(x, g, out_dtype, tm=512):
    m, d = x.shape
    return pl.pallas_call(
        _rmsnorm_body,
        grid=(m // tm,),
        in_specs=[pl.BlockSpec((tm, d), lambda i: (i, 0)),
                  pl.BlockSpec((1, d), lambda i: (0, 0))],
        out_specs=pl.BlockSpec((tm, d), lambda i: (i, 0)),
        out_shape=jax.ShapeDtypeStruct((m, d), out_dtype),
        compiler_params=_cparams(1),
        name="rmsnorm",
    )(x, g.reshape(1, d))


def _cast_weight_tiles(w_refs, wb_refs):
    @pl.when(pl.program_id(1) == 0)
    def _():
        for w_ref, wb_ref in zip(w_refs, wb_refs):
            wb_ref[...] = w_ref[...].astype(BF16)


def _mm_plain_body(a_ref, w_ref, o_ref, wb_ref):
    _cast_weight_tiles([w_ref], [wb_ref])
    o_ref[...] = jnp.dot(a_ref[...], wb_ref[...], preferred_element_type=F32).astype(o_ref.dtype)


def _mm_swiglu_body(a_ref, wg_ref, wu_ref, o_ref, wgb_ref, wub_ref):
    _cast_weight_tiles([wg_ref, wu_ref], [wgb_ref, wub_ref])
    a = a_ref[...]
    g = jnp.dot(a, wgb_ref[...], preferred_element_type=F32)
    u = jnp.dot(a, wub_ref[...], preferred_element_type=F32)
    o_ref[...] = (g * _sigmoid(g) * u).astype(o_ref.dtype)


def _mm_residual_body(a_ref, w_ref, res_ref, o_ref, wb_ref):
    _cast_weight_tiles([w_ref], [wb_ref])
    o_ref[...] = res_ref[...] + jnp.dot(a_ref[...], wb_ref[...], preferred_element_type=F32)


def _layer_weight_spec(w, layer, k, tn):
    if w.ndim == 2:
        return pl.BlockSpec((k, tn), lambda j, i: (0, j))
    return pl.BlockSpec((None, k, tn), lambda j, i: (layer, 0, j))


def _matmul(a, ws, *, tm, tn, out_dtype, residual=None, layer=0):
    m, k = a.shape
    n = ws[0].shape[-1]
    a_spec = pl.BlockSpec((tm, k), lambda j, i: (i, 0))
    w_spec = _layer_weight_spec(ws[0], layer, k, tn)
    o_spec = pl.BlockSpec((tm, tn), lambda j, i: (i, j))
    if len(ws) == 2:
        body, in_specs, args = _mm_swiglu_body, [a_spec, w_spec, w_spec], (a, ws[0], ws[1])
    elif residual is not None:
        body, in_specs, args = _mm_residual_body, [a_spec, w_spec, o_spec], (a, ws[0], residual)
    else:
        body, in_specs, args = _mm_plain_body, [a_spec, w_spec], (a, ws[0])
    return pl.pallas_call(
        body,
        grid=(n // tn, m // tm),
        in_specs=in_specs,
        out_specs=o_spec,
        out_shape=jax.ShapeDtypeStruct((m, n), out_dtype),
        scratch_shapes=[pltpu.VMEM((k, tn), BF16)] * len(ws),
        compiler_params=_cparams(2),
        name="matmul",
    )(*args)


def _norm_mm_body(*refs, n_w):
    x_ref, g_ref = refs[:2]
    w_refs = refs[2:2 + n_w]
    o_ref, hb_ref = refs[2 + n_w:]

    @pl.when(pl.program_id(1) == 0)
    def _():
        x = x_ref[...]
        inv = lax.rsqrt(jnp.mean(x * x, axis=-1, keepdims=True) + NORM_EPS)
        hb_ref[...] = (x * inv * g_ref[...]).astype(BF16)

    h = hb_ref[...]
    if n_w == 2:
        g = jnp.dot(h, w_refs[0][...].astype(BF16), preferred_element_type=F32)
        u = jnp.dot(h, w_refs[1][...].astype(BF16), preferred_element_type=F32)
        o_ref[...] = (g * _sigmoid(g) * u).astype(o_ref.dtype)
    else:
        o_ref[...] = jnp.dot(h, w_refs[0][...].astype(BF16), preferred_element_type=F32).astype(o_ref.dtype)


def _norm_matmul(x, gain, ws, *, tm, tn, out_dtype, layer=0):
    m, k = x.shape
    n = ws[0].shape[-1]
    if ws[0].ndim == 2:
        w_spec = pl.BlockSpec((k, tn), lambda i, j: (0, j))
    else:
        w_spec = pl.BlockSpec((None, k, tn), lambda i, j: (layer, 0, j))
    return pl.pallas_call(
        functools.partial(_norm_mm_body, n_w=len(ws)),
        grid=(m // tm, n // tn),
        in_specs=[pl.BlockSpec((tm, k), lambda i, j: (i, 0)), pl.BlockSpec((1, k), lambda i, j: (0, 0))]
        + [w_spec] * len(ws),
        out_specs=pl.BlockSpec((tm, tn), lambda i, j: (i, j)),
        out_shape=jax.ShapeDtypeStruct((m, n), out_dtype),
        scratch_shapes=[pltpu.VMEM((tm, k), BF16)],
        compiler_params=_cparams(2),
        name="norm_matmul",
    )(x, gain.reshape(1, k), *ws)


IN_PROJ_ROW_UNIT = 32


def _in_proj_body(*refs, has_extra, n_tiles, keep_cols):
    if has_extra:
        src_ref, x_ref, g_ref, wt_ref, extra_ref, o_ref, hb_ref = refs
    else:
        src_ref, x_ref, g_ref, wt_ref, o_ref, hb_ref = refs
    j = pl.program_id(1)

    @pl.when(j == 0)
    def _():
        x = x_ref[...]
        inv = lax.rsqrt(jnp.mean(x * x, axis=-1, keepdims=True) + NORM_EPS)
        hb_ref[...] = (x * inv * g_ref[...]).astype(BF16)

    h = hb_ref[...]
    out = lax.dot_general(h, wt_ref[0].astype(BF16), (((1,), (1,)), ((), ())), preferred_element_type=F32)
    if has_extra:
        @pl.when(j < n_tiles - 1)
        def _():
            o_ref[...] = out

        @pl.when(j == n_tiles - 1)
        def _():
            col = lax.broadcasted_iota(jnp.int32, out.shape, 1)
            o_ref[...] = jnp.where(col < keep_cols, out, 0.0) + jnp.dot(h, extra_ref[...],
                                                                        preferred_element_type=F32)
    else:
        o_ref[...] = out


def _in_proj(x, gain, wt, layer, src_rows, extra, keep_cols, tm=1024, tn=512):
    m, k = x.shape
    n_tiles = len(src_rows)
    assert all(r % IN_PROJ_ROW_UNIT == 0 for r in src_rows)
    src = jnp.asarray([r // IN_PROJ_ROW_UNIT for r in src_rows], jnp.int32)
    element = lambda n: pl.Element(n)
    in_specs = [pl.BlockSpec((tm, k), lambda i, j, src: (i, 0)),
                pl.BlockSpec((1, k), lambda i, j, src: (0, 0)),
                pl.BlockSpec((element(1), element(tn), element(k)),
                             lambda i, j, src: (layer, src[j] * IN_PROJ_ROW_UNIT, 0))]
    args = [src, x, gain.reshape(1, k), wt]
    if extra is not None:
        in_specs.append(pl.BlockSpec((k, tn), lambda i, j, src: (0, 0)))
        args.append(extra)
    return pl.pallas_call(
        functools.partial(_in_proj_body, has_extra=extra is not None, n_tiles=n_tiles, keep_cols=keep_cols),
        grid_spec=pltpu.PrefetchScalarGridSpec(
            num_scalar_prefetch=1,
            grid=(m // tm, n_tiles),
            in_specs=in_specs,
            out_specs=pl.BlockSpec((tm, tn), lambda i, j, src: (i, j)),
            scratch_shapes=[pltpu.VMEM((tm, k), BF16)]),
        out_shape=jax.ShapeDtypeStruct((m, n_tiles * tn), F32),
        compiler_params=_cparams(2),
        name="in_proj",
    )(*args)


def _head_sum(x, sel, sel_t):
    s = _dot_lhs_split(x, sel)
    return _dot_lhs_split(s, sel_t)


def _token_shift(p, carry_ref, mu):
    rows = p.shape[0]
    row = lax.broadcasted_iota(jnp.int32, p.shape, 0)
    prev = jnp.where(row == 0, carry_ref[...], pltpu.roll(p, 1, axis=0))
    carry_ref[...] = p[rows - 1:rows, :]
    return p + (prev - p) * mu


def _prep_body(*refs, tiles_per_seq, has_vres, emit_v, width):
    (rkv_ref, lora_ref, mu_rkv_ref, mu_lora_ref, w0_ref, a0_ref, kk_ref, ka_ref, rk_ref,
     dup_ref, iup_ref, gup_ref, sel_ref, selt_ref, tri_ref, ones_ref) = refs[:16]
    pos = 16
    if has_vres:
        vfirst_ref, v0_ref, vup_ref = refs[pos:pos + 3]
        pos += 3
    at_o, rt_o, bt_o, kt_o, bp_o, kp_o, vb_o, wc_o, bonus_o, g_o = refs[pos:pos + 10]
    pos += 10
    if emit_v:
        v_o = refs[pos]
        pos += 1
    carry_rkv, carry_lora = refs[pos:]
    R = width

    @pl.when(pl.program_id(0) % tiles_per_seq == 0)
    def _():
        carry_rkv[...] = jnp.zeros_like(carry_rkv)
        carry_lora[...] = jnp.zeros_like(carry_lora)

    z = _token_shift(rkv_ref[...], carry_rkv, mu_rkv_ref[...])
    zl = _token_shift(lora_ref[...], carry_lora, mu_lora_ref[...])
    r, k, v = z[:, :R], z[:, R:2 * R], z[:, 2 * R:]

    decay_logit = w0_ref[...] + _dot_x3(jnp.tanh(zl[:, :128]), dup_ref[...])
    neg = -decay_logit
    softplus = jnp.maximum(neg, 0.0) + jnp.log(1.0 + jnp.exp(-jnp.abs(neg)))
    lw = -jnp.exp(-softplus - 0.5)
    a = _sigmoid(a0_ref[...] + _dot_x3(zl[:, :256], iup_ref[...]))
    g_o[...] = _dot(_sigmoid(zl[:, 128:]), gup_ref[...])
    if has_vres:
        mix = _sigmoid(v0_ref[...] + _dot(zl[:, 384:], vup_ref[...]))
        v = v + (vfirst_ref[...] - v) * mix
    if emit_v:
        v_o[...] = v

    sel, selt = sel_ref[...], selt_ref[...]
    kk = k * kk_ref[...]
    norm = jnp.sqrt(_head_sum(kk * kk, sel, selt))
    kk = kk / jnp.maximum(norm, 1e-12)
    k = k * (1.0 + (a - 1.0) * ka_ref[...])
    bonus_o[...] = _head_sum(r * k * rk_ref[...], sel, selt) * v

    d = functools.partial(jnp.dot, preferred_element_type=F32)
    lw_hi = lw.astype(BF16)
    rest = lw - lw_hi.astype(F32)
    lw_mid = rest.astype(BF16)
    lw_lo = (rest - lw_mid.astype(F32)).astype(BF16)
    tri, ones = tri_ref[...], ones_ref[...]
    cum = d(tri, lw_hi) + d(tri, lw_mid) + d(tri, lw_lo)
    total = d(ones, lw_hi) + d(ones, lw_mid) + d(ones, lw_lo)
    w_inv = jnp.exp(-cum)
    w_rem = jnp.exp(total - cum)
    b = kk * a
    wc_o[...] = jnp.exp(total)
    pairs = R // LANES
    for o_ref, val in ((at_o, -kk * jnp.exp(cum - lw)), (rt_o, r * jnp.exp(cum)), (bt_o, b * w_inv),
                       (kt_o, k * w_inv), (bp_o, b * w_rem), (kp_o, k * w_rem), (vb_o, v)):
        val = val.astype(BF16)
        for p in range(pairs):
            o_ref[0, p] = val[:, p * LANES:(p + 1) * LANES]


def _rwkv_prep(p_all, bsz, seq, width, consts, vres, emit_v):
    tm = PREP_ROWS
    m = p_all.shape[0]
    R = width
    lora_w = 512
    lora_blk = (p_all.shape[1] - lora_w) // lora_w
    tiles_per_seq = seq // tm
    row = lambda i: (i, 0)
    fixed = lambda i: (0, 0)
    in_specs = [pl.BlockSpec((tm, 3 * R), row),
                pl.BlockSpec((tm, lora_w), lambda i: (i, lora_blk))]
    args = [p_all, p_all]
    for c in consts:
        in_specs.append(pl.BlockSpec(c.shape, fixed))
        args.append(c)
    if vres is not None:
        vfirst, v0, vup = vres
        in_specs += [pl.BlockSpec((tm, R), row), pl.BlockSpec(v0.shape, fixed), pl.BlockSpec(vup.shape, fixed)]
        args += [vfirst, v0, vup]
    pairs = R // LANES
    pm_spec = pl.BlockSpec((1, pairs, tm, LANES), lambda i: (i // tiles_per_seq, 0, i % tiles_per_seq, 0))
    pm_shape = jax.ShapeDtypeStruct((bsz, pairs, seq, LANES), BF16)
    nat_spec = pl.BlockSpec((tm, R), row)
    nat_shape = jax.ShapeDtypeStruct((m, R), F32)
    n_nat = 4 if emit_v else 3
    return pl.pallas_call(
        functools.partial(_prep_body, tiles_per_seq=tiles_per_seq, has_vres=vres is not None, emit_v=emit_v,
                          width=R),
        grid=(m // tm,),
        in_specs=in_specs,
        out_specs=[pm_spec] * 7 + [nat_spec] * n_nat,
        out_shape=[pm_shape] * 7 + [nat_shape] * n_nat,
        scratch_shapes=[pltpu.VMEM((1, 3 * R), F32), pltpu.VMEM((1, lora_w), F32)],
        compiler_params=_cparams(1),
        name="rwkv_prep",
    )(*args)


def _wkv_body(at_ref, rt_ref, bt_ref, kt_ref, bp_ref, kp_ref, v_ref, wc_ref, y_ref, state_ref, *, pairs, chunks):
    C = WKV_CHUNK
    hs = HEAD_SIZE

    @pl.when(pl.program_id(1) == 0)
    def _():
        state_ref[...] = jnp.zeros_like(state_ref)

    def head_masks(rows, width):
        lane = lax.broadcasted_iota(jnp.int32, (rows, width), 1)
        first = (lane % LANES) < hs
        return first, jnp.logical_not(first)

    def stack_heads(x):
        m0, m1 = head_masks(*x.shape)
        zero = jnp.zeros_like(x)
        return jnp.concatenate([jnp.where(m0, x, zero), jnp.where(m1, x, zero)], axis=0)

    row = lax.broadcasted_iota(jnp.int32, (C, LANES), 0)
    col = lax.broadcasted_iota(jnp.int32, (C, LANES), 1) % hs
    strict, incl = row > col, row >= col
    sq_r = lax.broadcasted_iota(jnp.int32, (LANES, LANES), 0)
    sq_c = lax.broadcasted_iota(jnp.int32, (LANES, LANES), 1)
    same_head = (sq_r // hs) == (sq_c // hs)
    eye = sq_r == sq_c
    levels = C.bit_length() - 1
    dot = functools.partial(jnp.dot, preferred_element_type=F32)
    P = range(pairs)

    def step(c, carry):
        sl = pl.ds(pl.multiple_of(c * C, C), C)
        at = [at_ref[0, p, sl, :] for p in P]
        rt = [rt_ref[0, p, sl, :] for p in P]
        v = [v_ref[0, p, sl, :] for p in P]
        vs = [stack_heads(v[p]) for p in P]
        lhs = [jnp.concatenate([at[p], rt[p]], axis=0) for p in P]
        rhs = [jnp.concatenate([stack_heads(bt_ref[0, p, sl, :]), stack_heads(kt_ref[0, p, sl, :])], axis=0)
               for p in P]
        abk = [_dot_nt(lhs[p], rhs[p]) for p in P]
        n = [jnp.where(strict, abk[p][:C, :LANES], 0.0) for p in P]
        a_rb = [jnp.where(incl, abk[p][C:, :LANES], 0.0) for p in P]
        a_ak = [jnp.where(strict, abk[p][:C, LANES:], 0.0) for p in P]
        a_rk = [jnp.where(incl, abk[p][C:, LANES:], 0.0) for p in P]
        av = [dot(a_ak[p].astype(BF16), vs[p]) for p in P]
        x = [jnp.concatenate([at[p].astype(F32), av[p]], axis=1) for p in P]
        for lvl in range(levels):
            last = lvl + 1 == levels
            for p in P:
                z = x[p] if last else jnp.concatenate([n[p], x[p]], axis=1)
                out = dot(n[p].astype(BF16), stack_heads(z.astype(BF16)))
                if last:
                    x[p] = x[p] + out
                else:
                    n[p] = out[:, :LANES]
                    x[p] = x[p] + out[:, LANES:]
        xb = [x[p].astype(BF16) for p in P]
        zeros_c = jnp.zeros((C, LANES), BF16)
        bx = [_dot_tn(jnp.concatenate([bp_ref[0, p, sl, :], kp_ref[0, p, sl, :]], axis=0),
                      jnp.concatenate([xb[p], jnp.concatenate([zeros_c, v[p]], axis=1)], axis=0)) for p in P]
        zeros_2c = jnp.zeros((2 * C, LANES), BF16)
        rx = [dot(jnp.concatenate([a_rb[p], a_rk[p]], axis=1).astype(BF16),
                  jnp.concatenate([stack_heads(xb[p]), jnp.concatenate([zeros_2c, vs[p]], axis=1)], axis=0))
              for p in P]
        for p in P:
            state = state_ref[p]
            wc = wc_ref[pl.ds(pl.multiple_of(c * C, C), 1), p * LANES:(p + 1) * LANES]
            trans = jnp.where(same_head, bx[p][:, :LANES], 0.0) + jnp.where(eye, wc, 0.0)
            inject = jnp.where(same_head, bx[p][:, LANES:], 0.0)
            r_eff = rt[p].astype(F32) + rx[p][:, :LANES]
            y_ref[0, p, sl, :] = _dot(r_eff, state) + rx[p][:, LANES:]
            state_ref[p] = _dot_x3(trans, state) + inject
        return carry

    lax.fori_loop(0, chunks, step, 0)


def _wkv(at, rt, bt, kt, bp, kp, v, wc, pairs=8, span=512):
    bsz, n_pairs, seq, _ = at.shape
    groups = n_pairs // pairs
    spec = pl.BlockSpec((1, pairs, span, LANES), lambda g, t: (g // groups, g % groups, t, 0))
    wc_spec = pl.BlockSpec((span, pairs * LANES), lambda g, t: ((g // groups) * (seq // span) + t, g % groups))
    return pl.pallas_call(
        functools.partial(_wkv_body, pairs=pairs, chunks=span // WKV_CHUNK),
        grid=(bsz * groups, seq // span),
        in_specs=[spec] * 7 + [wc_spec],
        out_specs=spec,
        out_shape=jax.ShapeDtypeStruct((bsz, n_pairs, seq, LANES), F32),
        scratch_shapes=[pltpu.VMEM((pairs, LANES, LANES), F32)],
        compiler_params=_cparams(2),
        name="wkv7",
    )(at, rt, bt, kt, bp, kp, v, wc)


def _post_pool_body(y_ref, bonus_ref, g_ref, u_ref, lnw_ref, lnb_ref, sel_ref, selt_ref,
                    pw_ref, ps_ref, ya_o, yb_o, ext_ref, *, tiles_per_seq, tm, group_width):
    sel, selt = sel_ref[...], selt_ref[...]
    inv_n = 1.0 / HEAD_SIZE
    y = jnp.concatenate([y_ref[0, p] for p in range(y_ref.shape[1])], axis=1)
    mean = _head_sum(y, sel, selt) * inv_n
    yc = y - mean
    var = _head_sum(yc * yc, sel, selt) * inv_n
    yn = yc * lax.rsqrt(var + GN_EPS) * lnw_ref[...] + lnb_ref[...]
    ya_o[...] = ((yn + bonus_ref[...]) * g_ref[...]).astype(ya_o.dtype)

    seq_tile = pl.program_id(0) % tiles_per_seq

    @pl.when(seq_tile == 0)
    def _():
        ext_ref[0:POOL_HALO, :] = jnp.zeros((POOL_HALO, ext_ref.shape[1]), F32)

    u = u_ref[...]
    ext_ref[POOL_HALO:POOL_HALO + tm, :] = u
    t_pos = seq_tile * tm + lax.broadcasted_iota(jnp.int32, (tm, group_width), 0)
    for gi, w in enumerate(POOL_WINDOWS):
        lanes = slice(gi * group_width, (gi + 1) * group_width)
        acc = u[:, lanes]
        for j in range(1, w):
            acc = acc + ext_ref[POOL_HALO - j:POOL_HALO - j + tm, lanes]
        count = jnp.minimum(t_pos + 1, w).astype(F32)
        mixed = acc / count - u[:, lanes]
        yg = _dot_x3(mixed, pw_ref[gi])
        yb_o[:, lanes] = (yg * ps_ref[:, lanes]).astype(yb_o.dtype)
    ext_ref[0:POOL_HALO, :] = u[tm - POOL_HALO:, :]


def _post_pool(y, bonus, g, p_all, pool_blk, consts, seq, tm=256):
    m, R = bonus.shape
    lnw, lnb, sel, selt, pw, ps = consts
    tiles_per_seq = seq // tm
    row = lambda i: (i, 0)
    fixed2 = lambda i: (0, 0)
    in_specs = [pl.BlockSpec((1, y.shape[1], tm, LANES), lambda i: (i // tiles_per_seq, 0, i % tiles_per_seq, 0)),
                pl.BlockSpec((tm, R), row), pl.BlockSpec((tm, R), row),
                pl.BlockSpec((tm, R), lambda i: (i, pool_blk))]
    in_specs += [pl.BlockSpec(c.shape, fixed2) for c in (lnw, lnb, sel, selt)]
    in_specs += [pl.BlockSpec(pw.shape, lambda i: (0, 0, 0)), pl.BlockSpec(ps.shape, fixed2)]
    out = jax.ShapeDtypeStruct((m, R), BF16)
    return pl.pallas_call(
        functools.partial(_post_pool_body, tiles_per_seq=tiles_per_seq, tm=tm, group_width=R // len(POOL_WINDOWS)),
        grid=(m // tm,),
        in_specs=in_specs,
        out_specs=[pl.BlockSpec((tm, R), row)] * 2,
        out_shape=[out, out],
        scratch_shapes=[pltpu.VMEM((POOL_HALO + tm, R), F32)],
        compiler_params=_cparams(1),
        name="post_pool",
    )(y, bonus, g, p_all, lnw, lnb, sel, selt, pw, ps)


def _merge_body(ya_ref, yb_ref, pa_ref, pb_ref, ga_ref, gb_ref, o_ref, pab_ref, pbb_ref):
    _cast_weight_tiles([pa_ref, pb_ref], [pab_ref, pbb_ref])
    ma = jnp.dot(ya_ref[...], pab_ref[...], preferred_element_type=F32)
    mb = jnp.dot(yb_ref[...], pbb_ref[...], preferred_element_type=F32)
    o_ref[...] = (_sigmoid(ga_ref[...]) * ma + _sigmoid(gb_ref[...]) * mb).astype(o_ref.dtype)


def _merge(ya, yb, pa, pb, layer, p_all, ga_col, gb_col, tm=512, tn=512):
    m, R = ya.shape
    d = pa.shape[-1]
    ga_blk, gb_blk = ga_col // tn, gb_col // tn
    return pl.pallas_call(
        _merge_body,
        grid=(d // tn, m // tm),
        in_specs=[pl.BlockSpec((tm, R), lambda j, i: (i, 0)),
                  pl.BlockSpec((tm, R), lambda j, i: (i, 0)),
                  _layer_weight_spec(pa, layer, R, tn),
                  _layer_weight_spec(pb, layer, R, tn),
                  pl.BlockSpec((tm, tn), lambda j, i: (i, ga_blk + j)),
                  pl.BlockSpec((tm, tn), lambda j, i: (i, gb_blk + j))],
        out_specs=pl.BlockSpec((tm, tn), lambda j, i: (i, j)),
        out_shape=jax.ShapeDtypeStruct((m, d), BF16),
        scratch_shapes=[pltpu.VMEM((R, tn), BF16)] * 2,
        compiler_params=_cparams(2),
        name="merge",
    )(ya, yb, pa, pb, p_all, p_all)


def _router_body(x_ref, g_ref, wr_ref, tri_ref, h_o, idx_o, gate_o, cnt_o, carry_ref):
    @pl.when(pl.program_id(0) == 0)
    def _():
        carry_ref[...] = jnp.zeros_like(carry_ref)

    x = x_ref[...]
    inv = lax.rsqrt(jnp.mean(x * x, axis=-1, keepdims=True) + NORM_EPS)
    h = x * inv * g_ref[...]
    h_o[...] = h.reshape(h_o.shape)
    logits = _dot_x3(h, wr_ref[...])
    lane = lax.broadcasted_iota(jnp.int32, logits.shape, 1)
    neg_inf = jnp.float32(-jnp.inf)
    l1 = jnp.where(lane < N_EXPERTS, logits, neg_inf)
    m1 = jnp.max(l1, axis=-1, keepdims=True)
    i1 = jnp.min(jnp.where(l1 == m1, lane, LANES), axis=-1, keepdims=True)
    l2 = jnp.where(lane == i1, neg_inf, l1)
    m2 = jnp.max(l2, axis=-1, keepdims=True)
    i2 = jnp.min(jnp.where(l2 == m2, lane, LANES), axis=-1, keepdims=True)
    e = jnp.exp(m2 - m1)
    g1 = 1.0 / (1.0 + e)
    gate_o[...] = jnp.where(lane == 0, g1, jnp.where(lane == 1, e * g1, 0.0))

    first, second = lane == i1, lane == i2
    picked = jnp.where(jnp.logical_or(first, second), 1.0, 0.0)
    before = carry_ref[...] + jnp.dot(tri_ref[...], picked.astype(BF16), preferred_element_type=F32)
    rank1 = jnp.sum(jnp.where(first, before, 0.0), axis=-1, keepdims=True).astype(jnp.int32)
    rank2 = jnp.sum(jnp.where(second, before, 0.0), axis=-1, keepdims=True).astype(jnp.int32)
    idx_o[...] = jnp.where(lane == 0, i1, jnp.where(lane == 1, i2, jnp.where(lane == 2, rank1,
                                                                              jnp.where(lane == 3, rank2, 0))))
    carry_ref[...] += jnp.sum(picked, axis=0, keepdims=True)
    cnt_o[...] = jnp.broadcast_to(carry_ref[...], cnt_o.shape).astype(jnp.int32)


def _router(x, g, wr_pad, tm=256):
    m, d = x.shape
    row = lambda i: (i, 0)
    t_idx = jnp.arange(tm, dtype=jnp.int32)
    tri_strict = (t_idx[None, :] < t_idx[:, None]).astype(BF16)
    return pl.pallas_call(
        _router_body,
        grid=(m // tm,),
        in_specs=[pl.BlockSpec((tm, d), row), pl.BlockSpec((1, d), lambda i: (0, 0)),
                  pl.BlockSpec(wr_pad.shape, lambda i: (0, 0)), pl.BlockSpec((tm, tm), lambda i: (0, 0))],
        out_specs=[pl.BlockSpec((tm, d // LANES, LANES), lambda i: (i, 0, 0)),
                   pl.BlockSpec((tm, LANES), row), pl.BlockSpec((tm, LANES), row),
                   pl.BlockSpec((8, LANES), lambda i: (0, 0))],
        out_shape=[jax.ShapeDtypeStruct((m, d // LANES, LANES), F32), jax.ShapeDtypeStruct((m, LANES), jnp.int32),
                   jax.ShapeDtypeStruct((m, LANES), F32), jax.ShapeDtypeStruct((8, LANES), jnp.int32)],
        scratch_shapes=[pltpu.VMEM((1, LANES), F32)],
        compiler_params=_cparams(1),
        name="router",
    )(x, g.reshape(1, d), wr_pad, tri_strict)


def _gather_body(tok_ref, used_ref, h_hbm, o_ref, buf_ref, sems, *, tg, n_steps):
    i = pl.program_id(0)

    def row_copy(step, r):
        slot = step % 2
        return pltpu.make_async_copy(h_hbm.at[pl.ds(tok_ref[step * tg + r], 1)], buf_ref.at[slot, pl.ds(r, 1)],
                                     sems.at[slot])

    def fetch(step):
        @pl.when(jnp.logical_and(step < n_steps, step * tg < used_ref[0]))
        def _():
            def start(q, c):
                for u in range(DMA_UNROLL):
                    row_copy(step, q * DMA_UNROLL + u).start()
                return c
            lax.fori_loop(0, tg // DMA_UNROLL, start, 0)

    @pl.when(i == 0)
    def _():
        fetch(i)

    fetch(i + 1)

    @pl.when(i * tg < used_ref[0])
    def _():
        pltpu.make_async_copy(h_hbm.at[pl.ds(0, tg)], buf_ref.at[i % 2], sems.at[i % 2]).wait()
        o_ref[...] = buf_ref[i % 2].reshape(o_ref.shape).astype(o_ref.dtype)

    @pl.when(i * tg >= used_ref[0])
    def _():
        o_ref[...] = jnp.zeros_like(o_ref)


def _gather_rows(tok, used_rows, h3, tg=256):
    p = tok.shape[0]
    _, nc, lanes = h3.shape
    n_steps = p // tg
    return pl.pallas_call(
        functools.partial(_gather_body, tg=tg, n_steps=n_steps),
        grid_spec=pltpu.PrefetchScalarGridSpec(
            num_scalar_prefetch=2,
            grid=(n_steps,),
            in_specs=[pl.BlockSpec(memory_space=pl.ANY)],
            out_specs=pl.BlockSpec((tg, nc * lanes), lambda i, tok, used: (i, 0)),
            scratch_shapes=[pltpu.VMEM((2, tg, nc, lanes), F32), pltpu.SemaphoreType.DMA((2,))]),
        out_shape=jax.ShapeDtypeStruct((p, nc * lanes), BF16),
        compiler_params=_cparams(1),
        name="moe_gather",
    )(tok, used_rows, h3)


DMA_UNROLL = 8
MOE_ROWS = 1024
MOE_SUB = 256
MOE_FF_TILE = 256
PLAN_FIELDS = 7


def _experts_body(plan_ref, x_ref, wg_ref, wu_ref, wd_ref, o_ref, wgb_ref, wub_ref, wdb_ref, *, n_items):
    w = pl.program_id(0)
    j = pl.program_id(1)
    field = lambda f: plan_ref[f * n_items + w]
    lo, hi, zero, live = field(3), field(4), field(5), field(6)
    n_sub = MOE_ROWS // MOE_SUB

    @pl.when(jnp.logical_and(zero == 1, j == 0))
    def _():
        o_ref[...] = jnp.zeros_like(o_ref)

    def rows(sl, wg, wu, wd):
        x = x_ref[sl, :]
        g = jnp.dot(x, wg, preferred_element_type=F32)
        u = jnp.dot(x, wu, preferred_element_type=F32)
        h = (g * _sigmoid(g) * u).astype(BF16)
        o_ref[sl, :] += jnp.dot(h, wd, preferred_element_type=F32)

    whole = jnp.logical_and(live == 1, jnp.logical_and(lo == 0, hi == n_sub))
    partial = jnp.logical_and(live == 1, jnp.logical_not(whole))

    @pl.when(whole)
    def _():
        rows(slice(None), wg_ref[...].astype(BF16), wu_ref[...].astype(BF16), wd_ref[...].astype(BF16))

    @pl.when(partial)
    def _():
        wgb_ref[...] = wg_ref[...].astype(BF16)
        wub_ref[...] = wu_ref[...].astype(BF16)
        wdb_ref[...] = wd_ref[...].astype(BF16)

    for s in range(n_sub):
        @pl.when(jnp.logical_and(partial, jnp.logical_and(lo <= s, s < hi)))
        def _():
            rows(slice(s * MOE_SUB, (s + 1) * MOE_SUB), wgb_ref[...], wub_ref[...], wdb_ref[...])


def _experts(plan, xs, w_gate, w_up, w_down):
    p, d = xs.shape
    f = w_gate.shape[2]
    tn = MOE_FF_TILE
    nj = f // tn
    n_items = plan.shape[0] // PLAN_FIELDS
    fld = lambda plan, f_idx, w: plan[f_idx * n_items + w]
    col = lambda plan, w, j: jnp.where(fld(plan, 6, w) == 1, j, nj - 1)
    return pl.pallas_call(
        functools.partial(_experts_body, n_items=n_items),
        grid_spec=pltpu.PrefetchScalarGridSpec(
            num_scalar_prefetch=1,
            grid=(n_items, nj),
            in_specs=[pl.BlockSpec((MOE_ROWS, d), lambda w, j, plan: (fld(plan, 1, w), 0)),
                      pl.BlockSpec((None, d, tn), lambda w, j, plan: (fld(plan, 2, w), 0, col(plan, w, j))),
                      pl.BlockSpec((None, d, tn), lambda w, j, plan: (fld(plan, 2, w), 0, col(plan, w, j))),
                      pl.BlockSpec((None, tn, d), lambda w, j, plan: (fld(plan, 2, w), col(plan, w, j), 0))],
            out_specs=pl.BlockSpec((MOE_ROWS, d), lambda w, j, plan: (fld(plan, 0, w), 0)),
            scratch_shapes=[pltpu.VMEM((d, tn), BF16), pltpu.VMEM((d, tn), BF16), pltpu.VMEM((tn, d), BF16)]),
        out_shape=jax.ShapeDtypeStruct((p, d), F32),
        compiler_params=_cparams(2),
        name="moe_experts",
    )(plan, xs, w_gate, w_up, w_down)


def _combine_body(d0_ref, d1_ref, x_ref, gate_ref, g_ref, y_hbm, o_ref, buf0, buf1, sem, *, tg, apply_norm):
    base = pl.program_id(0) * tg

    def row_copy(i, d_ref, buf):
        return pltpu.make_async_copy(y_hbm.at[pl.ds(d_ref[base + i], 1), :], buf.at[pl.ds(i, 1), :], sem)

    def start(q, c):
        for u in range(DMA_UNROLL):
            row_copy(q * DMA_UNROLL + u, d0_ref, buf0).start()
            row_copy(q * DMA_UNROLL + u, d1_ref, buf1).start()
        return c

    lax.fori_loop(0, tg // DMA_UNROLL, start, 0)
    for buf in (buf0, buf1):
        pltpu.make_async_copy(y_hbm.at[pl.ds(0, tg), :], buf, sem).wait()
    gate = gate_ref[...]
    x = x_ref[...] + gate[:, 0:1] * buf0[...] + gate[:, 1:2] * buf1[...]
    if apply_norm:
        x = x * lax.rsqrt(jnp.mean(x * x, axis=-1, keepdims=True) + NORM_EPS) * g_ref[...]
    o_ref[...] = x


def _combine(d0, d1, x, gate, g, ybuf, apply_norm, tg=256):
    m, d = x.shape
    row = lambda i, d0, d1: (i, 0)
    return pl.pallas_call(
        functools.partial(_combine_body, tg=tg, apply_norm=apply_norm),
        grid_spec=pltpu.PrefetchScalarGridSpec(
            num_scalar_prefetch=2,
            grid=(m // tg,),
            in_specs=[pl.BlockSpec((tg, d), row), pl.BlockSpec((tg, LANES), row),
                      pl.BlockSpec((1, d), lambda i, d0, d1: (0, 0)),
                      pl.BlockSpec(memory_space=pl.ANY)],
            out_specs=pl.BlockSpec((tg, d), row),
            scratch_shapes=[pltpu.VMEM((tg, d), F32), pltpu.VMEM((tg, d), F32), pltpu.SemaphoreType.DMA(())]),
        out_shape=jax.ShapeDtypeStruct((m, d), F32),
        compiler_params=_cparams(1),
        name="moe_combine",
    )(d0, d1, x, gate, g.reshape(1, d), ybuf)


def _moe_plan(top_e, rank, counts):
    i32 = jnp.int32
    n_tok = top_e.shape[0]
    n_slot = n_tok * TOP_K
    flat_e = top_e.reshape(n_slot)
    rank = rank.reshape(n_slot)
    padded = (counts + MOE_SUB - 1) // MOE_SUB * MOE_SUB
    pend = jnp.cumsum(padded)
    pstart = pend - padded
    dest = pstart[flat_e] + rank
    n_blocks = -(-(n_slot + N_EXPERTS * MOE_SUB) // MOE_ROWS)
    tok = jnp.zeros((n_blocks * MOE_ROWS,), i32).at[dest].set(jnp.arange(n_slot, dtype=i32) // TOP_K)
    used_rows = pend[-1]
    used_blocks = (used_rows + MOE_ROWS - 1) // MOE_ROWS

    n_items = n_blocks + N_EXPERTS
    blk_lo = (jnp.arange(n_blocks, dtype=i32) * MOE_ROWS)[:, None]
    lo_row = jnp.maximum(pstart[None, :], blk_lo)
    hi_row = jnp.minimum(pend[None, :], blk_lo + MOE_ROWS)
    valid = (hi_row > lo_row).reshape(-1)
    n_live = jnp.sum(valid.astype(i32))
    pos = jnp.where(valid, jnp.cumsum(valid.astype(i32)) - 1, n_items)
    compact = lambda v: jnp.zeros((n_items,), i32).at[pos].set(v.reshape(-1).astype(i32), mode="drop")
    blk_c = compact(jnp.broadcast_to(jnp.arange(n_blocks, dtype=i32)[:, None], lo_row.shape))
    exp_c = compact(jnp.broadcast_to(jnp.arange(N_EXPERTS, dtype=i32)[None, :], lo_row.shape))
    lo_c = compact((lo_row - blk_lo) // MOE_SUB)
    hi_c = compact((hi_row - blk_lo) // MOE_SUB)
    first_c = compact(lo_row == blk_lo)

    w = jnp.arange(n_items, dtype=i32)
    live = w < n_live
    tail_blk = used_blocks + (w - n_live)
    last = n_live - 1
    out_blk = jnp.where(live, blk_c, jnp.minimum(tail_blk, n_blocks - 1))
    x_blk = jnp.where(live, blk_c, blk_c[last])
    expert = jnp.where(live, exp_c, exp_c[last])
    zero = jnp.where(live, first_c, (tail_blk < n_blocks).astype(i32))
    plan = jnp.concatenate([out_blk, x_blk, expert, lo_c, hi_c, zero, live.astype(i32)]).astype(i32)
    dest = dest.reshape(n_tok, TOP_K).astype(i32)
    return tok, used_rows.astype(i32).reshape(1), plan, dest[:, 0], dest[:, 1]


def kernel(x, norm_mix, w_in, shift_mu, decay_w0, decay_up, iclr_a0, iclr_up, outgate_up, k_k, k_a, r_k, lnx_w, lnx_b, vres_down, vres_mu, vres_up, vres_v0, pool_w, pool_scale, proj_a, proj_b, w_o, norm_ffn, ffn_gate, ffn_up, ffn_down, router, moe_gate, moe_up, moe_down, norm_final):
    bsz, seq, d = x.shape
    depth = w_in.shape[0]
    R = decay_w0.shape[1]
    dl, il, gl = decay_up.shape[1], iclr_up.shape[1], outgate_up.shape[1]
    vl = vres_up.shape[1]
    assert (dl, il, vl, gl) == (96, 96, 64, 256) and 3 * R + dl + il + gl == shift_mu.shape[1]
    m = bsz * seq
    x = x.reshape(m, d)
    row = lambda t: t.reshape(1, -1)

    head_of_lane = jnp.arange(R, dtype=jnp.int32) // HEAD_SIZE
    sel = (head_of_lane[:, None] == jnp.arange(LANES, dtype=jnp.int32)[None, :]).astype(BF16)
    sel_t = sel.T
    t_idx = jnp.arange(PREP_ROWS, dtype=jnp.int32)
    same_chunk = (t_idx[:, None] // WKV_CHUNK) == (t_idx[None, :] // WKV_CHUNK)
    ones_bd = same_chunk.astype(BF16)
    tri_bd = (same_chunk & (t_idx[None, :] <= t_idx[:, None])).astype(BF16)

    w_in_t = jnp.swapaxes(w_in, 1, 2)
    v_first = None
    for l in range(depth):
        c0 = 3 * R
        c3 = c0 + dl + il + gl
        tn = 512
        lora_w = c3 - c0 + vl
        assert c0 % tn == 0 and (w_in.shape[2] - c3) % tn == 0 and lora_w == tn
        src_rows = list(range(0, c0, tn)) + list(range(c3, w_in.shape[2], tn)) + [c0]
        vd_mu = vres_mu[l - 1] if l > 0 else jnp.zeros((vl,), F32)
        extra = None
        if l > 0:
            extra = jnp.concatenate([jnp.zeros((d, c3 - c0), F32), vres_down[l - 1]], axis=1).astype(BF16)
        mu = shift_mu[l]
        mu_rkv = row(mu[:c0])
        mu_lora = row(jnp.concatenate([mu[c0:c3], vd_mu]))
        pool_col = c0
        ga_col = c0 + R
        gb_col = ga_col + d
        zeros = lambda n: jnp.zeros((n, R), F32)
        dup = jnp.concatenate([decay_up[l], zeros(128 - dl)], axis=0)
        iup = jnp.concatenate([zeros(dl), iclr_up[l], zeros(256 - dl - il)], axis=0)
        gup = jnp.concatenate([zeros(dl + il - 128), outgate_up[l], zeros(vl)], axis=0).astype(BF16)

        p_all = _in_proj(x, norm_mix[l], w_in_t, l, src_rows, extra, c3 - c0, tn=tn)

        consts = [mu_rkv, mu_lora, row(decay_w0[l]), row(iclr_a0[l]), row(k_k[l]), row(k_a[l]), row(r_k[l]),
                  dup, iup, gup, sel, sel_t, tri_bd, ones_bd]
        vres = None
        if l > 0:
            vup = jnp.concatenate([zeros(128 - vl), vres_up[l - 1]], axis=0).astype(BF16)
            vres = (v_first, row(vres_v0[l - 1]), vup)
        emit_v = l == 0 and depth > 1
        outs = _rwkv_prep(p_all, bsz, seq, R, consts, vres, emit_v)
        wc, bonus, g = outs[7:10]
        if emit_v:
            v_first = outs[10]
        y = _wkv(*outs[:7], wc)

        ya, yb = _post_pool(y, bonus, g, p_all, pool_col // R,
                            [row(lnx_w[l]), row(lnx_b[l]), sel, sel_t, pool_w[l], row(pool_scale[l])], seq)
        merged = _merge(ya, yb, proj_a, proj_b, l, p_all, ga_col, gb_col)
        x = _matmul(merged, [w_o], tm=1024, tn=512, out_dtype=F32, residual=x, layer=l)

        i = l // 2
        if l % 2 == 0:
            hidden = _norm_matmul(x, norm_ffn[l], [ffn_gate, ffn_up], tm=1024, tn=512, out_dtype=BF16, layer=i)
            x = _matmul(hidden, [ffn_down], tm=512, tn=512, out_dtype=F32, residual=x, layer=i)
            if l == depth - 1:
                x = _rmsnorm(x, norm_final, F32)
        else:
            wr_pad = jnp.concatenate([router[i], jnp.zeros((d, LANES - N_EXPERTS), F32)], axis=1)
            h3, idx, gate, counts = _router(x, norm_ffn[l], wr_pad)
            tok, used_rows, plan, d0, d1 = _moe_plan(idx[:, :TOP_K], idx[:, TOP_K:2 * TOP_K], counts[0, :N_EXPERTS])
            xs = _gather_rows(tok, used_rows, h3)
            ybuf = _experts(plan, xs, moe_gate[i], moe_up[i], moe_down[i])
            x = _combine(d0, d1, x, gate, norm_final, ybuf, apply_norm=l == depth - 1)
    return x.reshape(bsz, seq, d)
```

```python
import functools

import jax
import jax.numpy as jnp
from jax import lax
from jax.experimental import pallas as pl
from jax.experimental.pallas import tpu as pltpu

F32 = jnp.float32
BF16 = jnp.bfloat16

HEAD_SIZE = 64
POOL_WINDOWS = (2, 4, 8, 16)
N_EXPERTS = 8
TOP_K = 2
NORM_EPS = 1e-6
GN_EPS = 64e-5
LANES = 128
VMEM_LIMIT_BYTES = 56 * 1024 * 1024
WKV_CHUNK = 64
POOL_HALO = 16
PREP_ROWS = 256


def _cparams(n_axes):
    return pltpu.CompilerParams(dimension_semantics=("arbitrary",) * n_axes,
                                vmem_limit_bytes=VMEM_LIMIT_BYTES)


def _dot(a, b):
    return jnp.dot(a.astype(BF16), b.astype(BF16), preferred_element_type=F32)


def _split(x):
    hi = x.astype(BF16)
    lo = (x - hi.astype(F32)).astype(BF16)
    return hi, lo


def _dot_x3(a, b):
    a_hi, a_lo = _split(a)
    b_hi, b_lo = _split(b)
    d = functools.partial(jnp.dot, preferred_element_type=F32)
    return d(a_hi, b_hi) + d(a_hi, b_lo) + d(a_lo, b_hi)


def _dot_lhs_split(a, b_bf16):
    a_hi, a_lo = _split(a)
    d = functools.partial(jnp.dot, preferred_element_type=F32)
    return d(a_hi, b_bf16) + d(a_lo, b_bf16)


def _dot_tn(a, b):
    return lax.dot_general(a.astype(BF16), b.astype(BF16), (((0,), (0,)), ((), ())),
                           preferred_element_type=F32)


def _dot_nt(a, b):
    return lax.dot_general(a.astype(BF16), b.astype(BF16), (((1,), (1,)), ((), ())),
                           preferred_element_type=F32)


def _sigmoid(x):
    return 1.0 / (1.0 + jnp.exp(-x))


def _rmsnorm_body(x_ref, g_ref, o_ref):
    x = x_ref[...]
    inv = lax.rsqrt(jnp.mean(x * x, axis=-1, keepdims=True) + NORM_EPS)
    o_ref[...] = (x * inv * g_ref[...]).astype(o_ref.dtype)


def _rmsnorm(x, g, out_dtype, tm=512):
    m, d = x.shape
    return pl.pallas_call(
        _rmsnorm_body,
        grid=(m // tm,),
        in_specs=[pl.BlockSpec((tm, d), lambda i: (i, 0)),
                  pl.BlockSpec((1, d), lambda i: (0, 0))],
        out_specs=pl.BlockSpec((tm, d), lambda i: (i, 0)),
        out_shape=jax.ShapeDtypeStruct((m, d), out_dtype),
        compiler_params=_cparams(1),
        name="rmsnorm",
    )(x, g.reshape(1, d))


def _cast_weight_tiles(w_refs, wb_refs):
    @pl.when(pl.program_id(1) == 0)
    def _():
        for w_ref, wb_ref in zip(w_refs, wb_refs):
            wb_ref[...] = w_ref[...].astype(BF16)


def _mm_plain_body(a_ref, w_ref, o_ref, wb_ref):
    _cast_weight_tiles([w_ref], [wb_ref])
    o_ref[...] = jnp.dot(a_ref[...], wb_ref[...], preferred_element_type=F32).astype(o_ref.dtype)


def _mm_swiglu_body(a_ref, wg_ref, wu_ref, o_ref, wgb_ref, wub_ref):
    _cast_weight_tiles([wg_ref, wu_ref], [wgb_ref, wub_ref])
    a = a_ref[...]
    g = jnp.dot(a, wgb_ref[...], preferred_element_type=F32)
    u = jnp.dot(a, wub_ref[...], preferred_element_type=F32)
    o_ref[...] = (g * _sigmoid(g) * u).astype(o_ref.dtype)


def _mm_residual_body(a_ref, w_ref, res_ref, o_ref, wb_ref):
    _cast_weight_tiles([w_ref], [wb_ref])
    o_ref[...] = res_ref[...] + jnp.dot(a_ref[...], wb_ref[...], preferred_element_type=F32)


def _layer_weight_spec(w, layer, k, tn):
    if w.ndim == 2:
        return pl.BlockSpec((k, tn), lambda j, i: (0, j))
    return pl.BlockSpec((None, k, tn), lambda j, i: (layer, 0, j))


def _matmul(a, ws, *, tm, tn, out_dtype, residual=None, layer=0):
    m, k = a.shape
    n = ws[0].shape[-1]
    a_spec = pl.BlockSpec((tm, k), lambda j, i: (i, 0))
    w_spec = _layer_weight_spec(ws[0], layer, k, tn)
    o_spec = pl.BlockSpec((tm, tn), lambda j, i: (i, j))
    if len(ws) == 2:
        body, in_specs, args = _mm_swiglu_body, [a_spec, w_spec, w_spec], (a, ws[0], ws[1])
    elif residual is not None:
        body, in_specs, args = _mm_residual_body, [a_spec, w_spec, o_spec], (a, ws[0], residual)
    else:
        body, in_specs, args = _mm_plain_body, [a_spec, w_spec], (a, ws[0])
    return pl.pallas_call(
        body,
        grid=(n // tn, m // tm),
        in_specs=in_specs,
        out_specs=o_spec,
        out_shape=jax.ShapeDtypeStruct((m, n), out_dtype),
        scratch_shapes=[pltpu.VMEM((k, tn), BF16)] * len(ws),
        compiler_params=_cparams(2),
        name="matmul",
    )(*args)


def _norm_mm_body(*refs, n_w):
    x_ref, g_ref = refs[:2]
    w_refs = refs[2:2 + n_w]
    o_ref, hb_ref = refs[2 + n_w:]

    @pl.when(pl.program_id(1) == 0)
    def _():
        x = x_ref[...]
        inv = lax.rsqrt(jnp.mean(x * x, axis=-1, keepdims=True) + NORM_EPS)
        hb_ref[...] = (x * inv * g_ref[...]).astype(BF16)

    h = hb_ref[...]
    if n_w == 2:
        g = jnp.dot(h, w_refs[0][...].astype(BF16), preferred_element_type=F32)
        u = jnp.dot(h, w_refs[1][...].astype(BF16), preferred_element_type=F32)
        o_ref[...] = (g * _sigmoid(g) * u).astype(o_ref.dtype)
    else:
        o_ref[...] = jnp.dot(h, w_refs[0][...].astype(BF16), preferred_element_type=F32).astype(o_ref.dtype)


def _norm_matmul(x, gain, ws, *, tm, tn, out_dtype, layer=0):
    m, k = x.shape
    n = ws[0].shape[-1]
    if ws[0].ndim == 2:
        w_spec = pl.BlockSpec((k, tn), lambda i, j: (0, j))
    else:
        w_spec = pl.BlockSpec((None, k, tn), lambda i, j: (layer, 0, j))
    return pl.pallas_call(
        functools.partial(_norm_mm_body, n_w=len(ws)),
        grid=(m // tm, n // tn),
        in_specs=[pl.BlockSpec((tm, k), lambda i, j: (i, 0)), pl.BlockSpec((1, k), lambda i, j: (0, 0))]
        + [w_spec] * len(ws),
        out_specs=pl.BlockSpec((tm, tn), lambda i, j: (i, j)),
        out_shape=jax.ShapeDtypeStruct((m, n), out_dtype),
        scratch_shapes=[pltpu.VMEM((tm, k), BF16)],
        compiler_params=_cparams(2),
        name="norm_matmul",
    )(x, gain.reshape(1, k), *ws)


IN_PROJ_ROW_UNIT = 32


def _in_proj_body(*refs, has_extra, n_tiles, keep_cols):
    if has_extra:
        src_ref, x_ref, g_ref, wt_ref, extra_ref, o_ref, hb_ref = refs
    else:
        src_ref, x_ref, g_ref, wt_ref, o_ref, hb_ref = refs
    j = pl.program_id(1)

    @pl.when(j == 0)
    def _():
        x = x_ref[...]
        inv = lax.rsqrt(jnp.mean(x * x, axis=-1, keepdims=True) + NORM_EPS)
        hb_ref[...] = (x * inv * g_ref[...]).astype(BF16)

    h = hb_ref[...]
    out = lax.dot_general(h, wt_ref[0].astype(BF16), (((1,), (1,)), ((), ())), preferred_element_type=F32)
    if has_extra:
        @pl.when(j < n_tiles - 1)
        def _():
            o_ref[...] = out

        @pl.when(j == n_tiles - 1)
        def _():
            col = lax.broadcasted_iota(jnp.int32, out.shape, 1)
            o_ref[...] = jnp.where(col < keep_cols, out, 0.0) + jnp.dot(h, extra_ref[...],
                                                                        preferred_element_type=F32)
    else:
        o_ref[...] = out


def _in_proj(x, gain, wt, layer, src_rows, extra, keep_cols, tm=1024, tn=512):
    m, k = x.shape
    n_tiles = len(src_rows)
    assert all(r % IN_PROJ_ROW_UNIT == 0 for r in src_rows)
    src = jnp.asarray([r // IN_PROJ_ROW_UNIT for r in src_rows], jnp.int32)
    element = lambda n: pl.Element(n)
    in_specs = [pl.BlockSpec((tm, k), lambda i, j, src: (i, 0)),
                pl.BlockSpec((1, k), lambda i, j, src: (0, 0)),
                pl.BlockSpec((element(1), element(tn), element(k)),
                             lambda i, j, src: (layer, src[j] * IN_PROJ_ROW_UNIT, 0))]
    args = [src, x, gain.reshape(1, k), wt]
    if extra is not None:
        in_specs.append(pl.BlockSpec((k, tn), lambda i, j, src: (0, 0)))
        args.append(extra)
    return pl.pallas_call(
        functools.partial(_in_proj_body, has_extra=extra is not None, n_tiles=n_tiles, keep_cols=keep_cols),
        grid_spec=pltpu.PrefetchScalarGridSpec(
            num_scalar_prefetch=1,
            grid=(m // tm, n_tiles),
            in_specs=in_specs,
            out_specs=pl.BlockSpec((tm, tn), lambda i, j, src: (i, j)),
            scratch_shapes=[pltpu.VMEM((tm, k), BF16)]),
        out_shape=jax.ShapeDtypeStruct((m, n_tiles * tn), F32),
        compiler_params=_cparams(2),
        name="in_proj",
    )(*args)


def _head_sum(x, sel, sel_t):
    s = _dot_lhs_split(x, sel)
    return _dot_lhs_split(s, sel_t)


def _token_shift(p, carry_ref, mu):
    rows = p.shape[0]
    row = lax.broadcasted_iota(jnp.int32, p.shape, 0)
    prev = jnp.where(row == 0, carry_ref[...], pltpu.roll(p, 1, axis=0))
    carry_ref[...] = p[rows - 1:rows, :]
    return p + (prev - p) * mu


def _prep_body(*refs, tiles_per_seq, has_vres, emit_v, width):
    (rkv_ref, lora_ref, mu_rkv_ref, mu_lora_ref, w0_ref, a0_ref, kk_ref, ka_ref, rk_ref,
     dup_ref, iup_ref, gup_ref, sel_ref, selt_ref, tri_ref, ones_ref) = refs[:16]
    pos = 16
    if has_vres:
        vfirst_ref, v0_ref, vup_ref = refs[pos:pos + 3]
        pos += 3
    at_o, rt_o, bt_o, kt_o, bp_o, kp_o, vb_o, wc_o, bonus_o, g_o = refs[pos:pos + 10]
    pos += 10
    if emit_v:
        v_o = refs[pos]
        pos += 1
    carry_rkv, carry_lora = refs[pos:]
    R = width

    @pl.when(pl.program_id(0) % tiles_per_seq == 0)
    def _():
        carry_rkv[...] = jnp.zeros_like(carry_rkv)
        carry_lora[...] = jnp.zeros_like(carry_lora)

    z = _token_shift(rkv_ref[...], carry_rkv, mu_rkv_ref[...])
    zl = _token_shift(lora_ref[...], carry_lora, mu_lora_ref[...])
    r, k, v = z[:, :R], z[:, R:2 * R], z[:, 2 * R:]

    decay_logit = w0_ref[...] + _dot_x3(jnp.tanh(zl[:, :128]), dup_ref[...])
    neg = -decay_logit
    softplus = jnp.maximum(neg, 0.0) + jnp.log(1.0 + jnp.exp(-jnp.abs(neg)))
    lw = -jnp.exp(-softplus - 0.5)
    a = _sigmoid(a0_ref[...] + _dot_x3(zl[:, :256], iup_ref[...]))
    g_o[...] = _dot(_sigmoid(zl[:, 128:]), gup_ref[...])
    if has_vres:
        mix = _sigmoid(v0_ref[...] + _dot(zl[:, 384:], vup_ref[...]))
        v = v + (vfirst_ref[...] - v) * mix
    if emit_v:
        v_o[...] = v

    sel, selt = sel_ref[...], selt_ref[...]
    kk = k * kk_ref[...]
    norm = jnp.sqrt(_head_sum(kk * kk, sel, selt))
    kk = kk / jnp.maximum(norm, 1e-12)
    k = k * (1.0 + (a - 1.0) * ka_ref[...])
    bonus_o[...] = _head_sum(r * k * rk_ref[...], sel, selt) * v

    d = functools.partial(jnp.dot, preferred_element_type=F32)
    lw_hi = lw.astype(BF16)
    rest = lw - lw_hi.astype(F32)
    lw_mid = rest.astype(BF16)
    lw_lo = (rest - lw_mid.astype(F32)).astype(BF16)
    tri, ones = tri_ref[...], ones_ref[...]
    cum = d(tri, lw_hi) + d(tri, lw_mid) + d(tri, lw_lo)
    total = d(ones, lw_hi) + d(ones, lw_mid) + d(ones, lw_lo)
    w_inv = jnp.exp(-cum)
    w_rem = jnp.exp(total - cum)
    b = kk * a
    wc_o[...] = jnp.exp(total)
    pairs = R // LANES
    for o_ref, val in ((at_o, -kk * jnp.exp(cum - lw)), (rt_o, r * jnp.exp(cum)), (bt_o, b * w_inv),
                       (kt_o, k * w_inv), (bp_o, b * w_rem), (kp_o, k * w_rem), (vb_o, v)):
        val = val.astype(BF16)
        for p in range(pairs):
            o_ref[0, p] = val[:, p * LANES:(p + 1) * LANES]


def _rwkv_prep(p_all, bsz, seq, width, consts, vres, emit_v):
    tm = PREP_ROWS
    m = p_all.shape[0]
    R = width
    lora_w = 512
    lora_blk = (p_all.shape[1] - lora_w) // lora_w
    tiles_per_seq = seq // tm
    row = lambda i: (i, 0)
    fixed = lambda i: (0, 0)
    in_specs = [pl.BlockSpec((tm, 3 * R), row),
                pl.BlockSpec((tm, lora_w), lambda i: (i, lora_blk))]
    args = [p_all, p_all]
    for c in consts:
        in_specs.append(pl.BlockSpec(c.shape, fixed))
        args.append(c)
    if vres is not None:
        vfirst, v0, vup = vres
        in_specs += [pl.BlockSpec((tm, R), row), pl.BlockSpec(v0.shape, fixed), pl.BlockSpec(vup.shape, fixed)]
        args += [vfirst, v0, vup]
    pairs = R // LANES
    pm_spec = pl.BlockSpec((1, pairs, tm, LANES), lambda i: (i // tiles_per_seq, 0, i % tiles_per_seq, 0))
    pm_shape = jax.ShapeDtypeStruct((bsz, pairs, seq, LANES), BF16)
    nat_spec = pl.BlockSpec((tm, R), row)
    nat_shape = jax.ShapeDtypeStruct((m, R), F32)
    n_nat = 4 if emit_v else 3
    return pl.pallas_call(
        functools.partial(_prep_body, tiles_per_seq=tiles_per_seq, has_vres=vres is not None, emit_v=emit_v,
                          width=R),
        grid=(m // tm,),
        in_specs=in_specs,
        out_specs=[pm_spec] * 7 + [nat_spec] * n_nat,
        out_shape=[pm_shape] * 7 + [nat_shape] * n_nat,
        scratch_shapes=[pltpu.VMEM((1, 3 * R), F32), pltpu.VMEM((1, lora_w), F32)],
        compiler_params=_cparams(1),
        name="rwkv_prep",
    )(*args)


def _wkv_body(at_ref, rt_ref, bt_ref, kt_ref, bp_ref, kp_ref, v_ref, wc_ref, y_ref, state_ref, *, pairs, chunks):
    C = WKV_CHUNK
    hs = HEAD_SIZE

    @pl.when(pl.program_id(1) == 0)
    def _():
        state_ref[...] = jnp.zeros_like(state_ref)

    def head_masks(rows, width):
        lane = lax.broadcasted_iota(jnp.int32, (rows, width), 1)
        first = (lane % LANES) < hs
        return first, jnp.logical_not(first)

    def stack_heads(x):
        m0, m1 = head_masks(*x.shape)
        zero = jnp.zeros_like(x)
        return jnp.concatenate([jnp.where(m0, x, zero), jnp.where(m1, x, zero)], axis=0)

    row = lax.broadcasted_iota(jnp.int32, (C, LANES), 0)
    col = lax.broadcasted_iota(jnp.int32, (C, LANES), 1) % hs
    strict, incl = row > col, row >= col
    sq_r = lax.broadcasted_iota(jnp.int32, (LANES, LANES), 0)
    sq_c = lax.broadcasted_iota(jnp.int32, (LANES, LANES), 1)
    same_head = (sq_r // hs) == (sq_c // hs)
    eye = sq_r == sq_c
    levels = C.bit_length() - 1
    dot = functools.partial(jnp.dot, preferred_element_type=F32)
    P = range(pairs)

    def step(c, carry):
        sl = pl.ds(pl.multiple_of(c * C, C), C)
        at = [at_ref[0, p, sl, :] for p in P]
        rt = [rt_ref[0, p, sl, :] for p in P]
        v = [v_ref[0, p, sl, :] for p in P]
        vs = [stack_heads(v[p]) for p in P]
        lhs = [jnp.concatenate([at[p], rt[p]], axis=0) for p in P]
        rhs = [jnp.concatenate([stack_heads(bt_ref[0, p, sl, :]), stack_heads(kt_ref[0, p, sl, :])], axis=0)
               for p in P]
        abk = [_dot_nt(lhs[p], rhs[p]) for p in P]
        n = [jnp.where(strict, abk[p][:C, :LANES], 0.0) for p in P]
        a_rb = [jnp.where(incl, abk[p][C:, :LANES], 0.0) for p in P]
        a_ak = [jnp.where(strict, abk[p][:C, LANES:], 0.0) for p in P]
        a_rk = [jnp.where(incl, abk[p][C:, LANES:], 0.0) for p in P]
        av = [dot(a_ak[p].astype(BF16), vs[p]) for p in P]
        x = [jnp.concatenate([at[p].astype(F32), av[p]], axis=1) for p in P]
        for lvl in range(levels):
            last = lvl + 1 == levels
            for p in P:
                z = x[p] if last else jnp.concatenate([n[p], x[p]], axis=1)
                out = dot(n[p].astype(BF16), stack_heads(z.astype(BF16)))
                if last:
                    x[p] = x[p] + out
                else:
                    n[p] = out[:, :LANES]
                    x[p] = x[p] + out[:, LANES:]
        xb = [x[p].astype(BF16) for p in P]
        zeros_c = jnp.zeros((C, LANES), BF16)
        bx = [_dot_tn(jnp.concatenate([bp_ref[0, p, sl, :], kp_ref[0, p, sl, :]], axis=0),
                      jnp.concatenate([xb[p], jnp.concatenate([zeros_c, v[p]], axis=1)], axis=0)) for p in P]
        zeros_2c = jnp.zeros((2 * C, LANES), BF16)
        rx = [dot(jnp.concatenate([a_rb[p], a_rk[p]], axis=1).astype(BF16),
                  jnp.concatenate([stack_heads(xb[p]), jnp.concatenate([zeros_2c, vs[p]], axis=1)], axis=0))
              for p in P]
        for p in P:
            state = state_ref[p]
            wc = wc_ref[pl.ds(pl.multiple_of(c * C, C), 1), p * LANES:(p + 1) * LANES]
            trans = jnp.where(same_head, bx[p][:, :LANES], 0.0) + jnp.where(eye, wc, 0.0)
            inject = jnp.where(same_head, bx[p][:, LANES:], 0.0)
            r_eff = rt[p].astype(F32) + rx[p][:, :LANES]
            y_ref[0, p, sl, :] = _dot(r_eff, state) + rx[p][:, LANES:]
            state_ref[p] = _dot_x3(trans, state) + inject
        return carry

    lax.fori_loop(0, chunks, step, 0)


def _wkv(at, rt, bt, kt, bp, kp, v, wc, pairs=8, span=512):
    bsz, n_pairs, seq, _ = at.shape
    groups = n_pairs // pairs
    spec = pl.BlockSpec((1, pairs, span, LANES), lambda g, t: (g // groups, g % groups, t, 0))
    wc_spec = pl.BlockSpec((span, pairs * LANES), lambda g, t: ((g // groups) * (seq // span) + t, g % groups))
    return pl.pallas_call(
        functools.partial(_wkv_body, pairs=pairs, chunks=span // WKV_CHUNK),
        grid=(bsz * groups, seq // span),
        in_specs=[spec] * 7 + [wc_spec],
        out_specs=spec,
        out_shape=jax.ShapeDtypeStruct((bsz, n_pairs, seq, LANES), F32),
        scratch_shapes=[pltpu.VMEM((pairs, LANES, LANES), F32)],
        compiler_params=_cparams(2),
        name="wkv7",
    )(at, rt, bt, kt, bp, kp, v, wc)


def _post_pool_body(y_ref, bonus_ref, g_ref, u_ref, lnw_ref, lnb_ref, sel_ref, selt_ref,
                    pw_ref, ps_ref, ya_o, yb_o, ext_ref, *, tiles_per_seq, tm, group_width):
    sel, selt = sel_ref[...], selt_ref[...]
    inv_n = 1.0 / HEAD_SIZE
    y = jnp.concatenate([y_ref[0, p] for p in range(y_ref.shape[1])], axis=1)
    mean = _head_sum(y, sel, selt) * inv_n
    yc = y - mean
    var = _head_sum(yc * yc, sel, selt) * inv_n
    yn = yc * lax.rsqrt(var + GN_EPS) * lnw_ref[...] + lnb_ref[...]
    ya_o[...] = ((yn + bonus_ref[...]) * g_ref[...]).astype(ya_o.dtype)

    seq_tile = pl.program_id(0) % tiles_per_seq

    @pl.when(seq_tile == 0)
    def _():
        ext_ref[0:POOL_HALO, :] = jnp.zeros((POOL_HALO, ext_ref.shape[1]), F32)

    u = u_ref[...]
    ext_ref[POOL_HALO:POOL_HALO + tm, :] = u
    t_pos = seq_tile * tm + lax.broadcasted_iota(jnp.int32, (tm, group_width), 0)
    for gi, w in enumerate(POOL_WINDOWS):
        lanes = slice(gi * group_width, (gi + 1) * group_width)
        acc = u[:, lanes]
        for j in range(1, w):
            acc = acc + ext_ref[POOL_HALO - j:POOL_HALO - j + tm, lanes]
        count = jnp.minimum(t_pos + 1, w).astype(F32)
        mixed = acc / count - u[:, lanes]
        yg = _dot_x3(mixed, pw_ref[gi])
        yb_o[:, lanes] = (yg * ps_ref[:, lanes]).astype(yb_o.dtype)
    ext_ref[0:POOL_HALO, :] = u[tm - POOL_HALO:, :]


def _post_pool(y, bonus, g, p_all, pool_blk, consts, seq, tm=256):
    m, R = bonus.shape
    lnw, lnb, sel, selt, pw, ps = consts
    tiles_per_seq = seq // tm
    row = lambda i: (i, 0)
    fixed2 = lambda i: (0, 0)
    in_specs = [pl.BlockSpec((1, y.shape[1], tm, LANES), lambda i: (i // tiles_per_seq, 0, i % tiles_per_seq, 0)),
                pl.BlockSpec((tm, R), row), pl.BlockSpec((tm, R), row),
                pl.BlockSpec((tm, R), lambda i: (i, pool_blk))]
    in_specs += [pl.BlockSpec(c.shape, fixed2) for c in (lnw, lnb, sel, selt)]
    in_specs += [pl.BlockSpec(pw.shape, lambda i: (0, 0, 0)), pl.BlockSpec(ps.shape, fixed2)]
    out = jax.ShapeDtypeStruct((m, R), BF16)
    return pl.pallas_call(
        functools.partial(_post_pool_body, tiles_per_seq=tiles_per_seq, tm=tm, group_width=R // len(POOL_WINDOWS)),
        grid=(m // tm,),
        in_specs=in_specs,
        out_specs=[pl.BlockSpec((tm, R), row)] * 2,
        out_shape=[out, out],
        scratch_shapes=[pltpu.VMEM((POOL_HALO + tm, R), F32)],
        compiler_params=_cparams(1),
        name="post_pool",
    )(y, bonus, g, p_all, lnw, lnb, sel, selt, pw, ps)


def _mix_out_body(ya_ref, yb_ref, ga_ref, gb_ref, x_ref, pa_ref, pb_ref, wo_ref, o_ref):
    ma = jnp.dot(ya_ref[...], pa_ref[...], preferred_element_type=F32)
    mb = jnp.dot(yb_ref[...], pb_ref[...], preferred_element_type=F32)
    merged = (_sigmoid(ga_ref[...]) * ma + _sigmoid(gb_ref[...]) * mb).astype(BF16)
    o_ref[...] = x_ref[...] + jnp.dot(merged, wo_ref[...], preferred_element_type=F32)


def _mix_out(ya, yb, p_all, ga_col, gb_col, x, pa, pb, wo, tm=256):
    m, R = ya.shape
    d = wo.shape[1]
    row = lambda i: (i, 0)
    resident = lambda shape: pl.BlockSpec(shape, lambda i: (0, 0), pipeline_mode=pl.Buffered(1))
    return pl.pallas_call(
        _mix_out_body,
        grid=(m // tm,),
        in_specs=[pl.BlockSpec((tm, R), row), pl.BlockSpec((tm, R), row),
                  pl.BlockSpec((tm, d), lambda i: (i, ga_col // d)),
                  pl.BlockSpec((tm, d), lambda i: (i, gb_col // d)),
                  pl.BlockSpec((tm, d), row),
                  resident(pa.shape), resident(pb.shape), resident(wo.shape)],
        out_specs=pl.BlockSpec((tm, d), row),
        out_shape=jax.ShapeDtypeStruct((m, d), F32),
        compiler_params=_cparams(1),
        name="mix_out",
    )(ya, yb, p_all, p_all, x, pa, pb, wo)


def _router_body(x_ref, g_ref, wr_ref, tri_ref, h_o, idx_o, gate_o, cnt_o, carry_ref):
    @pl.when(pl.program_id(0) == 0)
    def _():
        carry_ref[...] = jnp.zeros_like(carry_ref)

    x = x_ref[...]
    inv = lax.rsqrt(jnp.mean(x * x, axis=-1, keepdims=True) + NORM_EPS)
    h = x * inv * g_ref[...]
    h_o[...] = h.reshape(h_o.shape)
    logits = _dot_x3(h, wr_ref[...])
    lane = lax.broadcasted_iota(jnp.int32, logits.shape, 1)
    neg_inf = jnp.float32(-jnp.inf)
    l1 = jnp.where(lane < N_EXPERTS, logits, neg_inf)
    m1 = jnp.max(l1, axis=-1, keepdims=True)
    i1 = jnp.min(jnp.where(l1 == m1, lane, LANES), axis=-1, keepdims=True)
    l2 = jnp.where(lane == i1, neg_inf, l1)
    m2 = jnp.max(l2, axis=-1, keepdims=True)
    i2 = jnp.min(jnp.where(l2 == m2, lane, LANES), axis=-1, keepdims=True)
    e = jnp.exp(m2 - m1)
    g1 = 1.0 / (1.0 + e)
    gate_o[...] = jnp.where(lane == 0, g1, jnp.where(lane == 1, e * g1, 0.0))

    first, second = lane == i1, lane == i2
    picked = jnp.where(jnp.logical_or(first, second), 1.0, 0.0)
    before = carry_ref[...] + jnp.dot(tri_ref[...], picked.astype(BF16), preferred_element_type=F32)
    rank1 = jnp.sum(jnp.where(first, before, 0.0), axis=-1, keepdims=True).astype(jnp.int32)
    rank2 = jnp.sum(jnp.where(second, before, 0.0), axis=-1, keepdims=True).astype(jnp.int32)
    idx_o[...] = jnp.where(lane == 0, i1, jnp.where(lane == 1, i2, jnp.where(lane == 2, rank1,
                                                                              jnp.where(lane == 3, rank2, 0))))
    carry_ref[...] += jnp.sum(picked, axis=0, keepdims=True)
    cnt_o[...] = jnp.broadcast_to(carry_ref[...], cnt_o.shape).astype(jnp.int32)


def _router(x, g, wr_pad, tm=256):
    m, d = x.shape
    row = lambda i: (i, 0)
    t_idx = jnp.arange(tm, dtype=jnp.int32)
    tri_strict = (t_idx[None, :] < t_idx[:, None]).astype(BF16)
    return pl.pallas_call(
        _router_body,
        grid=(m // tm,),
        in_specs=[pl.BlockSpec((tm, d), row), pl.BlockSpec((1, d), lambda i: (0, 0)),
                  pl.BlockSpec(wr_pad.shape, lambda i: (0, 0)), pl.BlockSpec((tm, tm), lambda i: (0, 0))],
        out_specs=[pl.BlockSpec((tm, d // LANES, LANES), lambda i: (i, 0, 0)),
                   pl.BlockSpec((tm, LANES), row), pl.BlockSpec((tm, LANES), row),
                   pl.BlockSpec((8, LANES), lambda i: (0, 0))],
        out_shape=[jax.ShapeDtypeStruct((m, d // LANES, LANES), F32), jax.ShapeDtypeStruct((m, LANES), jnp.int32),
                   jax.ShapeDtypeStruct((m, LANES), F32), jax.ShapeDtypeStruct((8, LANES), jnp.int32)],
        scratch_shapes=[pltpu.VMEM((1, LANES), F32)],
        compiler_params=_cparams(1),
        name="router",
    )(x, g.reshape(1, d), wr_pad, tri_strict)


def _gather_body(tok_ref, used_ref, h_hbm, o_ref, buf_ref, sems, *, tg, n_steps):
    i = pl.program_id(0)

    def row_copy(step, r):
        slot = step % 2
        return pltpu.make_async_copy(h_hbm.at[pl.ds(tok_ref[step * tg + r], 1)], buf_ref.at[slot, pl.ds(r, 1)],
                                     sems.at[slot])

    def fetch(step):
        @pl.when(jnp.logical_and(step < n_steps, step * tg < used_ref[0]))
        def _():
            def start(q, c):
                for u in range(DMA_UNROLL):
                    row_copy(step, q * DMA_UNROLL + u).start()
                return c
            lax.fori_loop(0, tg // DMA_UNROLL, start, 0)

    @pl.when(i == 0)
    def _():
        fetch(i)

    fetch(i + 1)

    @pl.when(i * tg < used_ref[0])
    def _():
        pltpu.make_async_copy(h_hbm.at[pl.ds(0, tg)], buf_ref.at[i % 2], sems.at[i % 2]).wait()
        o_ref[...] = buf_ref[i % 2].reshape(o_ref.shape).astype(o_ref.dtype)

    @pl.when(i * tg >= used_ref[0])
    def _():
        o_ref[...] = jnp.zeros_like(o_ref)


def _gather_rows(tok, used_rows, h3, tg=256):
    p = tok.shape[0]
    _, nc, lanes = h3.shape
    n_steps = p // tg
    return pl.pallas_call(
        functools.partial(_gather_body, tg=tg, n_steps=n_steps),
        grid_spec=pltpu.PrefetchScalarGridSpec(
            num_scalar_prefetch=2,
            grid=(n_steps,),
            in_specs=[pl.BlockSpec(memory_space=pl.ANY)],
            out_specs=pl.BlockSpec((tg, nc * lanes), lambda i, tok, used: (i, 0)),
            scratch_shapes=[pltpu.VMEM((2, tg, nc, lanes), F32), pltpu.SemaphoreType.DMA((2,))]),
        out_shape=jax.ShapeDtypeStruct((p, nc * lanes), BF16),
        compiler_params=_cparams(1),
        name="moe_gather",
    )(tok, used_rows, h3)


DMA_UNROLL = 8
MOE_ROWS = 1024
MOE_SUB = 256
MOE_FF_TILE = 256
PLAN_FIELDS = 7


def _experts_body(plan_ref, x_ref, wg_ref, wu_ref, wd_ref, o_ref, wgb_ref, wub_ref, wdb_ref, *, n_items):
    w = pl.program_id(0)
    j = pl.program_id(1)
    field = lambda f: plan_ref[f * n_items + w]
    lo, hi, zero, live = field(3), field(4), field(5), field(6)
    n_sub = MOE_ROWS // MOE_SUB

    @pl.when(jnp.logical_and(zero == 1, j == 0))
    def _():
        o_ref[...] = jnp.zeros_like(o_ref)

    def rows(sl, wg, wu, wd):
        x = x_ref[sl, :]
        g = jnp.dot(x, wg, preferred_element_type=F32)
        u = jnp.dot(x, wu, preferred_element_type=F32)
        h = (g * _sigmoid(g) * u).astype(BF16)
        o_ref[sl, :] += jnp.dot(h, wd, preferred_element_type=F32)

    whole = jnp.logical_and(live == 1, jnp.logical_and(lo == 0, hi == n_sub))
    partial = jnp.logical_and(live == 1, jnp.logical_not(whole))

    @pl.when(whole)
    def _():
        rows(slice(None), wg_ref[...].astype(BF16), wu_ref[...].astype(BF16), wd_ref[...].astype(BF16))

    @pl.when(partial)
    def _():
        wgb_ref[...] = wg_ref[...].astype(BF16)
        wub_ref[...] = wu_ref[...].astype(BF16)
        wdb_ref[...] = wd_ref[...].astype(BF16)

    for s in range(n_sub):
        @pl.when(jnp.logical_and(partial, jnp.logical_and(lo <= s, s < hi)))
        def _():
            rows(slice(s * MOE_SUB, (s + 1) * MOE_SUB), wgb_ref[...], wub_ref[...], wdb_ref[...])


def _experts(plan, xs, w_gate, w_up, w_down):
    p, d = xs.shape
    f = w_gate.shape[2]
    tn = MOE_FF_TILE
    nj = f // tn
    n_items = plan.shape[0] // PLAN_FIELDS
    fld = lambda plan, f_idx, w: plan[f_idx * n_items + w]
    col = lambda plan, w, j: jnp.where(fld(plan, 6, w) == 1, j, nj - 1)
    return pl.pallas_call(
        functools.partial(_experts_body, n_items=n_items),
        grid_spec=pltpu.PrefetchScalarGridSpec(
            num_scalar_prefetch=1,
            grid=(n_items, nj),
            in_specs=[pl.BlockSpec((MOE_ROWS, d), lambda w, j, plan: (fld(plan, 1, w), 0)),
                      pl.BlockSpec((None, d, tn), lambda w, j, plan: (fld(plan, 2, w), 0, col(plan, w, j))),
                      pl.BlockSpec((None, d, tn), lambda w, j, plan: (fld(plan, 2, w), 0, col(plan, w, j))),
                      pl.BlockSpec((None, tn, d), lambda w, j, plan: (fld(plan, 2, w), col(plan, w, j), 0))],
            out_specs=pl.BlockSpec((MOE_ROWS, d), lambda w, j, plan: (fld(plan, 0, w), 0)),
            scratch_shapes=[pltpu.VMEM((d, tn), BF16), pltpu.VMEM((d, tn), BF16), pltpu.VMEM((tn, d), BF16)]),
        out_shape=jax.ShapeDtypeStruct((p, d), F32),
        compiler_params=_cparams(2),
        name="moe_experts",
    )(plan, xs, w_gate, w_up, w_down)


def _combine_body(d0_ref, d1_ref, x_ref, gate_ref, g_ref, y_hbm, o_ref, buf_ref, sems, *, tg, n_steps, apply_norm):
    i = pl.program_id(0)

    def row_copy(step, r, d_ref, pick):
        slot = step % 2
        return pltpu.make_async_copy(y_hbm.at[pl.ds(d_ref[step * tg + r], 1), :],
                                     buf_ref.at[slot, pick, pl.ds(r, 1), :], sems.at[slot])

    def fetch(step):
        @pl.when(step < n_steps)
        def _():
            def start(q, c):
                for u in range(DMA_UNROLL):
                    row_copy(step, q * DMA_UNROLL + u, d0_ref, 0).start()
                    row_copy(step, q * DMA_UNROLL + u, d1_ref, 1).start()
                return c
            lax.fori_loop(0, tg // DMA_UNROLL, start, 0)

    @pl.when(i == 0)
    def _():
        fetch(i)

    fetch(i + 1)
    slot = i % 2
    for pick in range(2):
        pltpu.make_async_copy(y_hbm.at[pl.ds(0, tg), :], buf_ref.at[slot, pick], sems.at[slot]).wait()
    gate = gate_ref[...]
    x = x_ref[...] + gate[:, 0:1] * buf_ref[slot, 0] + gate[:, 1:2] * buf_ref[slot, 1]
    if apply_norm:
        x = x * lax.rsqrt(jnp.mean(x * x, axis=-1, keepdims=True) + NORM_EPS) * g_ref[...]
    o_ref[...] = x


def _combine(d0, d1, x, gate, g, ybuf, apply_norm, tg=256):
    m, d = x.shape
    row = lambda i, d0, d1: (i, 0)
    return pl.pallas_call(
        functools.partial(_combine_body, tg=tg, n_steps=m // tg, apply_norm=apply_norm),
        grid_spec=pltpu.PrefetchScalarGridSpec(
            num_scalar_prefetch=2,
            grid=(m // tg,),
            in_specs=[pl.BlockSpec((tg, d), row), pl.BlockSpec((tg, LANES), row),
                      pl.BlockSpec((1, d), lambda i, d0, d1: (0, 0)),
                      pl.BlockSpec(memory_space=pl.ANY)],
            out_specs=pl.BlockSpec((tg, d), row),
            scratch_shapes=[pltpu.VMEM((2, 2, tg, d), F32), pltpu.SemaphoreType.DMA((2,))]),
        out_shape=jax.ShapeDtypeStruct((m, d), F32),
        compiler_params=_cparams(1),
        name="moe_combine",
    )(d0, d1, x, gate, g.reshape(1, d), ybuf)


def _moe_plan(top_e, rank, counts):
    i32 = jnp.int32
    n_tok = top_e.shape[0]
    n_slot = n_tok * TOP_K
    flat_e = top_e.reshape(n_slot)
    rank = rank.reshape(n_slot)
    padded = (counts + MOE_SUB - 1) // MOE_SUB * MOE_SUB
    pend = jnp.cumsum(padded)
    pstart = pend - padded
    dest = pstart[flat_e] + rank
    n_blocks = -(-(n_slot + N_EXPERTS * MOE_SUB) // MOE_ROWS)
    tok = jnp.zeros((n_blocks * MOE_ROWS,), i32).at[dest].set(jnp.arange(n_slot, dtype=i32) // TOP_K)
    used_rows = pend[-1]
    used_blocks = (used_rows + MOE_ROWS - 1) // MOE_ROWS

    n_items = n_blocks + N_EXPERTS
    blk_lo = (jnp.arange(n_blocks, dtype=i32) * MOE_ROWS)[:, None]
    lo_row = jnp.maximum(pstart[None, :], blk_lo)
    hi_row = jnp.minimum(pend[None, :], blk_lo + MOE_ROWS)
    valid = (hi_row > lo_row).reshape(-1)
    n_live = jnp.sum(valid.astype(i32))
    pos = jnp.where(valid, jnp.cumsum(valid.astype(i32)) - 1, n_items)
    compact = lambda v: jnp.zeros((n_items,), i32).at[pos].set(v.reshape(-1).astype(i32), mode="drop")
    blk_c = compact(jnp.broadcast_to(jnp.arange(n_blocks, dtype=i32)[:, None], lo_row.shape))
    exp_c = compact(jnp.broadcast_to(jnp.arange(N_EXPERTS, dtype=i32)[None, :], lo_row.shape))
    lo_c = compact((lo_row - blk_lo) // MOE_SUB)
    hi_c = compact((hi_row - blk_lo) // MOE_SUB)
    first_c = compact(lo_row == blk_lo)

    w = jnp.arange(n_items, dtype=i32)
    live = w < n_live
    tail_blk = used_blocks + (w - n_live)
    last = n_live - 1
    out_blk = jnp.where(live, blk_c, jnp.minimum(tail_blk, n_blocks - 1))
    x_blk = jnp.where(live, blk_c, blk_c[last])
    expert = jnp.where(live, exp_c, exp_c[last])
    zero = jnp.where(live, first_c, (tail_blk < n_blocks).astype(i32))
    plan = jnp.concatenate([out_blk, x_blk, expert, lo_c, hi_c, zero, live.astype(i32)]).astype(i32)
    dest = dest.reshape(n_tok, TOP_K).astype(i32)
    return tok, used_rows.astype(i32).reshape(1), plan, dest[:, 0], dest[:, 1]


def kernel(x, norm_mix, w_in, shift_mu, decay_w0, decay_up, iclr_a0, iclr_up, outgate_up, k_k, k_a, r_k, lnx_w, lnx_b, vres_down, vres_mu, vres_up, vres_v0, pool_w, pool_scale, proj_a, proj_b, w_o, norm_ffn, ffn_gate, ffn_up, ffn_down, router, moe_gate, moe_up, moe_down, norm_final):
    bsz, seq, d = x.shape
    depth = w_in.shape[0]
    R = decay_w0.shape[1]
    dl, il, gl = decay_up.shape[1], iclr_up.shape[1], outgate_up.shape[1]
    vl = vres_up.shape[1]
    assert (dl, il, vl, gl) == (96, 96, 64, 256) and 3 * R + dl + il + gl == shift_mu.shape[1]
    m = bsz * seq
    x = x.reshape(m, d)
    row = lambda t: t.reshape(1, -1)

    head_of_lane = jnp.arange(R, dtype=jnp.int32) // HEAD_SIZE
    sel = (head_of_lane[:, None] == jnp.arange(LANES, dtype=jnp.int32)[None, :]).astype(BF16)
    sel_t = sel.T
    t_idx = jnp.arange(PREP_ROWS, dtype=jnp.int32)
    same_chunk = (t_idx[:, None] // WKV_CHUNK) == (t_idx[None, :] // WKV_CHUNK)
    ones_bd = same_chunk.astype(BF16)
    tri_bd = (same_chunk & (t_idx[None, :] <= t_idx[:, None])).astype(BF16)

    w_in_t = jnp.swapaxes(w_in, 1, 2)
    v_first = None
    for l in range(depth):
        c0 = 3 * R
        c3 = c0 + dl + il + gl
        tn = 512
        lora_w = c3 - c0 + vl
        assert c0 % tn == 0 and (w_in.shape[2] - c3) % tn == 0 and lora_w == tn
        src_rows = list(range(0, c0, tn)) + list(range(c3, w_in.shape[2], tn)) + [c0]
        vd_mu = vres_mu[l - 1] if l > 0 else jnp.zeros((vl,), F32)
        extra = None
        if l > 0:
            extra = jnp.concatenate([jnp.zeros((d, c3 - c0), F32), vres_down[l - 1]], axis=1).astype(BF16)
        mu = shift_mu[l]
        mu_rkv = row(mu[:c0])
        mu_lora = row(jnp.concatenate([mu[c0:c3], vd_mu]))
        pool_col = c0
        ga_col = c0 + R
        gb_col = ga_col + d
        zeros = lambda n: jnp.zeros((n, R), F32)
        dup = jnp.concatenate([decay_up[l], zeros(128 - dl)], axis=0)
        iup = jnp.concatenate([zeros(dl), iclr_up[l], zeros(256 - dl - il)], axis=0)
        gup = jnp.concatenate([zeros(dl + il - 128), outgate_up[l], zeros(vl)], axis=0).astype(BF16)

        p_all = _in_proj(x, norm_mix[l], w_in_t, l, src_rows, extra, c3 - c0, tn=tn)

        consts = [mu_rkv, mu_lora, row(decay_w0[l]), row(iclr_a0[l]), row(k_k[l]), row(k_a[l]), row(r_k[l]),
                  dup, iup, gup, sel, sel_t, tri_bd, ones_bd]
        vres = None
        if l > 0:
            vup = jnp.concatenate([zeros(128 - vl), vres_up[l - 1]], axis=0).astype(BF16)
            vres = (v_first, row(vres_v0[l - 1]), vup)
        emit_v = l == 0 and depth > 1
        outs = _rwkv_prep(p_all, bsz, seq, R, consts, vres, emit_v)
        wc, bonus, g = outs[7:10]
        if emit_v:
            v_first = outs[10]
        y = _wkv(*outs[:7], wc)

        ya, yb = _post_pool(y, bonus, g, p_all, pool_col // R,
                            [row(lnx_w[l]), row(lnx_b[l]), sel, sel_t, pool_w[l], row(pool_scale[l])], seq)
        assert ga_col % d == 0 and gb_col % d == 0
        x = _mix_out(ya, yb, p_all, ga_col, gb_col, x,
                     proj_a[l].astype(BF16), proj_b[l].astype(BF16), w_o[l].astype(BF16))

        i = l // 2
        if l % 2 == 0:
            hidden = _norm_matmul(x, norm_ffn[l], [ffn_gate, ffn_up], tm=1024, tn=512, out_dtype=BF16, layer=i)
            x = _matmul(hidden, [ffn_down], tm=512, tn=512, out_dtype=F32, residual=x, layer=i)
            if l == depth - 1:
                x = _rmsnorm(x, norm_final, F32)
        else:
            wr_pad = jnp.concatenate([router[i], jnp.zeros((d, LANES - N_EXPERTS), F32)], axis=1)
            h3, idx, gate, counts = _router(x, norm_ffn[l], wr_pad)
            tok, used_rows, plan, d0, d1 = _moe_plan(idx[:, :TOP_K], idx[:, TOP_K:2 * TOP_K], counts[0, :N_EXPERTS])
            xs = _gather_rows(tok, used_rows, h3)
            ybuf = _experts(plan, xs, moe_gate[i], moe_up[i], moe_down[i])
            x = _combine(d0, d1, x, gate, norm_final, ybuf, apply_norm=l == depth - 1)
    return x.reshape(bsz, seq, d)
```

```python
import functools

import jax
import jax.numpy as jnp
from jax import lax
from jax.experimental import pallas as pl
from jax.experimental.pallas import tpu as pltpu

F32 = jnp.float32
BF16 = jnp.bfloat16

HEAD_SIZE = 64
POOL_WINDOWS = (2, 4, 8, 16)
N_EXPERTS = 8
TOP_K = 2
NORM_EPS = 1e-6
GN_EPS = 64e-5
LANES = 128
VMEM_LIMIT_BYTES = 60 * 1024 * 1024
WKV_CHUNK = 64
POOL_HALO = 16
PREP_ROWS = 256


def _cparams(n_axes):
    return pltpu.CompilerParams(dimension_semantics=("arbitrary",) * n_axes,
                                vmem_limit_bytes=VMEM_LIMIT_BYTES)


def _dot(a, b):
    return jnp.dot(a.astype(BF16), b.astype(BF16), preferred_element_type=F32)


def _split(x):
    hi = x.astype(BF16)
    lo = (x - hi.astype(F32)).astype(BF16)
    return hi, lo


def _dot_x3(a, b):
    a_hi, a_lo = _split(a)
    b_hi, b_lo = _split(b)
    d = functools.partial(jnp.dot, preferred_element_type=F32)
    return d(a_hi, b_hi) + d(a_hi, b_lo) + d(a_lo, b_hi)


def _hi_lo_rows(w):
    hi = w.astype(BF16)
    return jnp.concatenate([hi, (w - hi.astype(F32)).astype(BF16)], axis=0)


def _dot_x3_presplit(a, w_hi_lo):
    k = w_hi_lo.shape[0] // 2
    a_hi, a_lo = _split(a)
    d = functools.partial(jnp.dot, preferred_element_type=F32)
    return d(a_hi, w_hi_lo[:k]) + d(a_hi, w_hi_lo[k:]) + d(a_lo, w_hi_lo[:k])


def _dot_lhs_split(a, b_bf16):
    a_hi, a_lo = _split(a)
    d = functools.partial(jnp.dot, preferred_element_type=F32)
    return d(a_hi, b_bf16) + d(a_lo, b_bf16)


def _dot_tn(a, b):
    return lax.dot_general(a.astype(BF16), b.astype(BF16), (((0,), (0,)), ((), ())),
                           preferred_element_type=F32)


def _dot_nt(a, b):
    return lax.dot_general(a.astype(BF16), b.astype(BF16), (((1,), (1,)), ((), ())),
                           preferred_element_type=F32)


def _sigmoid(x):
    return 1.0 / (1.0 + jnp.exp(-x))


def _rmsnorm_body(x_ref, g_ref, o_ref):
    x = x_ref[...]
    inv = lax.rsqrt(jnp.mean(x * x, axis=-1, keepdims=True) + NORM_EPS)
    o_ref[...] = (x * inv * g_ref[...]).astype(o_ref.dtype)


def _rmsnorm(x, g, out_dtype, tm=512):
    m, d = x.shape
    return pl.pallas_call(
        _rmsnorm_body,
        grid=(m // tm,),
        in_specs=[pl.BlockSpec((tm, d), lambda i: (i, 0)),
                  pl.BlockSpec((1, d), lambda i: (0, 0))],
        out_specs=pl.BlockSpec((tm, d), lambda i: (i, 0)),
        out_shape=jax.ShapeDtypeStruct((m, d), out_dtype),
        compiler_params=_cparams(1),
        name="rmsnorm",
    )(x, g.reshape(1, d))


def _cast_weight_tiles(w_refs, wb_refs):
    @pl.when(pl.program_id(1) == 0)
    def _():
        for w_ref, wb_ref in zip(w_refs, wb_refs):
            wb_ref[...] = w_ref[...].astype(BF16)


def _mm_plain_body(a_ref, w_ref, o_ref, wb_ref):
    _cast_weight_tiles([w_ref], [wb_ref])
    o_ref[...] = jnp.dot(a_ref[...], wb_ref[...], preferred_element_type=F32).astype(o_ref.dtype)


def _mm_swiglu_body(a_ref, wg_ref, wu_ref, o_ref, wgb_ref, wub_ref):
    _cast_weight_tiles([wg_ref, wu_ref], [wgb_ref, wub_ref])
    a = a_ref[...]
    g = jnp.dot(a, wgb_ref[...], preferred_element_type=F32)
    u = jnp.dot(a, wub_ref[...], preferred_element_type=F32)
    o_ref[...] = (g * _sigmoid(g) * u).astype(o_ref.dtype)


def _mm_residual_body(a_ref, w_ref, res_ref, o_ref, wb_ref):
    _cast_weight_tiles([w_ref], [wb_ref])
    o_ref[...] = res_ref[...] + jnp.dot(a_ref[...], wb_ref[...], preferred_element_type=F32)


def _layer_weight_spec(w, layer, k, tn):
    if w.ndim == 2:
        return pl.BlockSpec((k, tn), lambda j, i: (0, j))
    return pl.BlockSpec((None, k, tn), lambda j, i: (layer, 0, j))


def _matmul(a, ws, *, tm, tn, out_dtype, residual=None, layer=0):
    m, k = a.shape
    n = ws[0].shape[-1]
    a_spec = pl.BlockSpec((tm, k), lambda j, i: (i, 0))
    w_spec = _layer_weight_spec(ws[0], layer, k, tn)
    o_spec = pl.BlockSpec((tm, tn), lambda j, i: (i, j))
    if len(ws) == 2:
        body, in_specs, args = _mm_swiglu_body, [a_spec, w_spec, w_spec], (a, ws[0], ws[1])
    elif residual is not None:
        body, in_specs, args = _mm_residual_body, [a_spec, w_spec, o_spec], (a, ws[0], residual)
    else:
        body, in_specs, args = _mm_plain_body, [a_spec, w_spec], (a, ws[0])
    return pl.pallas_call(
        body,
        grid=(n // tn, m // tm),
        in_specs=in_specs,
        out_specs=o_spec,
        out_shape=jax.ShapeDtypeStruct((m, n), out_dtype),
        scratch_shapes=[pltpu.VMEM((k, tn), BF16)] * len(ws),
        compiler_params=_cparams(2),
        name="matmul",
    )(*args)


def _norm_mm_body(*refs, n_w):
    x_ref, g_ref = refs[:2]
    w_refs = refs[2:2 + n_w]
    o_ref, hb_ref = refs[2 + n_w:]

    @pl.when(pl.program_id(1) == 0)
    def _():
        x = x_ref[...]
        inv = lax.rsqrt(jnp.mean(x * x, axis=-1, keepdims=True) + NORM_EPS)
        hb_ref[...] = (x * inv * g_ref[...]).astype(BF16)

    h = hb_ref[...]
    if n_w == 2:
        g = jnp.dot(h, w_refs[0][...].astype(BF16), preferred_element_type=F32)
        u = jnp.dot(h, w_refs[1][...].astype(BF16), preferred_element_type=F32)
        o_ref[...] = (g * _sigmoid(g) * u).astype(o_ref.dtype)
    else:
        o_ref[...] = jnp.dot(h, w_refs[0][...].astype(BF16), preferred_element_type=F32).astype(o_ref.dtype)


def _norm_matmul(x, gain, ws, *, tm, tn, out_dtype, layer=0):
    m, k = x.shape
    n = ws[0].shape[-1]
    if ws[0].ndim == 2:
        w_spec = pl.BlockSpec((k, tn), lambda i, j: (0, j))
    else:
        w_spec = pl.BlockSpec((None, k, tn), lambda i, j: (layer, 0, j))
    return pl.pallas_call(
        functools.partial(_norm_mm_body, n_w=len(ws)),
        grid=(m // tm, n // tn),
        in_specs=[pl.BlockSpec((tm, k), lambda i, j: (i, 0)), pl.BlockSpec((1, k), lambda i, j: (0, 0))]
        + [w_spec] * len(ws),
        out_specs=pl.BlockSpec((tm, tn), lambda i, j: (i, j)),
        out_shape=jax.ShapeDtypeStruct((m, n), out_dtype),
        scratch_shapes=[pltpu.VMEM((tm, k), BF16)],
        compiler_params=_cparams(2),
        name="norm_matmul",
    )(x, gain.reshape(1, k), *ws)


IN_PROJ_ROW_UNIT = 32


def _in_proj_body(*refs, has_extra, n_tiles, keep_cols):
    if has_extra:
        src_ref, x_ref, g_ref, wt_ref, extra_ref, o_ref, hb_ref = refs
    else:
        src_ref, x_ref, g_ref, wt_ref, o_ref, hb_ref = refs
    j = pl.program_id(1)

    @pl.when(j == 0)
    def _():
        x = x_ref[...]
        inv = lax.rsqrt(jnp.mean(x * x, axis=-1, keepdims=True) + NORM_EPS)
        hb_ref[...] = (x * inv * g_ref[...]).astype(BF16)

    h = hb_ref[...]
    out = lax.dot_general(h, wt_ref[0].astype(BF16), (((1,), (1,)), ((), ())), preferred_element_type=F32)
    if has_extra:
        @pl.when(j < n_tiles - 1)
        def _():
            o_ref[...] = out

        @pl.when(j == n_tiles - 1)
        def _():
            col = lax.broadcasted_iota(jnp.int32, out.shape, 1)
            o_ref[...] = jnp.where(col < keep_cols, out, 0.0) + jnp.dot(h, extra_ref[...],
                                                                        preferred_element_type=F32)
    else:
        o_ref[...] = out


def _in_proj(x, gain, wt, layer, src_rows, extra, keep_cols, tm=2048, tn=512):
    m, k = x.shape
    n_tiles = len(src_rows)
    assert all(r % IN_PROJ_ROW_UNIT == 0 for r in src_rows)
    src = jnp.asarray([r // IN_PROJ_ROW_UNIT for r in src_rows], jnp.int32)
    element = lambda n: pl.Element(n)
    tm = min(tm, m)
    in_specs = [pl.BlockSpec((tm, k), lambda i, j, src: (i, 0), pipeline_mode=pl.Buffered(1)),
                pl.BlockSpec((1, k), lambda i, j, src: (0, 0)),
                pl.BlockSpec((element(1), element(tn), element(k)),
                             lambda i, j, src: (layer, src[j] * IN_PROJ_ROW_UNIT, 0))]
    args = [src, x, gain.reshape(1, k), wt]
    if extra is not None:
        in_specs.append(pl.BlockSpec((k, tn), lambda i, j, src: (0, 0), pipeline_mode=pl.Buffered(1)))
        args.append(extra)
    return pl.pallas_call(
        functools.partial(_in_proj_body, has_extra=extra is not None, n_tiles=n_tiles, keep_cols=keep_cols),
        grid_spec=pltpu.PrefetchScalarGridSpec(
            num_scalar_prefetch=1,
            grid=(m // tm, n_tiles),
            in_specs=in_specs,
            out_specs=pl.BlockSpec((tm, tn), lambda i, j, src: (i, j)),
            scratch_shapes=[pltpu.VMEM((tm, k), BF16)]),
        out_shape=jax.ShapeDtypeStruct((m, n_tiles * tn), F32),
        compiler_params=_cparams(2),
        name="in_proj",
    )(*args)


def _head_sum(x, sel, sel_t):
    s = _dot_lhs_split(x, sel)
    return _dot_lhs_split(s, sel_t)


def _token_shift(p, carry_ref, mu):
    rows = p.shape[0]
    row = lax.broadcasted_iota(jnp.int32, p.shape, 0)
    prev = jnp.where(row == 0, carry_ref[...], pltpu.roll(p, 1, axis=0))
    carry_ref[...] = p[rows - 1:rows, :]
    return p + (prev - p) * mu


def _prep_body(*refs, tiles_per_seq, has_vres, emit_v, width):
    (rkv_ref, lora_ref, mu_rkv_ref, mu_lora_ref, w0_ref, a0_ref, kk_ref, ka_ref, rk_ref,
     dup_ref, iup_ref, gup_ref, sel_ref, selt_ref, tri_ref, ones_ref) = refs[:16]
    pos = 16
    if has_vres:
        vfirst_ref, v0_ref, vup_ref = refs[pos:pos + 3]
        pos += 3
    at_o, rt_o, bt_o, kt_o, bp_o, kp_o, vb_o, wc_o, bonus_o, g_o = refs[pos:pos + 10]
    pos += 10
    if emit_v:
        v_o = refs[pos]
        pos += 1
    carry_rkv, carry_lora = refs[pos:]
    R = width

    @pl.when(pl.program_id(0) % tiles_per_seq == 0)
    def _():
        carry_rkv[...] = jnp.zeros_like(carry_rkv)
        carry_lora[...] = jnp.zeros_like(carry_lora)

    z = _token_shift(rkv_ref[...], carry_rkv, mu_rkv_ref[...])
    zl = _token_shift(lora_ref[...], carry_lora, mu_lora_ref[...])
    r, k, v = z[:, :R], z[:, R:2 * R], z[:, 2 * R:]

    decay_logit = w0_ref[...] + _dot_x3_presplit(jnp.tanh(zl[:, :128]), dup_ref[...])
    neg = -decay_logit
    softplus = jnp.maximum(neg, 0.0) + jnp.log(1.0 + jnp.exp(-jnp.abs(neg)))
    lw = -jnp.exp(-softplus - 0.5)
    a = _sigmoid(a0_ref[...] + _dot_x3_presplit(zl[:, :256], iup_ref[...]))
    g_o[...] = _dot(_sigmoid(zl[:, 128:]), gup_ref[...])
    if has_vres:
        mix = _sigmoid(v0_ref[...] + _dot(zl[:, 384:], vup_ref[...]))
        v = v + (vfirst_ref[...] - v) * mix
    if emit_v:
        v_o[...] = v

    sel, selt = sel_ref[...], selt_ref[...]
    kk = k * kk_ref[...]
    norm = jnp.sqrt(_head_sum(kk * kk, sel, selt))
    kk = kk / jnp.maximum(norm, 1e-12)
    k = k * (1.0 + (a - 1.0) * ka_ref[...])
    bonus_o[...] = _head_sum(r * k * rk_ref[...], sel, selt) * v

    d = functools.partial(jnp.dot, preferred_element_type=F32)
    lw_hi = lw.astype(BF16)
    rest = lw - lw_hi.astype(F32)
    lw_mid = rest.astype(BF16)
    lw_lo = (rest - lw_mid.astype(F32)).astype(BF16)
    tri, ones = tri_ref[...], ones_ref[...]
    cum = d(tri, lw_hi) + d(tri, lw_mid) + d(tri, lw_lo)
    total = d(ones, lw_hi) + d(ones, lw_mid) + d(ones, lw_lo)
    w_inv = jnp.exp(-cum)
    w_rem = jnp.exp(total - cum)
    b = kk * a
    wc_o[...] = jnp.exp(total)
    pairs = R // LANES
    for o_ref, val in ((at_o, -kk * jnp.exp(cum - lw)), (rt_o, r * jnp.exp(cum)), (bt_o, b * w_inv),
                       (kt_o, k * w_inv), (bp_o, b * w_rem), (kp_o, k * w_rem), (vb_o, v)):
        val = val.astype(BF16)
        for p in range(pairs):
            o_ref[0, p] = val[:, p * LANES:(p + 1) * LANES]


def _rwkv_prep(p_all, bsz, seq, width, consts, vres, emit_v):
    tm = PREP_ROWS
    m = p_all.shape[0]
    R = width
    lora_w = 512
    lora_blk = (p_all.shape[1] - lora_w) // lora_w
    tiles_per_seq = seq // tm
    row = lambda i: (i, 0)
    fixed = lambda i: (0, 0)
    in_specs = [pl.BlockSpec((tm, 3 * R), row),
                pl.BlockSpec((tm, lora_w), lambda i: (i, lora_blk))]
    args = [p_all, p_all]
    for c in consts:
        in_specs.append(pl.BlockSpec(c.shape, fixed))
        args.append(c)
    if vres is not None:
        vfirst, v0, vup = vres
        in_specs += [pl.BlockSpec((tm, R), row), pl.BlockSpec(v0.shape, fixed), pl.BlockSpec(vup.shape, fixed)]
        args += [vfirst, v0, vup]
    pairs = R // LANES
    pm_spec = pl.BlockSpec((1, pairs, tm, LANES), lambda i: (i // tiles_per_seq, 0, i % tiles_per_seq, 0))
    pm_shape = jax.ShapeDtypeStruct((bsz, pairs, seq, LANES), BF16)
    nat_spec = pl.BlockSpec((tm, R), row)
    nat_shape = jax.ShapeDtypeStruct((m, R), F32)
    n_nat = 4 if emit_v else 3
    return pl.pallas_call(
        functools.partial(_prep_body, tiles_per_seq=tiles_per_seq, has_vres=vres is not None, emit_v=emit_v,
                          width=R),
        grid=(m // tm,),
        in_specs=in_specs,
        out_specs=[pm_spec] * 7 + [nat_spec] * n_nat,
        out_shape=[pm_shape] * 7 + [nat_shape] * n_nat,
        scratch_shapes=[pltpu.VMEM((1, 3 * R), F32), pltpu.VMEM((1, lora_w), F32)],
        compiler_params=_cparams(1),
        name="rwkv_prep",
    )(*args)


def _wkv_body(at_ref, rt_ref, bt_ref, kt_ref, bp_ref, kp_ref, v_ref, wc_ref, y_ref, state_ref, *, pairs, chunks):
    C = WKV_CHUNK
    hs = HEAD_SIZE

    @pl.when(pl.program_id(1) == 0)
    def _():
        state_ref[...] = jnp.zeros_like(state_ref)

    def head_masks(rows, width):
        lane = lax.broadcasted_iota(jnp.int32, (rows, width), 1)
        first = (lane % LANES) < hs
        return first, jnp.logical_not(first)

    def stack_heads(x):
        m0, m1 = head_masks(*x.shape)
        zero = jnp.zeros_like(x)
        return jnp.concatenate([jnp.where(m0, x, zero), jnp.where(m1, x, zero)], axis=0)

    row = lax.broadcasted_iota(jnp.int32, (C, LANES), 0)
    col = lax.broadcasted_iota(jnp.int32, (C, LANES), 1) % hs
    strict, incl = row > col, row >= col
    sq_r = lax.broadcasted_iota(jnp.int32, (LANES, LANES), 0)
    sq_c = lax.broadcasted_iota(jnp.int32, (LANES, LANES), 1)
    same_head = (sq_r // hs) == (sq_c // hs)
    eye = sq_r == sq_c
    levels = C.bit_length() - 1
    dot = functools.partial(jnp.dot, preferred_element_type=F32)
    P = range(pairs)

    def step(c, carry):
        sl = pl.ds(pl.multiple_of(c * C, C), C)
        at = [at_ref[0, p, sl, :] for p in P]
        rt = [rt_ref[0, p, sl, :] for p in P]
        v = [v_ref[0, p, sl, :] for p in P]
        vs = [stack_heads(v[p]) for p in P]
        lhs = [jnp.concatenate([at[p], rt[p]], axis=0) for p in P]
        rhs = [jnp.concatenate([stack_heads(bt_ref[0, p, sl, :]), stack_heads(kt_ref[0, p, sl, :])], axis=0)
               for p in P]
        abk = [_dot_nt(lhs[p], rhs[p]) for p in P]
        n = [jnp.where(strict, abk[p][:C, :LANES], 0.0) for p in P]
        a_rb = [jnp.where(incl, abk[p][C:, :LANES], 0.0) for p in P]
        a_ak = [jnp.where(strict, abk[p][:C, LANES:], 0.0) for p in P]
        a_rk = [jnp.where(incl, abk[p][C:, LANES:], 0.0) for p in P]
        av = [dot(a_ak[p].astype(BF16), vs[p]) for p in P]
        x = [jnp.concatenate([at[p].astype(F32), av[p]], axis=1) for p in P]
        for lvl in range(levels):
            last = lvl + 1 == levels
            for p in P:
                z = x[p] if last else jnp.concatenate([n[p], x[p]], axis=1)
                out = dot(n[p].astype(BF16), stack_heads(z.astype(BF16)))
                if last:
                    x[p] = x[p] + out
                else:
                    n[p] = out[:, :LANES]
                    x[p] = x[p] + out[:, LANES:]
        xb = [x[p].astype(BF16) for p in P]
        zeros_c = jnp.zeros((C, LANES), BF16)
        bx = [_dot_tn(jnp.concatenate([bp_ref[0, p, sl, :], kp_ref[0, p, sl, :]], axis=0),
                      jnp.concatenate([xb[p], jnp.concatenate([zeros_c, v[p]], axis=1)], axis=0)) for p in P]
        zeros_2c = jnp.zeros((2 * C, LANES), BF16)
        rx = [dot(jnp.concatenate([a_rb[p], a_rk[p]], axis=1).astype(BF16),
                  jnp.concatenate([stack_heads(xb[p]), jnp.concatenate([zeros_2c, vs[p]], axis=1)], axis=0))
              for p in P]
        for p in P:
            state = state_ref[p]
            wc = wc_ref[pl.ds(pl.multiple_of(c * C, C), 1), p * LANES:(p + 1) * LANES]
            trans = jnp.where(same_head, bx[p][:, :LANES], 0.0) + jnp.where(eye, wc, 0.0)
            inject = jnp.where(same_head, bx[p][:, LANES:], 0.0)
            r_eff = rt[p].astype(F32) + rx[p][:, :LANES]
            y_ref[0, p, sl, :] = _dot(r_eff, state) + rx[p][:, LANES:]
            state_ref[p] = _dot_x3(trans, state) + inject
        return carry

    lax.fori_loop(0, chunks, step, 0)


def _wkv(at, rt, bt, kt, bp, kp, v, wc, pairs=8, span=512):
    bsz, n_pairs, seq, _ = at.shape
    groups = n_pairs // pairs
    spec = pl.BlockSpec((1, pairs, span, LANES), lambda g, t: (g // groups, g % groups, t, 0))
    wc_spec = pl.BlockSpec((span, pairs * LANES), lambda g, t: ((g // groups) * (seq // span) + t, g % groups))
    return pl.pallas_call(
        functools.partial(_wkv_body, pairs=pairs, chunks=span // WKV_CHUNK),
        grid=(bsz * groups, seq // span),
        in_specs=[spec] * 7 + [wc_spec],
        out_specs=spec,
        out_shape=jax.ShapeDtypeStruct((bsz, n_pairs, seq, LANES), F32),
        scratch_shapes=[pltpu.VMEM((pairs, LANES, LANES), F32)],
        compiler_params=_cparams(2),
        name="wkv7",
    )(at, rt, bt, kt, bp, kp, v, wc)


def _post_pool_body(y_ref, bonus_ref, g_ref, u_ref, lnw_ref, lnb_ref, sel_ref, selt_ref,
                    pw_ref, ps_ref, ya_o, yb_o, ext_ref, *, tiles_per_seq, tm, group_width):
    sel, selt = sel_ref[...], selt_ref[...]
    inv_n = 1.0 / HEAD_SIZE
    y = jnp.concatenate([y_ref[0, p] for p in range(y_ref.shape[1])], axis=1)
    mean = _head_sum(y, sel, selt) * inv_n
    yc = y - mean
    var = _head_sum(yc * yc, sel, selt) * inv_n
    yn = yc * lax.rsqrt(var + GN_EPS) * lnw_ref[...] + lnb_ref[...]
    ya_o[...] = ((yn + bonus_ref[...]) * g_ref[...]).astype(ya_o.dtype)

    seq_tile = pl.program_id(0) % tiles_per_seq

    @pl.when(seq_tile == 0)
    def _():
        ext_ref[0:POOL_HALO, :] = jnp.zeros((POOL_HALO, ext_ref.shape[1]), F32)

    u = u_ref[...]
    ext_ref[POOL_HALO:POOL_HALO + tm, :] = u
    t_pos = seq_tile * tm + lax.broadcasted_iota(jnp.int32, (tm, group_width), 0)
    for gi, w in enumerate(POOL_WINDOWS):
        lanes = slice(gi * group_width, (gi + 1) * group_width)
        acc = u[:, lanes]
        for j in range(1, w):
            acc = acc + ext_ref[POOL_HALO - j:POOL_HALO - j + tm, lanes]
        count = jnp.minimum(t_pos + 1, w).astype(F32)
        mixed = acc / count - u[:, lanes]
        yg = _dot_x3(mixed, pw_ref[gi])
        yb_o[:, lanes] = (yg * ps_ref[:, lanes]).astype(yb_o.dtype)
    ext_ref[0:POOL_HALO, :] = u[tm - POOL_HALO:, :]


def _post_pool(y, bonus, g, p_all, pool_blk, consts, seq, tm=256):
    m, R = bonus.shape
    lnw, lnb, sel, selt, pw, ps = consts
    tiles_per_seq = seq // tm
    row = lambda i: (i, 0)
    fixed2 = lambda i: (0, 0)
    in_specs = [pl.BlockSpec((1, y.shape[1], tm, LANES), lambda i: (i // tiles_per_seq, 0, i % tiles_per_seq, 0)),
                pl.BlockSpec((tm, R), row), pl.BlockSpec((tm, R), row),
                pl.BlockSpec((tm, R), lambda i: (i, pool_blk))]
    in_specs += [pl.BlockSpec(c.shape, fixed2) for c in (lnw, lnb, sel, selt)]
    in_specs += [pl.BlockSpec(pw.shape, lambda i: (0, 0, 0)), pl.BlockSpec(ps.shape, fixed2)]
    out = jax.ShapeDtypeStruct((m, R), BF16)
    return pl.pallas_call(
        functools.partial(_post_pool_body, tiles_per_seq=tiles_per_seq, tm=tm, group_width=R // len(POOL_WINDOWS)),
        grid=(m // tm,),
        in_specs=in_specs,
        out_specs=[pl.BlockSpec((tm, R), row)] * 2,
        out_shape=[out, out],
        scratch_shapes=[pltpu.VMEM((POOL_HALO + tm, R), F32)],
        compiler_params=_cparams(1),
        name="post_pool",
    )(y, bonus, g, p_all, lnw, lnb, sel, selt, pw, ps)


def _mix_out_body(ya_ref, yb_ref, ga_ref, gb_ref, x_ref, pa_ref, pb_ref, wo_ref, o_ref):
    ma = jnp.dot(ya_ref[...], pa_ref[...], preferred_element_type=F32)
    mb = jnp.dot(yb_ref[...], pb_ref[...], preferred_element_type=F32)
    merged = (_sigmoid(ga_ref[...]) * ma + _sigmoid(gb_ref[...]) * mb).astype(BF16)
    o_ref[...] = x_ref[...] + jnp.dot(merged, wo_ref[...], preferred_element_type=F32)


def _mix_out(ya, yb, p_all, ga_col, gb_col, x, pa, pb, wo, tm=256):
    m, R = ya.shape
    d = wo.shape[1]
    row = lambda i: (i, 0)
    resident = lambda shape: pl.BlockSpec(shape, lambda i: (0, 0), pipeline_mode=pl.Buffered(1))
    return pl.pallas_call(
        _mix_out_body,
        grid=(m // tm,),
        in_specs=[pl.BlockSpec((tm, R), row), pl.BlockSpec((tm, R), row),
                  pl.BlockSpec((tm, d), lambda i: (i, ga_col // d)),
                  pl.BlockSpec((tm, d), lambda i: (i, gb_col // d)),
                  pl.BlockSpec((tm, d), row),
                  resident(pa.shape), resident(pb.shape), resident(wo.shape)],
        out_specs=pl.BlockSpec((tm, d), row),
        out_shape=jax.ShapeDtypeStruct((m, d), F32),
        compiler_params=_cparams(1),
        name="mix_out",
    )(ya, yb, p_all, p_all, x, pa, pb, wo)


def _router_body(x_ref, g_ref, wr_ref, tri_ref, h_o, idx_o, gate_o, cnt_o, carry_ref):
    @pl.when(pl.program_id(0) == 0)
    def _():
        carry_ref[...] = jnp.zeros_like(carry_ref)

    x = x_ref[...]
    inv = lax.rsqrt(jnp.mean(x * x, axis=-1, keepdims=True) + NORM_EPS)
    h = x * inv * g_ref[...]
    h_o[...] = h.reshape(h_o.shape)
    logits = _dot_x3_presplit(h, wr_ref[...])
    lane = lax.broadcasted_iota(jnp.int32, logits.shape, 1)
    neg_inf = jnp.float32(-jnp.inf)
    l1 = jnp.where(lane < N_EXPERTS, logits, neg_inf)
    m1 = jnp.max(l1, axis=-1, keepdims=True)
    i1 = jnp.min(jnp.where(l1 == m1, lane, LANES), axis=-1, keepdims=True)
    l2 = jnp.where(lane == i1, neg_inf, l1)
    m2 = jnp.max(l2, axis=-1, keepdims=True)
    i2 = jnp.min(jnp.where(l2 == m2, lane, LANES), axis=-1, keepdims=True)
    e = jnp.exp(m2 - m1)
    g1 = 1.0 / (1.0 + e)
    gate_o[...] = jnp.where(lane == 0, g1, jnp.where(lane == 1, e * g1, 0.0))

    first, second = lane == i1, lane == i2
    picked = jnp.where(jnp.logical_or(first, second), 1.0, 0.0)
    before = carry_ref[...] + jnp.dot(tri_ref[...], picked.astype(BF16), preferred_element_type=F32)
    rank1 = jnp.sum(jnp.where(first, before, 0.0), axis=-1, keepdims=True).astype(jnp.int32)
    rank2 = jnp.sum(jnp.where(second, before, 0.0), axis=-1, keepdims=True).astype(jnp.int32)
    idx_o[...] = jnp.where(lane == 0, i1, jnp.where(lane == 1, i2, jnp.where(lane == 2, rank1,
                                                                              jnp.where(lane == 3, rank2, 0))))
    carry_ref[...] += jnp.sum(picked, axis=0, keepdims=True)
    cnt_o[...] = jnp.broadcast_to(carry_ref[...], cnt_o.shape).astype(jnp.int32)


def _router(x, g, wr_pad, tm=256):
    m, d = x.shape
    row = lambda i: (i, 0)
    t_idx = jnp.arange(tm, dtype=jnp.int32)
    tri_strict = (t_idx[None, :] < t_idx[:, None]).astype(BF16)
    return pl.pallas_call(
        _router_body,
        grid=(m // tm,),
        in_specs=[pl.BlockSpec((tm, d), row), pl.BlockSpec((1, d), lambda i: (0, 0)),
                  pl.BlockSpec(wr_pad.shape, lambda i: (0, 0)), pl.BlockSpec((tm, tm), lambda i: (0, 0))],
        out_specs=[pl.BlockSpec((tm, d // LANES, LANES), lambda i: (i, 0, 0)),
                   pl.BlockSpec((tm, LANES), row), pl.BlockSpec((tm, LANES), row),
                   pl.BlockSpec((8, LANES), lambda i: (0, 0))],
        out_shape=[jax.ShapeDtypeStruct((m, d // LANES, LANES), F32), jax.ShapeDtypeStruct((m, LANES), jnp.int32),
                   jax.ShapeDtypeStruct((m, LANES), F32), jax.ShapeDtypeStruct((8, LANES), jnp.int32)],
        scratch_shapes=[pltpu.VMEM((1, LANES), F32)],
        compiler_params=_cparams(1),
        name="router",
    )(x, g.reshape(1, d), wr_pad, tri_strict)


def _gather_body(tok_ref, used_ref, h_hbm, o_ref, buf_ref, sems, *, tg, n_steps):
    i = pl.program_id(0)

    def row_copy(step, r):
        slot = step % 2
        return pltpu.make_async_copy(h_hbm.at[pl.ds(tok_ref[step * tg + r], 1)], buf_ref.at[slot, pl.ds(r, 1)],
                                     sems.at[slot])

    def fetch(step):
        @pl.when(jnp.logical_and(step < n_steps, step * tg < used_ref[0]))
        def _():
            def start(q, c):
                for u in range(DMA_UNROLL):
                    row_copy(step, q * DMA_UNROLL + u).start()
                return c
            lax.fori_loop(0, tg // DMA_UNROLL, start, 0)

    @pl.when(i == 0)
    def _():
        fetch(i)

    fetch(i + 1)

    @pl.when(i * tg < used_ref[0])
    def _():
        pltpu.make_async_copy(h_hbm.at[pl.ds(0, tg)], buf_ref.at[i % 2], sems.at[i % 2]).wait()
        o_ref[...] = buf_ref[i % 2].reshape(o_ref.shape).astype(o_ref.dtype)

    @pl.when(i * tg >= used_ref[0])
    def _():
        o_ref[...] = jnp.zeros_like(o_ref)


def _gather_rows(tok, used_rows, h3, tg=256):
    p = tok.shape[0]
    _, nc, lanes = h3.shape
    n_steps = p // tg
    return pl.pallas_call(
        functools.partial(_gather_body, tg=tg, n_steps=n_steps),
        grid_spec=pltpu.PrefetchScalarGridSpec(
            num_scalar_prefetch=2,
            grid=(n_steps,),
            in_specs=[pl.BlockSpec(memory_space=pl.ANY)],
            out_specs=pl.BlockSpec((tg, nc * lanes), lambda i, tok, used: (i, 0)),
            scratch_shapes=[pltpu.VMEM((2, tg, nc, lanes), F32), pltpu.SemaphoreType.DMA((2,))]),
        out_shape=jax.ShapeDtypeStruct((p, nc * lanes), BF16),
        compiler_params=_cparams(1),
        name="moe_gather",
    )(tok, used_rows, h3)


DMA_UNROLL = 8
MOE_ROWS = 1024
MOE_SUB = 256
MOE_FF_TILE = 256
PLAN_FIELDS = 7


def _experts_body(plan_ref, x_ref, wg_ref, wu_ref, wd_ref, o_ref, wgb_ref, wub_ref, wdb_ref, *, n_items):
    w = pl.program_id(0)
    j = pl.program_id(1)
    field = lambda f: plan_ref[f * n_items + w]
    lo, hi, zero, live = field(3), field(4), field(5), field(6)
    n_sub = MOE_ROWS // MOE_SUB

    @pl.when(jnp.logical_and(zero == 1, j == 0))
    def _():
        o_ref[...] = jnp.zeros_like(o_ref)

    def rows(sl, wg, wu, wd):
        x = x_ref[sl, :]
        g = jnp.dot(x, wg, preferred_element_type=F32)
        u = jnp.dot(x, wu, preferred_element_type=F32)
        h = (g * _sigmoid(g) * u).astype(BF16)
        o_ref[sl, :] += jnp.dot(h, wd, preferred_element_type=F32)

    whole = jnp.logical_and(live == 1, jnp.logical_and(lo == 0, hi == n_sub))
    partial = jnp.logical_and(live == 1, jnp.logical_not(whole))

    @pl.when(whole)
    def _():
        rows(slice(None), wg_ref[...].astype(BF16), wu_ref[...].astype(BF16), wd_ref[...].astype(BF16))

    @pl.when(partial)
    def _():
        wgb_ref[...] = wg_ref[...].astype(BF16)
        wub_ref[...] = wu_ref[...].astype(BF16)
        wdb_ref[...] = wd_ref[...].astype(BF16)

    for s in range(n_sub):
        @pl.when(jnp.logical_and(partial, jnp.logical_and(lo <= s, s < hi)))
        def _():
            rows(slice(s * MOE_SUB, (s + 1) * MOE_SUB), wgb_ref[...], wub_ref[...], wdb_ref[...])


def _experts(plan, xs, w_gate, w_up, w_down):
    p, d = xs.shape
    f = w_gate.shape[2]
    tn = MOE_FF_TILE
    nj = f // tn
    n_items = plan.shape[0] // PLAN_FIELDS
    fld = lambda plan, f_idx, w: plan[f_idx * n_items + w]
    col = lambda plan, w, j: jnp.where(fld(plan, 6, w) == 1, j, nj - 1)
    return pl.pallas_call(
        functools.partial(_experts_body, n_items=n_items),
        grid_spec=pltpu.PrefetchScalarGridSpec(
            num_scalar_prefetch=1,
            grid=(n_items, nj),
            in_specs=[pl.BlockSpec((MOE_ROWS, d), lambda w, j, plan: (fld(plan, 1, w), 0)),
                      pl.BlockSpec((None, d, tn), lambda w, j, plan: (fld(plan, 2, w), 0, col(plan, w, j))),
                      pl.BlockSpec((None, d, tn), lambda w, j, plan: (fld(plan, 2, w), 0, col(plan, w, j))),
                      pl.BlockSpec((None, tn, d), lambda w, j, plan: (fld(plan, 2, w), col(plan, w, j), 0))],
            out_specs=pl.BlockSpec((MOE_ROWS, d), lambda w, j, plan: (fld(plan, 0, w), 0)),
            scratch_shapes=[pltpu.VMEM((d, tn), BF16), pltpu.VMEM((d, tn), BF16), pltpu.VMEM((tn, d), BF16)]),
        out_shape=jax.ShapeDtypeStruct((p, d), F32),
        compiler_params=_cparams(2),
        name="moe_experts",
    )(plan, xs, w_gate, w_up, w_down)


def _combine_body(d0_ref, d1_ref, x_ref, gate_ref, g_ref, y_hbm, o_ref, buf_ref, sems, *, tg, n_steps, apply_norm):
    i = pl.program_id(0)

    def row_copy(step, r, d_ref, pick):
        slot = step % 2
        return pltpu.make_async_copy(y_hbm.at[pl.ds(d_ref[step * tg + r], 1), :],
                                     buf_ref.at[slot, pick, pl.ds(r, 1), :], sems.at[slot])

    def fetch(step):
        @pl.when(step < n_steps)
        def _():
            def start(q, c):
                for u in range(DMA_UNROLL):
                    row_copy(step, q * DMA_UNROLL + u, d0_ref, 0).start()
                    row_copy(step, q * DMA_UNROLL + u, d1_ref, 1).start()
                return c
            lax.fori_loop(0, tg // DMA_UNROLL, start, 0)

    @pl.when(i == 0)
    def _():
        fetch(i)

    fetch(i + 1)
    slot = i % 2
    for pick in range(2):
        pltpu.make_async_copy(y_hbm.at[pl.ds(0, tg), :], buf_ref.at[slot, pick], sems.at[slot]).wait()
    gate = gate_ref[...]
    x = x_ref[...] + gate[:, 0:1] * buf_ref[slot, 0] + gate[:, 1:2] * buf_ref[slot, 1]
    if apply_norm:
        x = x * lax.rsqrt(jnp.mean(x * x, axis=-1, keepdims=True) + NORM_EPS) * g_ref[...]
    o_ref[...] = x


def _combine(d0, d1, x, gate, g, ybuf, apply_norm, tg=256):
    m, d = x.shape
    row = lambda i, d0, d1: (i, 0)
    return pl.pallas_call(
        functools.partial(_combine_body, tg=tg, n_steps=m // tg, apply_norm=apply_norm),
        grid_spec=pltpu.PrefetchScalarGridSpec(
            num_scalar_prefetch=2,
            grid=(m // tg,),
            in_specs=[pl.BlockSpec((tg, d), row), pl.BlockSpec((tg, LANES), row),
                      pl.BlockSpec((1, d), lambda i, d0, d1: (0, 0)),
                      pl.BlockSpec(memory_space=pl.ANY)],
            out_specs=pl.BlockSpec((tg, d), row),
            scratch_shapes=[pltpu.VMEM((2, 2, tg, d), F32), pltpu.SemaphoreType.DMA((2,))]),
        out_shape=jax.ShapeDtypeStruct((m, d), F32),
        compiler_params=_cparams(1),
        name="moe_combine",
    )(d0, d1, x, gate, g.reshape(1, d), ybuf)


def _moe_plan(top_e, rank, counts):
    i32 = jnp.int32
    n_tok = top_e.shape[0]
    n_slot = n_tok * TOP_K
    flat_e = top_e.reshape(n_slot)
    rank = rank.reshape(n_slot)
    padded = (counts + MOE_SUB - 1) // MOE_SUB * MOE_SUB
    pend = jnp.cumsum(padded)
    pstart = pend - padded
    dest = pstart[flat_e] + rank
    n_blocks = -(-(n_slot + N_EXPERTS * MOE_SUB) // MOE_ROWS)
    tok = jnp.zeros((n_blocks * MOE_ROWS,), i32).at[dest].set(jnp.arange(n_slot, dtype=i32) // TOP_K)
    used_rows = pend[-1]
    used_blocks = (used_rows + MOE_ROWS - 1) // MOE_ROWS

    n_items = n_blocks + N_EXPERTS
    blk_lo = (jnp.arange(n_blocks, dtype=i32) * MOE_ROWS)[:, None]
    lo_row = jnp.maximum(pstart[None, :], blk_lo)
    hi_row = jnp.minimum(pend[None, :], blk_lo + MOE_ROWS)
    valid = (hi_row > lo_row).reshape(-1)
    n_live = jnp.sum(valid.astype(i32))
    pos = jnp.where(valid, jnp.cumsum(valid.astype(i32)) - 1, n_items)
    compact = lambda v: jnp.zeros((n_items,), i32).at[pos].set(v.reshape(-1).astype(i32), mode="drop")
    blk_c = compact(jnp.broadcast_to(jnp.arange(n_blocks, dtype=i32)[:, None], lo_row.shape))
    exp_c = compact(jnp.broadcast_to(jnp.arange(N_EXPERTS, dtype=i32)[None, :], lo_row.shape))
    lo_c = compact((lo_row - blk_lo) // MOE_SUB)
    hi_c = compact((hi_row - blk_lo) // MOE_SUB)
    first_c = compact(lo_row == blk_lo)

    w = jnp.arange(n_items, dtype=i32)
    live = w < n_live
    tail_blk = used_blocks + (w - n_live)
    last = n_live - 1
    out_blk = jnp.where(live, blk_c, jnp.minimum(tail_blk, n_blocks - 1))
    x_blk = jnp.where(live, blk_c, blk_c[last])
    expert = jnp.where(live, exp_c, exp_c[last])
    zero = jnp.where(live, first_c, (tail_blk < n_blocks).astype(i32))
    plan = jnp.concatenate([out_blk, x_blk, expert, lo_c, hi_c, zero, live.astype(i32)]).astype(i32)
    dest = dest.reshape(n_tok, TOP_K).astype(i32)
    return tok, used_rows.astype(i32).reshape(1), plan, dest[:, 0], dest[:, 1]


def kernel(x, norm_mix, w_in, shift_mu, decay_w0, decay_up, iclr_a0, iclr_up, outgate_up, k_k, k_a, r_k, lnx_w, lnx_b, vres_down, vres_mu, vres_up, vres_v0, pool_w, pool_scale, proj_a, proj_b, w_o, norm_ffn, ffn_gate, ffn_up, ffn_down, router, moe_gate, moe_up, moe_down, norm_final):
    bsz, seq, d = x.shape
    depth = w_in.shape[0]
    R = decay_w0.shape[1]
    dl, il, gl = decay_up.shape[1], iclr_up.shape[1], outgate_up.shape[1]
    vl = vres_up.shape[1]
    assert (dl, il, vl, gl) == (96, 96, 64, 256) and 3 * R + dl + il + gl == shift_mu.shape[1]
    m = bsz * seq
    x = x.reshape(m, d)
    row = lambda t: t.reshape(1, -1)

    head_of_lane = jnp.arange(R, dtype=jnp.int32) // HEAD_SIZE
    sel = (head_of_lane[:, None] == jnp.arange(LANES, dtype=jnp.int32)[None, :]).astype(BF16)
    sel_t = sel.T
    t_idx = jnp.arange(PREP_ROWS, dtype=jnp.int32)
    same_chunk = (t_idx[:, None] // WKV_CHUNK) == (t_idx[None, :] // WKV_CHUNK)
    ones_bd = same_chunk.astype(BF16)
    tri_bd = (same_chunk & (t_idx[None, :] <= t_idx[:, None])).astype(BF16)

    w_in_t = jnp.swapaxes(w_in, 1, 2)
    v_first = None
    for l in range(depth):
        c0 = 3 * R
        c3 = c0 + dl + il + gl
        tn = 512
        lora_w = c3 - c0 + vl
        assert c0 % tn == 0 and (w_in.shape[2] - c3) % tn == 0 and lora_w == tn
        src_rows = list(range(0, c0, tn)) + list(range(c3, w_in.shape[2], tn)) + [c0]
        vd_mu = vres_mu[l - 1] if l > 0 else jnp.zeros((vl,), F32)
        extra = None
        if l > 0:
            extra = jnp.concatenate([jnp.zeros((d, c3 - c0), F32), vres_down[l - 1]], axis=1).astype(BF16)
        mu = shift_mu[l]
        mu_rkv = row(mu[:c0])
        mu_lora = row(jnp.concatenate([mu[c0:c3], vd_mu]))
        pool_col = c0
        ga_col = c0 + R
        gb_col = ga_col + d
        zeros = lambda n: jnp.zeros((n, R), F32)
        dup = _hi_lo_rows(jnp.concatenate([decay_up[l], zeros(128 - dl)], axis=0))
        iup = _hi_lo_rows(jnp.concatenate([zeros(dl), iclr_up[l], zeros(256 - dl - il)], axis=0))
        gup = jnp.concatenate([zeros(dl + il - 128), outgate_up[l], zeros(vl)], axis=0).astype(BF16)

        p_all = _in_proj(x, norm_mix[l], w_in_t, l, src_rows, extra, c3 - c0, tn=tn)

        consts = [mu_rkv, mu_lora, row(decay_w0[l]), row(iclr_a0[l]), row(k_k[l]), row(k_a[l]), row(r_k[l]),
                  dup, iup, gup, sel, sel_t, tri_bd, ones_bd]
        vres = None
        if l > 0:
            vup = jnp.concatenate([zeros(128 - vl), vres_up[l - 1]], axis=0).astype(BF16)
            vres = (v_first, row(vres_v0[l - 1]), vup)
        emit_v = l == 0 and depth > 1
        outs = _rwkv_prep(p_all, bsz, seq, R, consts, vres, emit_v)
        wc, bonus, g = outs[7:10]
        if emit_v:
            v_first = outs[10]
        y = _wkv(*outs[:7], wc)

        ya, yb = _post_pool(y, bonus, g, p_all, pool_col // R,
                            [row(lnx_w[l]), row(lnx_b[l]), sel, sel_t, pool_w[l], row(pool_scale[l])], seq)
        assert ga_col % d == 0 and gb_col % d == 0
        x = _mix_out(ya, yb, p_all, ga_col, gb_col, x,
                     proj_a[l].astype(BF16), proj_b[l].astype(BF16), w_o[l].astype(BF16))

        i = l // 2
        if l % 2 == 0:
            hidden = _norm_matmul(x, norm_ffn[l], [ffn_gate, ffn_up], tm=1024, tn=512, out_dtype=BF16, layer=i)
            x = _matmul(hidden, [ffn_down], tm=512, tn=512, out_dtype=F32, residual=x, layer=i)
            if l == depth - 1:
                x = _rmsnorm(x, norm_final, F32)
        else:
            wr_pad = _hi_lo_rows(jnp.concatenate([router[i], jnp.zeros((d, LANES - N_EXPERTS), F32)], axis=1))
            h3, idx, gate, counts = _router(x, norm_ffn[l], wr_pad)
            tok, used_rows, plan, d0, d1 = _moe_plan(idx[:, :TOP_K], idx[:, TOP_K:2 * TOP_K], counts[0, :N_EXPERTS])
            xs = _gather_rows(tok, used_rows, h3)
            ybuf = _experts(plan, xs, moe_gate[i], moe_up[i], moe_down[i])
            x = _combine(d0, d1, x, gate, norm_final, ybuf, apply_norm=l == depth - 1)
    return x.reshape(bsz, seq, d)
```

```python
import functools

import jax
import jax.numpy as jnp
from jax import lax
from jax.experimental import pallas as pl
from jax.experimental.pallas import tpu as pltpu

F32 = jnp.float32
BF16 = jnp.bfloat16

HEAD_SIZE = 64
POOL_WINDOWS = (2, 4, 8, 16)
N_EXPERTS = 8
TOP_K = 2
NORM_EPS = 1e-6
GN_EPS = 64e-5
LANES = 128
VMEM_LIMIT_BYTES = 60 * 1024 * 1024
WKV_CHUNK = 64
POOL_HALO = 16
PREP_ROWS = 256
IN_PROJ_ROW_UNIT = 32
DMA_UNROLL = 8
MOE_ROWS = 1024
MOE_SUB = 256
MOE_FF_TILE = 256
PLAN_FIELDS = 7


def _cparams(n_axes):
    return pltpu.CompilerParams(dimension_semantics=("arbitrary",) * n_axes,
                                vmem_limit_bytes=VMEM_LIMIT_BYTES)


def _dot(a, b):
    return jnp.dot(a.astype(BF16), b.astype(BF16), preferred_element_type=F32)


def _split(x):
    hi = x.astype(BF16)
    lo = (x - hi.astype(F32)).astype(BF16)
    return hi, lo


def _dot_x3(a, b):
    a_hi, a_lo = _split(a)
    b_hi, b_lo = _split(b)
    d = functools.partial(jnp.dot, preferred_element_type=F32)
    return d(a_hi, b_hi) + d(a_hi, b_lo) + d(a_lo, b_hi)


def _hi_lo_rows(w):
    hi = w.astype(BF16)
    return jnp.concatenate([hi, (w - hi.astype(F32)).astype(BF16)], axis=0)


def _dot_x3_presplit(a, w_hi_lo):
    k = w_hi_lo.shape[0] // 2
    a_hi, a_lo = _split(a)
    d = functools.partial(jnp.dot, preferred_element_type=F32)
    return d(a_hi, w_hi_lo[:k]) + d(a_hi, w_hi_lo[k:]) + d(a_lo, w_hi_lo[:k])


def _dot_lhs_split(a, b_bf16):
    a_hi, a_lo = _split(a)
    d = functools.partial(jnp.dot, preferred_element_type=F32)
    return d(a_hi, b_bf16) + d(a_lo, b_bf16)


def _dot_tn(a, b):
    return lax.dot_general(a.astype(BF16), b.astype(BF16), (((0,), (0,)), ((), ())),
                           preferred_element_type=F32)


def _dot_nt(a, b):
    return lax.dot_general(a.astype(BF16), b.astype(BF16), (((1,), (1,)), ((), ())),
                           preferred_element_type=F32)


def _sigmoid(x):
    return 1.0 / (1.0 + jnp.exp(-x))


def _rmsnorm_body(x_ref, g_ref, o_ref):
    x = x_ref[...]
    inv = lax.rsqrt(jnp.mean(x * x, axis=-1, keepdims=True) + NORM_EPS)
    o_ref[...] = (x * inv * g_ref[...]).astype(o_ref.dtype)


def _rmsnorm(x, g, out_dtype, tm=512):
    m, d = x.shape
    return pl.pallas_call(
        _rmsnorm_body,
        grid=(m // tm,),
        in_specs=[pl.BlockSpec((tm, d), lambda i: (i, 0)),
                  pl.BlockSpec((1, d), lambda i: (0, 0))],
        out_specs=pl.BlockSpec((tm, d), lambda i: (i, 0)),
        out_shape=jax.ShapeDtypeStruct((m, d), out_dtype),
        compiler_params=_cparams(1),
        name="rmsnorm",
    )(x, g.reshape(1, d))


def _mm_residual_body(a_ref, w_ref, res_ref, o_ref, wb_ref):
    @pl.when(pl.program_id(1) == 0)
    def _():
        wb_ref[...] = w_ref[...].astype(BF16)

    o_ref[...] = res_ref[...] + jnp.dot(a_ref[...], wb_ref[...], preferred_element_type=F32)


def _matmul_residual(a, w, layer, residual, *, tm, tn):
    m, k = a.shape
    n = w.shape[-1]
    o_spec = pl.BlockSpec((tm, tn), lambda j, i: (i, j))
    return pl.pallas_call(
        _mm_residual_body,
        grid=(n // tn, m // tm),
        in_specs=[pl.BlockSpec((tm, k), lambda j, i: (i, 0)),
                  pl.BlockSpec((None, k, tn), lambda j, i: (layer, 0, j)),
                  o_spec],
        out_specs=o_spec,
        out_shape=jax.ShapeDtypeStruct((m, n), F32),
        scratch_shapes=[pltpu.VMEM((k, tn), BF16)],
        compiler_params=_cparams(2),
        name="matmul_residual",
    )(a, w, residual)


def _norm_swiglu_body(x_ref, g_ref, wg_ref, wu_ref, o_ref, hb_ref):
    @pl.when(pl.program_id(1) == 0)
    def _():
        x = x_ref[...]
        inv = lax.rsqrt(jnp.mean(x * x, axis=-1, keepdims=True) + NORM_EPS)
        hb_ref[...] = (x * inv * g_ref[...]).astype(BF16)

    h = hb_ref[...]
    g = jnp.dot(h, wg_ref[...].astype(BF16), preferred_element_type=F32)
    u = jnp.dot(h, wu_ref[...].astype(BF16), preferred_element_type=F32)
    o_ref[...] = (g * _sigmoid(g) * u).astype(o_ref.dtype)


def _norm_swiglu(x, gain, w_gate, w_up, layer, *, tm, tn):
    m, k = x.shape
    n = w_gate.shape[-1]
    w_spec = pl.BlockSpec((None, k, tn), lambda i, j: (layer, 0, j))
    return pl.pallas_call(
        _norm_swiglu_body,
        grid=(m // tm, n // tn),
        in_specs=[pl.BlockSpec((tm, k), lambda i, j: (i, 0)), pl.BlockSpec((1, k), lambda i, j: (0, 0)),
                  w_spec, w_spec],
        out_specs=pl.BlockSpec((tm, tn), lambda i, j: (i, j)),
        out_shape=jax.ShapeDtypeStruct((m, n), BF16),
        scratch_shapes=[pltpu.VMEM((tm, k), BF16)],
        compiler_params=_cparams(2),
        name="norm_swiglu",
    )(x, gain.reshape(1, k), w_gate, w_up)


def _in_proj_body(*refs, has_extra, n_tiles, keep_cols):
    if has_extra:
        src_ref, x_ref, g_ref, wt_ref, extra_ref, o_ref, hb_ref = refs
    else:
        src_ref, x_ref, g_ref, wt_ref, o_ref, hb_ref = refs
    j = pl.program_id(1)

    @pl.when(j == 0)
    def _():
        x = x_ref[...]
        inv = lax.rsqrt(jnp.mean(x * x, axis=-1, keepdims=True) + NORM_EPS)
        hb_ref[...] = (x * inv * g_ref[...]).astype(BF16)

    h = hb_ref[...]
    out = lax.dot_general(h, wt_ref[0].astype(BF16), (((1,), (1,)), ((), ())), preferred_element_type=F32)
    if has_extra:
        @pl.when(j < n_tiles - 1)
        def _():
            o_ref[...] = out

        @pl.when(j == n_tiles - 1)
        def _():
            col = lax.broadcasted_iota(jnp.int32, out.shape, 1)
            o_ref[...] = jnp.where(col < keep_cols, out, 0.0) + jnp.dot(h, extra_ref[...],
                                                                        preferred_element_type=F32)
    else:
        o_ref[...] = out


def _in_proj(x, gain, wt, layer, src_rows, extra, keep_cols, tm=2048, tn=512):
    m, k = x.shape
    n_tiles = len(src_rows)
    assert all(r % IN_PROJ_ROW_UNIT == 0 for r in src_rows)
    src = jnp.asarray([r // IN_PROJ_ROW_UNIT for r in src_rows], jnp.int32)
    element = lambda n: pl.Element(n)
    tm = min(tm, m)
    in_specs = [pl.BlockSpec((tm, k), lambda i, j, src: (i, 0), pipeline_mode=pl.Buffered(1)),
                pl.BlockSpec((1, k), lambda i, j, src: (0, 0)),
                pl.BlockSpec((element(1), element(tn), element(k)),
                             lambda i, j, src: (layer, src[j] * IN_PROJ_ROW_UNIT, 0))]
    args = [src, x, gain.reshape(1, k), wt]
    if extra is not None:
        in_specs.append(pl.BlockSpec((k, tn), lambda i, j, src: (0, 0), pipeline_mode=pl.Buffered(1)))
        args.append(extra)
    return pl.pallas_call(
        functools.partial(_in_proj_body, has_extra=extra is not None, n_tiles=n_tiles, keep_cols=keep_cols),
        grid_spec=pltpu.PrefetchScalarGridSpec(
            num_scalar_prefetch=1,
            grid=(m // tm, n_tiles),
            in_specs=in_specs,
            out_specs=pl.BlockSpec((tm, tn), lambda i, j, src: (i, j)),
            scratch_shapes=[pltpu.VMEM((tm, k), BF16)]),
        out_shape=jax.ShapeDtypeStruct((m, n_tiles * tn), F32),
        compiler_params=_cparams(2),
        name="in_proj",
    )(*args)


def _head_sum(x, sel, sel_t):
    s = _dot_lhs_split(x, sel)
    return _dot_lhs_split(s, sel_t)


def _token_shift(p, carry_ref, mu):
    rows = p.shape[0]
    row = lax.broadcasted_iota(jnp.int32, p.shape, 0)
    prev = jnp.where(row == 0, carry_ref[...], pltpu.roll(p, 1, axis=0))
    carry_ref[...] = p[rows - 1:rows, :]
    return p + (prev - p) * mu


def _prep_body(*refs, tiles_per_seq, has_vres, emit_v, width):
    (rkv_ref, lora_ref, mu_rkv_ref, mu_lora_ref, w0_ref, a0_ref, kk_ref, ka_ref, rk_ref,
     dup_ref, iup_ref, gup_ref, sel_ref, selt_ref, tri_ref, ones_ref) = refs[:16]
    pos = 16
    if has_vres:
        vfirst_ref, v0_ref, vup_ref = refs[pos:pos + 3]
        pos += 3
    at_o, rt_o, bt_o, kt_o, bp_o, kp_o, vb_o, wc_o, bonus_o, g_o = refs[pos:pos + 10]
    pos += 10
    if emit_v:
        v_o = refs[pos]
        pos += 1
    carry_rkv, carry_lora = refs[pos:]
    R = width

    @pl.when(pl.program_id(0) % tiles_per_seq == 0)
    def _():
        carry_rkv[...] = jnp.zeros_like(carry_rkv)
        carry_lora[...] = jnp.zeros_like(carry_lora)

    z = _token_shift(rkv_ref[...], carry_rkv, mu_rkv_ref[...])
    zl = _token_shift(lora_ref[...], carry_lora, mu_lora_ref[...])
    r, k, v = z[:, :R], z[:, R:2 * R], z[:, 2 * R:]

    decay_logit = w0_ref[...] + _dot_x3_presplit(jnp.tanh(zl[:, :128]), dup_ref[...])
    neg = -decay_logit
    softplus = jnp.maximum(neg, 0.0) + jnp.log(1.0 + jnp.exp(-jnp.abs(neg)))
    lw = -jnp.exp(-softplus - 0.5)
    a = _sigmoid(a0_ref[...] + _dot_x3_presplit(zl[:, :256], iup_ref[...]))
    g_o[...] = _dot(_sigmoid(zl[:, 128:]), gup_ref[...])
    if has_vres:
        mix = _sigmoid(v0_ref[...] + _dot(zl[:, 384:], vup_ref[...]))
        v = v + (vfirst_ref[...] - v) * mix
    if emit_v:
        v_o[...] = v

    sel, selt = sel_ref[...], selt_ref[...]
    kk = k * kk_ref[...]
    norm = jnp.sqrt(_head_sum(kk * kk, sel, selt))
    kk = kk / jnp.maximum(norm, 1e-12)
    k = k * (1.0 + (a - 1.0) * ka_ref[...])
    bonus_o[...] = _head_sum(r * k * rk_ref[...], sel, selt) * v

    d = functools.partial(jnp.dot, preferred_element_type=F32)
    lw_hi = lw.astype(BF16)
    rest = lw - lw_hi.astype(F32)
    lw_mid = rest.astype(BF16)
    lw_lo = (rest - lw_mid.astype(F32)).astype(BF16)
    tri, ones = tri_ref[...], ones_ref[...]
    cum = d(tri, lw_hi) + d(tri, lw_mid) + d(tri, lw_lo)
    total = d(ones, lw_hi) + d(ones, lw_mid) + d(ones, lw_lo)
    w_inv = jnp.exp(-cum)
    w_rem = jnp.exp(total - cum)
    b = kk * a
    wc_o[...] = jnp.exp(total)
    pairs = R // LANES
    for o_ref, val in ((at_o, -kk * jnp.exp(cum - lw)), (rt_o, r * jnp.exp(cum)), (bt_o, b * w_inv),
                       (kt_o, k * w_inv), (bp_o, b * w_rem), (kp_o, k * w_rem), (vb_o, v)):
        val = val.astype(BF16)
        for p in range(pairs):
            o_ref[0, p] = val[:, p * LANES:(p + 1) * LANES]


def _rwkv_prep(p_all, bsz, seq, width, consts, vres, emit_v):
    tm = PREP_ROWS
    m = p_all.shape[0]
    R = width
    lora_w = 512
    lora_blk = (p_all.shape[1] - lora_w) // lora_w
    tiles_per_seq = seq // tm
    row = lambda i: (i, 0)
    fixed = lambda i: (0, 0)
    in_specs = [pl.BlockSpec((tm, 3 * R), row),
                pl.BlockSpec((tm, lora_w), lambda i: (i, lora_blk))]
    args = [p_all, p_all]
    for c in consts:
        in_specs.append(pl.BlockSpec(c.shape, fixed))
        args.append(c)
    if vres is not None:
        vfirst, v0, vup = vres
        in_specs += [pl.BlockSpec((tm, R), row), pl.BlockSpec(v0.shape, fixed), pl.BlockSpec(vup.shape, fixed)]
        args += [vfirst, v0, vup]
    pairs = R // LANES
    pm_spec = pl.BlockSpec((1, pairs, tm, LANES), lambda i: (i // tiles_per_seq, 0, i % tiles_per_seq, 0))
    pm_shape = jax.ShapeDtypeStruct((bsz, pairs, seq, LANES), BF16)
    nat_spec = pl.BlockSpec((tm, R), row)
    nat_shape = jax.ShapeDtypeStruct((m, R), F32)
    n_nat = 4 if emit_v else 3
    return pl.pallas_call(
        functools.partial(_prep_body, tiles_per_seq=tiles_per_seq, has_vres=vres is not None, emit_v=emit_v,
                          width=R),
        grid=(m // tm,),
        in_specs=in_specs,
        out_specs=[pm_spec] * 7 + [nat_spec] * n_nat,
        out_shape=[pm_shape] * 7 + [nat_shape] * n_nat,
        scratch_shapes=[pltpu.VMEM((1, 3 * R), F32), pltpu.VMEM((1, lora_w), F32)],
        compiler_params=_cparams(1),
        name="rwkv_prep",
    )(*args)


def _wkv_body(at_ref, rt_ref, bt_ref, kt_ref, bp_ref, kp_ref, v_ref, wc_ref, y_ref, state_ref, *, pairs, chunks):
    C = WKV_CHUNK
    hs = HEAD_SIZE

    @pl.when(pl.program_id(1) == 0)
    def _():
        state_ref[...] = jnp.zeros_like(state_ref)

    def head_masks(rows, width):
        lane = lax.broadcasted_iota(jnp.int32, (rows, width), 1)
        first = (lane % LANES) < hs
        return first, jnp.logical_not(first)

    def stack_heads(x):
        m0, m1 = head_masks(*x.shape)
        zero = jnp.zeros_like(x)
        return jnp.concatenate([jnp.where(m0, x, zero), jnp.where(m1, x, zero)], axis=0)

    row = lax.broadcasted_iota(jnp.int32, (C, LANES), 0)
    col = lax.broadcasted_iota(jnp.int32, (C, LANES), 1) % hs
    strict, incl = row > col, row >= col
    sq_r = lax.broadcasted_iota(jnp.int32, (LANES, LANES), 0)
    sq_c = lax.broadcasted_iota(jnp.int32, (LANES, LANES), 1)
    same_head = (sq_r // hs) == (sq_c // hs)
    eye = sq_r == sq_c
    levels = C.bit_length() - 1
    dot = functools.partial(jnp.dot, preferred_element_type=F32)
    P = range(pairs)

    def step(c, carry):
        sl = pl.ds(pl.multiple_of(c * C, C), C)
        at = [at_ref[0, p, sl, :] for p in P]
        rt = [rt_ref[0, p, sl, :] for p in P]
        v = [v_ref[0, p, sl, :] for p in P]
        vs = [stack_heads(v[p]) for p in P]
        lhs = [jnp.concatenate([at[p], rt[p]], axis=0) for p in P]
        rhs = [jnp.concatenate([stack_heads(bt_ref[0, p, sl, :]), stack_heads(kt_ref[0, p, sl, :])], axis=0)
               for p in P]
        abk = [_dot_nt(lhs[p], rhs[p]) for p in P]
        n = [jnp.where(strict, abk[p][:C, :LANES], 0.0) for p in P]
        a_rb = [jnp.where(incl, abk[p][C:, :LANES], 0.0) for p in P]
        a_ak = [jnp.where(strict, abk[p][:C, LANES:], 0.0) for p in P]
        a_rk = [jnp.where(incl, abk[p][C:, LANES:], 0.0) for p in P]
        av = [dot(a_ak[p].astype(BF16), vs[p]) for p in P]
        x = [jnp.concatenate([at[p].astype(F32), av[p]], axis=1) for p in P]
        for lvl in range(levels):
            last = lvl + 1 == levels
            for p in P:
                z = x[p] if last else jnp.concatenate([n[p], x[p]], axis=1)
                out = dot(n[p].astype(BF16), stack_heads(z.astype(BF16)))
                if last:
                    x[p] = x[p] + out
                else:
                    n[p] = out[:, :LANES]
                    x[p] = x[p] + out[:, LANES:]
        xb = [x[p].astype(BF16) for p in P]
        zeros_c = jnp.zeros((C, LANES), BF16)
        bx = [_dot_tn(jnp.concatenate([bp_ref[0, p, sl, :], kp_ref[0, p, sl, :]], axis=0),
                      jnp.concatenate([xb[p], jnp.concatenate([zeros_c, v[p]], axis=1)], axis=0)) for p in P]
        zeros_2c = jnp.zeros((2 * C, LANES), BF16)
        rx = [dot(jnp.concatenate([a_rb[p], a_rk[p]], axis=1).astype(BF16),
                  jnp.concatenate([stack_heads(xb[p]), jnp.concatenate([zeros_2c, vs[p]], axis=1)], axis=0))
              for p in P]
        for p in P:
            state = state_ref[p]
            wc = wc_ref[pl.ds(pl.multiple_of(c * C, C), 1), p * LANES:(p + 1) * LANES]
            trans = jnp.where(same_head, bx[p][:, :LANES], 0.0) + jnp.where(eye, wc, 0.0)
            inject = jnp.where(same_head, bx[p][:, LANES:], 0.0)
            r_eff = rt[p].astype(F32) + rx[p][:, :LANES]
            y_ref[0, p, sl, :] = _dot(r_eff, state) + rx[p][:, LANES:]
            state_ref[p] = _dot_x3(trans, state) + inject
        return carry

    lax.fori_loop(0, chunks, step, 0)


def _wkv(at, rt, bt, kt, bp, kp, v, wc, pairs=8, span=512):
    bsz, n_pairs, seq, _ = at.shape
    groups = n_pairs // pairs
    spec = pl.BlockSpec((1, pairs, span, LANES), lambda g, t: (g // groups, g % groups, t, 0))
    wc_spec = pl.BlockSpec((span, pairs * LANES), lambda g, t: ((g // groups) * (seq // span) + t, g % groups))
    return pl.pallas_call(
        functools.partial(_wkv_body, pairs=pairs, chunks=span // WKV_CHUNK),
        grid=(bsz * groups, seq // span),
        in_specs=[spec] * 7 + [wc_spec],
        out_specs=spec,
        out_shape=jax.ShapeDtypeStruct((bsz, n_pairs, seq, LANES), F32),
        scratch_shapes=[pltpu.VMEM((pairs, LANES, LANES), F32)],
        compiler_params=_cparams(2),
        name="wkv7",
    )(at, rt, bt, kt, bp, kp, v, wc)


def _mixer_outputs(y_ref, bonus_ref, g_ref, u_ref, lnw_ref, lnb_ref, sel_ref, selt_ref, pw_ref, ps_ref, ext_ref,
                   *, tiles_per_seq, tm, group_width):
    seq_tile = pl.program_id(0) % tiles_per_seq

    @pl.when(seq_tile == 0)
    def _():
        ext_ref[0:POOL_HALO, :] = jnp.zeros((POOL_HALO, ext_ref.shape[1]), F32)

    u = u_ref[...]
    ext_ref[POOL_HALO:POOL_HALO + tm, :] = u
    t_pos = seq_tile * tm + lax.broadcasted_iota(jnp.int32, (tm, group_width), 0)
    groups = []
    for gi, w in enumerate(POOL_WINDOWS):
        lanes = slice(gi * group_width, (gi + 1) * group_width)
        acc = u[:, lanes]
        for j in range(1, w):
            acc = acc + ext_ref[POOL_HALO - j:POOL_HALO - j + tm, lanes]
        count = jnp.minimum(t_pos + 1, w).astype(F32)
        mixed = acc / count - u[:, lanes]
        yg = _dot_x3(mixed, pw_ref[gi])
        groups.append((yg * ps_ref[:, lanes]).astype(BF16))
    ext_ref[0:POOL_HALO, :] = u[tm - POOL_HALO:, :]

    sel, selt = sel_ref[...], selt_ref[...]
    inv_n = 1.0 / HEAD_SIZE
    y = jnp.concatenate([y_ref[0, p] for p in range(y_ref.shape[1])], axis=1)
    mean = _head_sum(y, sel, selt) * inv_n
    yc = y - mean
    var = _head_sum(yc * yc, sel, selt) * inv_n
    yn = yc * lax.rsqrt(var + GN_EPS) * lnw_ref[...] + lnb_ref[...]
    ya = ((yn + bonus_ref[...]) * g_ref[...]).astype(BF16)
    return ya, jnp.concatenate(groups, axis=1)


def _post_mix_body(y_ref, bonus_ref, g_ref, u_ref, ga_ref, gb_ref, x_ref, lnw_ref, lnb_ref, sel_ref, selt_ref,
                   pw_ref, ps_ref, pa_ref, pb_ref, wo_ref, o_ref, ext_ref, **static):
    ya, yb = _mixer_outputs(y_ref, bonus_ref, g_ref, u_ref, lnw_ref, lnb_ref, sel_ref, selt_ref, pw_ref, ps_ref,
                            ext_ref, **static)
    ma = jnp.dot(ya, pa_ref[...], preferred_element_type=F32)
    mb = jnp.dot(yb, pb_ref[...], preferred_element_type=F32)
    merged = (_sigmoid(ga_ref[...]) * ma + _sigmoid(gb_ref[...]) * mb).astype(BF16)
    o_ref[...] = x_ref[...] + jnp.dot(merged, wo_ref[...], preferred_element_type=F32)


def _post_mix(y, bonus, g, p_all, pool_col, ga_col, gb_col, x, consts, weights, seq, tm=256):
    m, R = bonus.shape
    d = x.shape[1]
    lnw, lnb, sel, selt, pw, ps = consts
    tiles_per_seq = seq // tm
    row = lambda i: (i, 0)
    fixed2 = lambda i: (0, 0)
    resident = lambda w: pl.BlockSpec(w.shape, fixed2, pipeline_mode=pl.Buffered(1))
    in_specs = [pl.BlockSpec((1, y.shape[1], tm, LANES), lambda i: (i // tiles_per_seq, 0, i % tiles_per_seq, 0)),
                pl.BlockSpec((tm, R), row), pl.BlockSpec((tm, R), row),
                pl.BlockSpec((tm, R), lambda i: (i, pool_col // R)),
                pl.BlockSpec((tm, d), lambda i: (i, ga_col // d)),
                pl.BlockSpec((tm, d), lambda i: (i, gb_col // d)),
                pl.BlockSpec((tm, d), row)]
    in_specs += [pl.BlockSpec(c.shape, fixed2) for c in (lnw, lnb, sel, selt)]
    in_specs += [pl.BlockSpec(pw.shape, lambda i: (0, 0, 0)), pl.BlockSpec(ps.shape, fixed2)]
    in_specs += [resident(w) for w in weights]
    return pl.pallas_call(
        functools.partial(_post_mix_body, tiles_per_seq=tiles_per_seq, tm=tm, group_width=R // len(POOL_WINDOWS)),
        grid=(m // tm,),
        in_specs=in_specs,
        out_specs=pl.BlockSpec((tm, d), row),
        out_shape=jax.ShapeDtypeStruct((m, d), F32),
        scratch_shapes=[pltpu.VMEM((POOL_HALO + tm, R), F32)],
        compiler_params=_cparams(1),
        name="post_mix",
    )(y, bonus, g, p_all, p_all, p_all, x, lnw, lnb, sel, selt, pw, ps, *weights)


def _router_body(x_ref, g_ref, wr_ref, tri_ref, h_o, idx_o, gate_o, cnt_o, carry_ref):
    @pl.when(pl.program_id(0) == 0)
    def _():
        carry_ref[...] = jnp.zeros_like(carry_ref)

    x = x_ref[...]
    inv = lax.rsqrt(jnp.mean(x * x, axis=-1, keepdims=True) + NORM_EPS)
    h = x * inv * g_ref[...]
    h_o[...] = h.reshape(h_o.shape)
    logits = _dot_x3_presplit(h, wr_ref[...])
    lane = lax.broadcasted_iota(jnp.int32, logits.shape, 1)
    neg_inf = jnp.float32(-jnp.inf)
    l1 = jnp.where(lane < N_EXPERTS, logits, neg_inf)
    m1 = jnp.max(l1, axis=-1, keepdims=True)
    i1 = jnp.min(jnp.where(l1 == m1, lane, LANES), axis=-1, keepdims=True)
    l2 = jnp.where(lane == i1, neg_inf, l1)
    m2 = jnp.max(l2, axis=-1, keepdims=True)
    i2 = jnp.min(jnp.where(l2 == m2, lane, LANES), axis=-1, keepdims=True)
    e = jnp.exp(m2 - m1)
    g1 = 1.0 / (1.0 + e)
    gate_o[...] = jnp.where(lane == 0, g1, jnp.where(lane == 1, e * g1, 0.0))

    first, second = lane == i1, lane == i2
    picked = jnp.where(jnp.logical_or(first, second), 1.0, 0.0)
    before = carry_ref[...] + jnp.dot(tri_ref[...], picked.astype(BF16), preferred_element_type=F32)
    rank1 = jnp.sum(jnp.where(first, before, 0.0), axis=-1, keepdims=True).astype(jnp.int32)
    rank2 = jnp.sum(jnp.where(second, before, 0.0), axis=-1, keepdims=True).astype(jnp.int32)
    idx_o[...] = jnp.where(lane == 0, i1, jnp.where(lane == 1, i2, jnp.where(lane == 2, rank1,
                                                                              jnp.where(lane == 3, rank2, 0))))
    carry_ref[...] += jnp.sum(picked, axis=0, keepdims=True)
    cnt_o[...] = jnp.broadcast_to(carry_ref[...], cnt_o.shape).astype(jnp.int32)


def _router(x, g, wr_pad, tm=256):
    m, d = x.shape
    row = lambda i: (i, 0)
    t_idx = jnp.arange(tm, dtype=jnp.int32)
    tri_strict = (t_idx[None, :] < t_idx[:, None]).astype(BF16)
    return pl.pallas_call(
        _router_body,
        grid=(m // tm,),
        in_specs=[pl.BlockSpec((tm, d), row), pl.BlockSpec((1, d), lambda i: (0, 0)),
                  pl.BlockSpec(wr_pad.shape, lambda i: (0, 0)), pl.BlockSpec((tm, tm), lambda i: (0, 0))],
        out_specs=[pl.BlockSpec((tm, d // LANES, LANES), lambda i: (i, 0, 0)),
                   pl.BlockSpec((tm, LANES), row), pl.BlockSpec((tm, LANES), row),
                   pl.BlockSpec((8, LANES), lambda i: (0, 0))],
        out_shape=[jax.ShapeDtypeStruct((m, d // LANES, LANES), F32), jax.ShapeDtypeStruct((m, LANES), jnp.int32),
                   jax.ShapeDtypeStruct((m, LANES), F32), jax.ShapeDtypeStruct((8, LANES), jnp.int32)],
        scratch_shapes=[pltpu.VMEM((1, LANES), F32)],
        compiler_params=_cparams(1),
        name="router",
    )(x, g.reshape(1, d), wr_pad, tri_strict)


def _gather_body(tok_ref, used_ref, h_hbm, o_ref, buf_ref, sems, *, tg, n_steps):
    i = pl.program_id(0)

    def row_copy(step, r):
        slot = step % 2
        return pltpu.make_async_copy(h_hbm.at[pl.ds(tok_ref[step * tg + r], 1)], buf_ref.at[slot, pl.ds(r, 1)],
                                     sems.at[slot])

    def fetch(step):
        @pl.when(jnp.logical_and(step < n_steps, step * tg < used_ref[0]))
        def _():
            def start(q, c):
                for u in range(DMA_UNROLL):
                    row_copy(step, q * DMA_UNROLL + u).start()
                return c
            lax.fori_loop(0, tg // DMA_UNROLL, start, 0)

    @pl.when(i == 0)
    def _():
        fetch(i)

    fetch(i + 1)

    @pl.when(i * tg < used_ref[0])
    def _():
        pltpu.make_async_copy(h_hbm.at[pl.ds(0, tg)], buf_ref.at[i % 2], sems.at[i % 2]).wait()
        o_ref[...] = buf_ref[i % 2].reshape(o_ref.shape).astype(o_ref.dtype)

    @pl.when(i * tg >= used_ref[0])
    def _():
        o_ref[...] = jnp.zeros_like(o_ref)


def _gather_rows(tok, used_rows, h3, tg=256):
    p = tok.shape[0]
    _, nc, lanes = h3.shape
    n_steps = p // tg
    return pl.pallas_call(
        functools.partial(_gather_body, tg=tg, n_steps=n_steps),
        grid_spec=pltpu.PrefetchScalarGridSpec(
            num_scalar_prefetch=2,
            grid=(n_steps,),
            in_specs=[pl.BlockSpec(memory_space=pl.ANY)],
            out_specs=pl.BlockSpec((tg, nc * lanes), lambda i, tok, used: (i, 0)),
            scratch_shapes=[pltpu.VMEM((2, tg, nc, lanes), F32), pltpu.SemaphoreType.DMA((2,))]),
        out_shape=jax.ShapeDtypeStruct((p, nc * lanes), BF16),
        compiler_params=_cparams(1),
        name="moe_gather",
    )(tok, used_rows, h3)


def _experts_body(plan_ref, x_ref, wg_ref, wu_ref, wd_ref, o_ref, wgb_ref, wub_ref, wdb_ref, *, n_items):
    w = pl.program_id(0)
    j = pl.program_id(1)
    field = lambda f: plan_ref[f * n_items + w]
    lo, hi, zero, live = field(3), field(4), field(5), field(6)
    n_sub = MOE_ROWS // MOE_SUB

    @pl.when(jnp.logical_and(zero == 1, j == 0))
    def _():
        o_ref[...] = jnp.zeros_like(o_ref)

    def rows(sl, wg, wu, wd):
        x = x_ref[sl, :]
        g = jnp.dot(x, wg, preferred_element_type=F32)
        u = jnp.dot(x, wu, preferred_element_type=F32)
        h = (g * _sigmoid(g) * u).astype(BF16)
        o_ref[sl, :] += jnp.dot(h, wd, preferred_element_type=F32)

    whole = jnp.logical_and(live == 1, jnp.logical_and(lo == 0, hi == n_sub))
    partial = jnp.logical_and(live == 1, jnp.logical_not(whole))

    @pl.when(whole)
    def _():
        rows(slice(None), wg_ref[...].astype(BF16), wu_ref[...].astype(BF16), wd_ref[...].astype(BF16))

    @pl.when(partial)
    def _():
        wgb_ref[...] = wg_ref[...].astype(BF16)
        wub_ref[...] = wu_ref[...].astype(BF16)
        wdb_ref[...] = wd_ref[...].astype(BF16)

    for s in range(n_sub):
        @pl.when(jnp.logical_and(partial, jnp.logical_and(lo <= s, s < hi)))
        def _():
            rows(slice(s * MOE_SUB, (s + 1) * MOE_SUB), wgb_ref[...], wub_ref[...], wdb_ref[...])


def _experts(plan, xs, w_gate, w_up, w_down):
    p, d = xs.shape
    f = w_gate.shape[2]
    tn = MOE_FF_TILE
    nj = f // tn
    n_items = plan.shape[0] // PLAN_FIELDS
    fld = lambda plan, f_idx, w: plan[f_idx * n_items + w]
    col = lambda plan, w, j: jnp.where(fld(plan, 6, w) == 1, j, nj - 1)
    return pl.pallas_call(
        functools.partial(_experts_body, n_items=n_items),
        grid_spec=pltpu.PrefetchScalarGridSpec(
            num_scalar_prefetch=1,
            grid=(n_items, nj),
            in_specs=[pl.BlockSpec((MOE_ROWS, d), lambda w, j, plan: (fld(plan, 1, w), 0)),
                      pl.BlockSpec((None, d, tn), lambda w, j, plan: (fld(plan, 2, w), 0, col(plan, w, j))),
                      pl.BlockSpec((None, d, tn), lambda w, j, plan: (fld(plan, 2, w), 0, col(plan, w, j))),
                      pl.BlockSpec((None, tn, d), lambda w, j, plan: (fld(plan, 2, w), col(plan, w, j), 0))],
            out_specs=pl.BlockSpec((MOE_ROWS, d), lambda w, j, plan: (fld(plan, 0, w), 0)),
            scratch_shapes=[pltpu.VMEM((d, tn), BF16), pltpu.VMEM((d, tn), BF16), pltpu.VMEM((tn, d), BF16)]),
        out_shape=jax.ShapeDtypeStruct((p, d), F32),
        compiler_params=_cparams(2),
        name="moe_experts",
    )(plan, xs, w_gate, w_up, w_down)


def _combine_body(d0_ref, d1_ref, x_ref, gate_ref, g_ref, y_hbm, o_ref, buf_ref, sems, *, tg, n_steps, apply_norm):
    i = pl.program_id(0)

    def row_copy(step, r, d_ref, pick):
        slot = step % 2
        return pltpu.make_async_copy(y_hbm.at[pl.ds(d_ref[step * tg + r], 1), :],
                                     buf_ref.at[slot, pick, pl.ds(r, 1), :], sems.at[slot])

    def fetch(step):
        @pl.when(step < n_steps)
        def _():
            def start(q, c):
                for u in range(DMA_UNROLL):
                    row_copy(step, q * DMA_UNROLL + u, d0_ref, 0).start()
                    row_copy(step, q * DMA_UNROLL + u, d1_ref, 1).start()
                return c
            lax.fori_loop(0, tg // DMA_UNROLL, start, 0)

    @pl.when(i == 0)
    def _():
        fetch(i)

    fetch(i + 1)
    slot = i % 2
    for pick in range(2):
        pltpu.make_async_copy(y_hbm.at[pl.ds(0, tg), :], buf_ref.at[slot, pick], sems.at[slot]).wait()
    gate = gate_ref[...]
    x = x_ref[...] + gate[:, 0:1] * buf_ref[slot, 0] + gate[:, 1:2] * buf_ref[slot, 1]
    if apply_norm:
        x = x * lax.rsqrt(jnp.mean(x * x, axis=-1, keepdims=True) + NORM_EPS) * g_ref[...]
    o_ref[...] = x


def _combine(d0, d1, x, gate, g, ybuf, apply_norm, tg=256):
    m, d = x.shape
    row = lambda i, d0, d1: (i, 0)
    return pl.pallas_call(
        functools.partial(_combine_body, tg=tg, n_steps=m // tg, apply_norm=apply_norm),
        grid_spec=pltpu.PrefetchScalarGridSpec(
            num_scalar_prefetch=2,
            grid=(m // tg,),
            in_specs=[pl.BlockSpec((tg, d), row), pl.BlockSpec((tg, LANES), row),
                      pl.BlockSpec((1, d), lambda i, d0, d1: (0, 0)),
                      pl.BlockSpec(memory_space=pl.ANY)],
            out_specs=pl.BlockSpec((tg, d), row),
            scratch_shapes=[pltpu.VMEM((2, 2, tg, d), F32), pltpu.SemaphoreType.DMA((2,))]),
        out_shape=jax.ShapeDtypeStruct((m, d), F32),
        compiler_params=_cparams(1),
        name="moe_combine",
    )(d0, d1, x, gate, g.reshape(1, d), ybuf)


def _moe_plan(top_e, rank, counts):
    i32 = jnp.int32
    n_tok = top_e.shape[0]
    n_slot = n_tok * TOP_K
    flat_e = top_e.reshape(n_slot)
    rank = rank.reshape(n_slot)
    padded = (counts + MOE_SUB - 1) // MOE_SUB * MOE_SUB
    pend = jnp.cumsum(padded)
    pstart = pend - padded
    dest = pstart[flat_e] + rank
    n_blocks = -(-(n_slot + N_EXPERTS * MOE_SUB) // MOE_ROWS)
    tok = jnp.zeros((n_blocks * MOE_ROWS,), i32).at[dest].set(jnp.arange(n_slot, dtype=i32) // TOP_K)
    used_rows = pend[-1]
    used_blocks = (used_rows + MOE_ROWS - 1) // MOE_ROWS

    n_items = n_blocks + N_EXPERTS
    blk_lo = (jnp.arange(n_blocks, dtype=i32) * MOE_ROWS)[:, None]
    lo_row = jnp.maximum(pstart[None, :], blk_lo)
    hi_row = jnp.minimum(pend[None, :], blk_lo + MOE_ROWS)
    valid = (hi_row > lo_row).reshape(-1)
    n_live = jnp.sum(valid.astype(i32))
    pos = jnp.where(valid, jnp.cumsum(valid.astype(i32)) - 1, n_items)
    compact = lambda v: jnp.zeros((n_items,), i32).at[pos].set(v.reshape(-1).astype(i32), mode="drop")
    blk_c = compact(jnp.broadcast_to(jnp.arange(n_blocks, dtype=i32)[:, None], lo_row.shape))
    exp_c = compact(jnp.broadcast_to(jnp.arange(N_EXPERTS, dtype=i32)[None, :], lo_row.shape))
    lo_c = compact((lo_row - blk_lo) // MOE_SUB)
    hi_c = compact((hi_row - blk_lo) // MOE_SUB)
    first_c = compact(lo_row == blk_lo)

    w = jnp.arange(n_items, dtype=i32)
    live = w < n_live
    tail_blk = used_blocks + (w - n_live)
    last = n_live - 1
    out_blk = jnp.where(live, blk_c, jnp.minimum(tail_blk, n_blocks - 1))
    x_blk = jnp.where(live, blk_c, blk_c[last])
    expert = jnp.where(live, exp_c, exp_c[last])
    zero = jnp.where(live, first_c, (tail_blk < n_blocks).astype(i32))
    plan = jnp.concatenate([out_blk, x_blk, expert, lo_c, hi_c, zero, live.astype(i32)]).astype(i32)
    dest = dest.reshape(n_tok, TOP_K).astype(i32)
    return tok, used_rows.astype(i32).reshape(1), plan, dest[:, 0], dest[:, 1]


def kernel(x, norm_mix, w_in, shift_mu, decay_w0, decay_up, iclr_a0, iclr_up, outgate_up, k_k, k_a, r_k, lnx_w, lnx_b, vres_down, vres_mu, vres_up, vres_v0, pool_w, pool_scale, proj_a, proj_b, w_o, norm_ffn, ffn_gate, ffn_up, ffn_down, router, moe_gate, moe_up, moe_down, norm_final):
    bsz, seq, d = x.shape
    depth = w_in.shape[0]
    R = decay_w0.shape[1]
    dl, il, gl = decay_up.shape[1], iclr_up.shape[1], outgate_up.shape[1]
    vl = vres_up.shape[1]
    assert (dl, il, vl, gl) == (96, 96, 64, 256) and 3 * R + dl + il + gl == shift_mu.shape[1]
    m = bsz * seq
    x = x.reshape(m, d)
    row = lambda t: t.reshape(1, -1)

    head_of_lane = jnp.arange(R, dtype=jnp.int32) // HEAD_SIZE
    sel = (head_of_lane[:, None] == jnp.arange(LANES, dtype=jnp.int32)[None, :]).astype(BF16)
    sel_t = sel.T
    t_idx = jnp.arange(PREP_ROWS, dtype=jnp.int32)
    same_chunk = (t_idx[:, None] // WKV_CHUNK) == (t_idx[None, :] // WKV_CHUNK)
    ones_bd = same_chunk.astype(BF16)
    tri_bd = (same_chunk & (t_idx[None, :] <= t_idx[:, None])).astype(BF16)

    w_in_t = jnp.swapaxes(w_in, 1, 2)
    v_first = None
    for l in range(depth):
        c0 = 3 * R
        c3 = c0 + dl + il + gl
        tn = 512
        lora_w = c3 - c0 + vl
        assert c0 % tn == 0 and (w_in.shape[2] - c3) % tn == 0 and lora_w == tn
        src_rows = list(range(0, c0, tn)) + list(range(c3, w_in.shape[2], tn)) + [c0]
        vd_mu = vres_mu[l - 1] if l > 0 else jnp.zeros((vl,), F32)
        extra = None
        if l > 0:
            extra = jnp.concatenate([jnp.zeros((d, c3 - c0), F32), vres_down[l - 1]], axis=1).astype(BF16)
        mu = shift_mu[l]
        mu_rkv = row(mu[:c0])
        mu_lora = row(jnp.concatenate([mu[c0:c3], vd_mu]))
        pool_col = c0
        ga_col = c0 + R
        gb_col = ga_col + d
        zeros = lambda n: jnp.zeros((n, R), F32)
        dup = _hi_lo_rows(jnp.concatenate([decay_up[l], zeros(128 - dl)], axis=0))
        iup = _hi_lo_rows(jnp.concatenate([zeros(dl), iclr_up[l], zeros(256 - dl - il)], axis=0))
        gup = jnp.concatenate([zeros(dl + il - 128), outgate_up[l], zeros(vl)], axis=0).astype(BF16)

        p_all = _in_proj(x, norm_mix[l], w_in_t, l, src_rows, extra, c3 - c0, tn=tn)

        consts = [mu_rkv, mu_lora, row(decay_w0[l]), row(iclr_a0[l]), row(k_k[l]), row(k_a[l]), row(r_k[l]),
                  dup, iup, gup, sel, sel_t, tri_bd, ones_bd]
        vres = None
        if l > 0:
            vup = jnp.concatenate([zeros(128 - vl), vres_up[l - 1]], axis=0).astype(BF16)
            vres = (v_first, row(vres_v0[l - 1]), vup)
        emit_v = l == 0 and depth > 1
        outs = _rwkv_prep(p_all, bsz, seq, R, consts, vres, emit_v)
        wc, bonus, g = outs[7:10]
        if emit_v:
            v_first = outs[10]
        y = _wkv(*outs[:7], wc)

        assert pool_col % R == 0 and ga_col % d == 0 and gb_col % d == 0
        x = _post_mix(y, bonus, g, p_all, pool_col, ga_col, gb_col, x,
                      [row(lnx_w[l]), row(lnx_b[l]), sel, sel_t, pool_w[l], row(pool_scale[l])],
                      [proj_a[l].astype(BF16), proj_b[l].astype(BF16), w_o[l].astype(BF16)], seq)

        i = l // 2
        if l % 2 == 0:
            hidden = _norm_swiglu(x, norm_ffn[l], ffn_gate, ffn_up, i, tm=1024, tn=512)
            x = _matmul_residual(hidden, ffn_down, i, x, tm=512, tn=512)
            if l == depth - 1:
                x = _rmsnorm(x, norm_final, F32)
        else:
            wr_pad = _hi_lo_rows(jnp.concatenate([router[i], jnp.zeros((d, LANES - N_EXPERTS), F32)], axis=1))
            h3, idx, gate, counts = _router(x, norm_ffn[l], wr_pad)
            tok, used_rows, plan, d0, d1 = _moe_plan(idx[:, :TOP_K], idx[:, TOP_K:2 * TOP_K], counts[0, :N_EXPERTS])
            xs = _gather_rows(tok, used_rows, h3)
            ybuf = _experts(plan, xs, moe_gate[i], moe_up[i], moe_down[i])
            x = _combine(d0, d1, x, gate, norm_final, ybuf, apply_norm=l == depth - 1)
    return x.reshape(bsz, seq, d)
```

```python
import functools

import jax
import jax.numpy as jnp
from jax import lax
from jax.experimental import pallas as pl
from jax.experimental.pallas import tpu as pltpu

F32 = jnp.float32
BF16 = jnp.bfloat16

HEAD_SIZE = 64
POOL_WINDOWS = (2, 4, 8, 16)
N_EXPERTS = 8
TOP_K = 2
NORM_EPS = 1e-6
GN_EPS = 64e-5
LANES = 128
VMEM_LIMIT_BYTES = 60 * 1024 * 1024
WKV_CHUNK = 64
POOL_HALO = 16
PREP_ROWS = 256
IN_PROJ_ROW_UNIT = 32
DMA_UNROLL = 8
MOE_ROWS = 1024
MOE_SUB = 256
MOE_FF_TILE = 256
PLAN_FIELDS = 7


def _cparams(n_axes):
    return pltpu.CompilerParams(dimension_semantics=("arbitrary",) * n_axes,
                                vmem_limit_bytes=VMEM_LIMIT_BYTES)


def _dot(a, b):
    return jnp.dot(a.astype(BF16), b.astype(BF16), preferred_element_type=F32)


def _split(x):
    hi = x.astype(BF16)
    lo = (x - hi.astype(F32)).astype(BF16)
    return hi, lo


def _dot_x3(a, b):
    a_hi, a_lo = _split(a)
    b_hi, b_lo = _split(b)
    d = functools.partial(jnp.dot, preferred_element_type=F32)
    return d(a_hi, b_hi) + d(a_hi, b_lo) + d(a_lo, b_hi)


def _hi_lo_rows(w):
    hi = w.astype(BF16)
    return jnp.concatenate([hi, (w - hi.astype(F32)).astype(BF16)], axis=0)


def _dot_x3_presplit(a, w_hi_lo):
    k = w_hi_lo.shape[0] // 2
    a_hi, a_lo = _split(a)
    d = functools.partial(jnp.dot, preferred_element_type=F32)
    return d(a_hi, w_hi_lo[:k]) + d(a_hi, w_hi_lo[k:]) + d(a_lo, w_hi_lo[:k])


def _dot_lhs_split(a, b_bf16):
    a_hi, a_lo = _split(a)
    d = functools.partial(jnp.dot, preferred_element_type=F32)
    return d(a_hi, b_bf16) + d(a_lo, b_bf16)


def _dot_tn(a, b):
    return lax.dot_general(a.astype(BF16), b.astype(BF16), (((0,), (0,)), ((), ())),
                           preferred_element_type=F32)


def _dot_nt(a, b):
    return lax.dot_general(a.astype(BF16), b.astype(BF16), (((1,), (1,)), ((), ())),
                           preferred_element_type=F32)


def _sigmoid(x):
    return 1.0 / (1.0 + jnp.exp(-x))


def _rmsnorm_body(x_ref, g_ref, o_ref):
    x = x_ref[...]
    inv = lax.rsqrt(jnp.mean(x * x, axis=-1, keepdims=True) + NORM_EPS)
    o_ref[...] = (x * inv * g_ref[...]).astype(o_ref.dtype)


def _rmsnorm(x, g, out_dtype, tm=512):
    m, d = x.shape
    return pl.pallas_call(
        _rmsnorm_body,
        grid=(m // tm,),
        in_specs=[pl.BlockSpec((tm, d), lambda i: (i, 0)),
                  pl.BlockSpec((1, d), lambda i: (0, 0))],
        out_specs=pl.BlockSpec((tm, d), lambda i: (i, 0)),
        out_shape=jax.ShapeDtypeStruct((m, d), out_dtype),
        compiler_params=_cparams(1),
        name="rmsnorm",
    )(x, g.reshape(1, d))


def _mm_residual_body(a_ref, w_ref, res_ref, o_ref, wb_ref):
    @pl.when(pl.program_id(1) == 0)
    def _():
        wb_ref[...] = w_ref[...].astype(BF16)

    o_ref[...] = res_ref[...] + jnp.dot(a_ref[...], wb_ref[...], preferred_element_type=F32)


def _matmul_residual(a, w, layer, residual, *, tm, tn):
    m, k = a.shape
    n = w.shape[-1]
    o_spec = pl.BlockSpec((tm, tn), lambda j, i: (i, j))
    return pl.pallas_call(
        _mm_residual_body,
        grid=(n // tn, m // tm),
        in_specs=[pl.BlockSpec((tm, k), lambda j, i: (i, 0)),
                  pl.BlockSpec((None, k, tn), lambda j, i: (layer, 0, j)),
                  o_spec],
        out_specs=o_spec,
        out_shape=jax.ShapeDtypeStruct((m, n), F32),
        scratch_shapes=[pltpu.VMEM((k, tn), BF16)],
        compiler_params=_cparams(2),
        name="matmul_residual",
    )(a, w, residual)


def _norm_swiglu_body(x_ref, g_ref, wg_ref, wu_ref, o_ref, hb_ref):
    @pl.when(pl.program_id(1) == 0)
    def _():
        x = x_ref[...]
        inv = lax.rsqrt(jnp.mean(x * x, axis=-1, keepdims=True) + NORM_EPS)
        hb_ref[...] = (x * inv * g_ref[...]).astype(BF16)

    h = hb_ref[...]
    g = jnp.dot(h, wg_ref[...].astype(BF16), preferred_element_type=F32)
    u = jnp.dot(h, wu_ref[...].astype(BF16), preferred_element_type=F32)
    o_ref[...] = (g * _sigmoid(g) * u).astype(o_ref.dtype)


def _norm_swiglu(x, gain, w_gate, w_up, layer, *, tm, tn):
    m, k = x.shape
    n = w_gate.shape[-1]
    w_spec = pl.BlockSpec((None, k, tn), lambda i, j: (layer, 0, j))
    return pl.pallas_call(
        _norm_swiglu_body,
        grid=(m // tm, n // tn),
        in_specs=[pl.BlockSpec((tm, k), lambda i, j: (i, 0)), pl.BlockSpec((1, k), lambda i, j: (0, 0)),
                  w_spec, w_spec],
        out_specs=pl.BlockSpec((tm, tn), lambda i, j: (i, j)),
        out_shape=jax.ShapeDtypeStruct((m, n), BF16),
        scratch_shapes=[pltpu.VMEM((tm, k), BF16)],
        compiler_params=_cparams(2),
        name="norm_swiglu",
    )(x, gain.reshape(1, k), w_gate, w_up)


def _in_proj_body(*refs, has_extra, n_tiles, keep_cols):
    if has_extra:
        src_ref, x_ref, g_ref, wt_ref, extra_ref, o_ref, hb_ref = refs
    else:
        src_ref, x_ref, g_ref, wt_ref, o_ref, hb_ref = refs
    j = pl.program_id(1)

    @pl.when(j == 0)
    def _():
        x = x_ref[...]
        inv = lax.rsqrt(jnp.mean(x * x, axis=-1, keepdims=True) + NORM_EPS)
        hb_ref[...] = (x * inv * g_ref[...]).astype(BF16)

    h = hb_ref[...]
    out = lax.dot_general(h, wt_ref[0].astype(BF16), (((1,), (1,)), ((), ())), preferred_element_type=F32)
    if has_extra:
        @pl.when(j < n_tiles - 1)
        def _():
            o_ref[...] = out

        @pl.when(j == n_tiles - 1)
        def _():
            col = lax.broadcasted_iota(jnp.int32, out.shape, 1)
            o_ref[...] = jnp.where(col < keep_cols, out, 0.0) + jnp.dot(h, extra_ref[...],
                                                                        preferred_element_type=F32)
    else:
        o_ref[...] = out


def _in_proj(x, gain, wt, layer, src_rows, extra, keep_cols, tm=2048, tn=512):
    m, k = x.shape
    n_tiles = len(src_rows)
    assert all(r % IN_PROJ_ROW_UNIT == 0 for r in src_rows)
    src = jnp.asarray([r // IN_PROJ_ROW_UNIT for r in src_rows], jnp.int32)
    element = lambda n: pl.Element(n)
    tm = min(tm, m)
    in_specs = [pl.BlockSpec((tm, k), lambda i, j, src: (i, 0), pipeline_mode=pl.Buffered(1)),
                pl.BlockSpec((1, k), lambda i, j, src: (0, 0)),
                pl.BlockSpec((element(1), element(tn), element(k)),
                             lambda i, j, src: (layer, src[j] * IN_PROJ_ROW_UNIT, 0))]
    args = [src, x, gain.reshape(1, k), wt]
    if extra is not None:
        in_specs.append(pl.BlockSpec((k, tn), lambda i, j, src: (0, 0), pipeline_mode=pl.Buffered(1)))
        args.append(extra)
    return pl.pallas_call(
        functools.partial(_in_proj_body, has_extra=extra is not None, n_tiles=n_tiles, keep_cols=keep_cols),
        grid_spec=pltpu.PrefetchScalarGridSpec(
            num_scalar_prefetch=1,
            grid=(m // tm, n_tiles),
            in_specs=in_specs,
            out_specs=pl.BlockSpec((tm, tn), lambda i, j, src: (i, j)),
            scratch_shapes=[pltpu.VMEM((tm, k), BF16)]),
        out_shape=jax.ShapeDtypeStruct((m, n_tiles * tn), F32),
        compiler_params=_cparams(2),
        name="in_proj",
    )(*args)


def _head_sum(x, sel, sel_t):
    s = _dot_lhs_split(x, sel)
    return _dot_lhs_split(s, sel_t)


def _token_shift(p, carry_ref, mu):
    rows = p.shape[0]
    row = lax.broadcasted_iota(jnp.int32, p.shape, 0)
    prev = jnp.where(row == 0, carry_ref[...], pltpu.roll(p, 1, axis=0))
    carry_ref[...] = p[rows - 1:rows, :]
    return p + (prev - p) * mu


def _prep_body(*refs, tiles_per_seq, has_vres, emit_v, width):
    (rkv_ref, lora_ref, mu_rkv_ref, mu_lora_ref, w0_ref, a0_ref, kk_ref, ka_ref, rk_ref,
     dup_ref, iup_ref, gup_ref, sel_ref, selt_ref, tri_ref, ones_ref) = refs[:16]
    pos = 16
    if has_vres:
        vfirst_ref, v0_ref, vup_ref = refs[pos:pos + 3]
        pos += 3
    at_o, rt_o, bt_o, kt_o, bp_o, kp_o, vb_o, wc_o, bonus_o, g_o = refs[pos:pos + 10]
    pos += 10
    if emit_v:
        v_o = refs[pos]
        pos += 1
    carry_rkv, carry_lora = refs[pos:]
    R = width

    @pl.when(pl.program_id(0) % tiles_per_seq == 0)
    def _():
        carry_rkv[...] = jnp.zeros_like(carry_rkv)
        carry_lora[...] = jnp.zeros_like(carry_lora)

    z = _token_shift(rkv_ref[...], carry_rkv, mu_rkv_ref[...])
    zl = _token_shift(lora_ref[...], carry_lora, mu_lora_ref[...])
    r, k, v = z[:, :R], z[:, R:2 * R], z[:, 2 * R:]

    decay_logit = w0_ref[...] + _dot_x3_presplit(jnp.tanh(zl[:, :128]), dup_ref[...])
    neg = -decay_logit
    softplus = jnp.maximum(neg, 0.0) + jnp.log(1.0 + jnp.exp(-jnp.abs(neg)))
    lw = -jnp.exp(-softplus - 0.5)
    a = _sigmoid(a0_ref[...] + _dot_x3_presplit(zl[:, :256], iup_ref[...]))
    g_o[...] = _dot(_sigmoid(zl[:, 128:]), gup_ref[...])
    if has_vres:
        mix = _sigmoid(v0_ref[...] + _dot(zl[:, 384:], vup_ref[...]))
        v = v + (vfirst_ref[...] - v) * mix
    if emit_v:
        v_o[...] = v

    sel, selt = sel_ref[...], selt_ref[...]
    kk = k * kk_ref[...]
    norm = jnp.sqrt(_head_sum(kk * kk, sel, selt))
    kk = kk / jnp.maximum(norm, 1e-12)
    k = k * (1.0 + (a - 1.0) * ka_ref[...])
    bonus_o[...] = _head_sum(r * k * rk_ref[...], sel, selt) * v

    d = functools.partial(jnp.dot, preferred_element_type=F32)
    lw_hi = lw.astype(BF16)
    rest = lw - lw_hi.astype(F32)
    lw_mid = rest.astype(BF16)
    lw_lo = (rest - lw_mid.astype(F32)).astype(BF16)
    tri, ones = tri_ref[...], ones_ref[...]
    cum = d(tri, lw_hi) + d(tri, lw_mid) + d(tri, lw_lo)
    total = d(ones, lw_hi) + d(ones, lw_mid) + d(ones, lw_lo)
    w_inv = jnp.exp(-cum)
    w_rem = jnp.exp(total - cum)
    b = kk * a
    wc_o[...] = jnp.exp(total)
    pairs = R // LANES
    for o_ref, val in ((at_o, -kk * jnp.exp(cum - lw)), (rt_o, r * jnp.exp(cum)), (bt_o, b * w_inv),
                       (kt_o, k * w_inv), (bp_o, b * w_rem), (kp_o, k * w_rem), (vb_o, v)):
        val = val.astype(BF16)
        for p in range(pairs):
            o_ref[0, p] = val[:, p * LANES:(p + 1) * LANES]


def _rwkv_prep(p_all, bsz, seq, width, consts, vres, emit_v):
    tm = PREP_ROWS
    m = p_all.shape[0]
    R = width
    lora_w = 512
    lora_blk = (p_all.shape[1] - lora_w) // lora_w
    tiles_per_seq = seq // tm
    row = lambda i: (i, 0)
    fixed = lambda i: (0, 0)
    in_specs = [pl.BlockSpec((tm, 3 * R), row),
                pl.BlockSpec((tm, lora_w), lambda i: (i, lora_blk))]
    args = [p_all, p_all]
    for c in consts:
        in_specs.append(pl.BlockSpec(c.shape, fixed))
        args.append(c)
    if vres is not None:
        vfirst, v0, vup = vres
        in_specs += [pl.BlockSpec((tm, R), row), pl.BlockSpec(v0.shape, fixed), pl.BlockSpec(vup.shape, fixed)]
        args += [vfirst, v0, vup]
    pairs = R // LANES
    pm_spec = pl.BlockSpec((1, pairs, tm, LANES), lambda i: (i // tiles_per_seq, 0, i % tiles_per_seq, 0))
    pm_shape = jax.ShapeDtypeStruct((bsz, pairs, seq, LANES), BF16)
    nat_spec = pl.BlockSpec((tm, R), row)
    nat_shape = jax.ShapeDtypeStruct((m, R), F32)
    n_nat = 4 if emit_v else 3
    return pl.pallas_call(
        functools.partial(_prep_body, tiles_per_seq=tiles_per_seq, has_vres=vres is not None, emit_v=emit_v,
                          width=R),
        grid=(m // tm,),
        in_specs=in_specs,
        out_specs=[pm_spec] * 7 + [nat_spec] * n_nat,
        out_shape=[pm_shape] * 7 + [nat_shape] * n_nat,
        scratch_shapes=[pltpu.VMEM((1, 3 * R), F32), pltpu.VMEM((1, lora_w), F32)],
        compiler_params=_cparams(1),
        name="rwkv_prep",
    )(*args)


def _wkv_body(at_ref, rt_ref, bt_ref, kt_ref, bp_ref, kp_ref, v_ref, wc_ref, y_ref, state_ref, *,
              batches, pairs, chunks):
    C = WKV_CHUNK
    hs = HEAD_SIZE

    @pl.when(pl.program_id(1) == 0)
    def _():
        state_ref[...] = jnp.zeros_like(state_ref)

    def head_masks(rows, width):
        lane = lax.broadcasted_iota(jnp.int32, (rows, width), 1)
        first = (lane % LANES) < hs
        return first, jnp.logical_not(first)

    def stack_heads(x):
        m0, m1 = head_masks(*x.shape)
        zero = jnp.zeros_like(x)
        return jnp.concatenate([jnp.where(m0, x, zero), jnp.where(m1, x, zero)], axis=0)

    row = lax.broadcasted_iota(jnp.int32, (C, LANES), 0)
    col = lax.broadcasted_iota(jnp.int32, (C, LANES), 1) % hs
    strict, incl = row > col, row >= col
    sq_r = lax.broadcasted_iota(jnp.int32, (LANES, LANES), 0)
    sq_c = lax.broadcasted_iota(jnp.int32, (LANES, LANES), 1)
    same_head = (sq_r // hs) == (sq_c // hs)
    eye = sq_r == sq_c
    levels = C.bit_length() - 1
    dot = functools.partial(jnp.dot, preferred_element_type=F32)
    P = range(batches * pairs)

    def step(c, carry):
        sl = pl.ds(pl.multiple_of(c * C, C), C)
        at = [at_ref[p // pairs, p % pairs, sl, :] for p in P]
        rt = [rt_ref[p // pairs, p % pairs, sl, :] for p in P]
        v = [v_ref[p // pairs, p % pairs, sl, :] for p in P]
        vs = [stack_heads(v[p]) for p in P]
        lhs = [jnp.concatenate([at[p], rt[p]], axis=0) for p in P]
        rhs = [jnp.concatenate([stack_heads(bt_ref[p // pairs, p % pairs, sl, :]), stack_heads(kt_ref[p // pairs, p % pairs, sl, :])], axis=0)
               for p in P]
        abk = [_dot_nt(lhs[p], rhs[p]) for p in P]
        n = [jnp.where(strict, abk[p][:C, :LANES], 0.0) for p in P]
        a_rb = [jnp.where(incl, abk[p][C:, :LANES], 0.0) for p in P]
        a_ak = [jnp.where(strict, abk[p][:C, LANES:], 0.0) for p in P]
        a_rk = [jnp.where(incl, abk[p][C:, LANES:], 0.0) for p in P]
        av = [dot(a_ak[p].astype(BF16), vs[p]) for p in P]
        x = [jnp.concatenate([at[p].astype(F32), av[p]], axis=1) for p in P]
        for lvl in range(levels):
            last = lvl + 1 == levels
            for p in P:
                z = x[p] if last else jnp.concatenate([n[p], x[p]], axis=1)
                out = dot(n[p].astype(BF16), stack_heads(z.astype(BF16)))
                if last:
                    x[p] = x[p] + out
                else:
                    n[p] = out[:, :LANES]
                    x[p] = x[p] + out[:, LANES:]
        xb = [x[p].astype(BF16) for p in P]
        zeros_c = jnp.zeros((C, LANES), BF16)
        bx = [_dot_tn(jnp.concatenate([bp_ref[p // pairs, p % pairs, sl, :], kp_ref[p // pairs, p % pairs, sl, :]], axis=0),
                      jnp.concatenate([xb[p], jnp.concatenate([zeros_c, v[p]], axis=1)], axis=0)) for p in P]
        zeros_2c = jnp.zeros((2 * C, LANES), BF16)
        rx = [dot(jnp.concatenate([a_rb[p], a_rk[p]], axis=1).astype(BF16),
                  jnp.concatenate([stack_heads(xb[p]), jnp.concatenate([zeros_2c, vs[p]], axis=1)], axis=0))
              for p in P]
        for p in P:
            state = state_ref[p]
            lanes = slice((p % pairs) * LANES, (p % pairs + 1) * LANES)
            wc = wc_ref[p // pairs, pl.ds(pl.multiple_of(c * C, C), 1), lanes]
            trans = jnp.where(same_head, bx[p][:, :LANES], 0.0) + jnp.where(eye, wc, 0.0)
            inject = jnp.where(same_head, bx[p][:, LANES:], 0.0)
            r_eff = rt[p].astype(F32) + rx[p][:, :LANES]
            y_ref[p // pairs, p % pairs, sl, :] = _dot(r_eff, state) + rx[p][:, LANES:]
            state_ref[p] = _dot_x3(trans, state) + inject
        return carry

    lax.fori_loop(0, chunks, step, 0)


def _wkv(at, rt, bt, kt, bp, kp, v, wc, pairs=8, span=512):
    bsz, n_pairs, seq, _ = at.shape
    spec = pl.BlockSpec((bsz, pairs, span, LANES), lambda g, t: (0, g, t, 0))
    wc_spec = pl.BlockSpec((bsz, span, pairs * LANES), lambda g, t: (0, t, g))
    return pl.pallas_call(
        functools.partial(_wkv_body, batches=bsz, pairs=pairs, chunks=span // WKV_CHUNK),
        grid=(n_pairs // pairs, seq // span),
        in_specs=[spec] * 7 + [wc_spec],
        out_specs=spec,
        out_shape=jax.ShapeDtypeStruct((bsz, n_pairs, seq, LANES), F32),
        scratch_shapes=[pltpu.VMEM((bsz * pairs, LANES, LANES), F32)],
        compiler_params=_cparams(2),
        name="wkv7",
    )(at, rt, bt, kt, bp, kp, v, wc.reshape(bsz, seq, wc.shape[1]))


def _mixer_outputs(y_ref, bonus_ref, g_ref, u_ref, lnw_ref, lnb_ref, sel_ref, selt_ref, pw_ref, ps_ref, ext_ref,
                   *, tiles_per_seq, tm, group_width):
    seq_tile = pl.program_id(0) % tiles_per_seq

    @pl.when(seq_tile == 0)
    def _():
        ext_ref[0:POOL_HALO, :] = jnp.zeros((POOL_HALO, ext_ref.shape[1]), F32)

    u = u_ref[...]
    ext_ref[POOL_HALO:POOL_HALO + tm, :] = u
    t_pos = seq_tile * tm + lax.broadcasted_iota(jnp.int32, (tm, group_width), 0)
    groups = []
    for gi, w in enumerate(POOL_WINDOWS):
        lanes = slice(gi * group_width, (gi + 1) * group_width)
        acc = u[:, lanes]
        for j in range(1, w):
            acc = acc + ext_ref[POOL_HALO - j:POOL_HALO - j + tm, lanes]
        count = jnp.minimum(t_pos + 1, w).astype(F32)
        mixed = acc / count - u[:, lanes]
        yg = _dot_x3(mixed, pw_ref[gi])
        groups.append((yg * ps_ref[:, lanes]).astype(BF16))
    ext_ref[0:POOL_HALO, :] = u[tm - POOL_HALO:, :]

    sel, selt = sel_ref[...], selt_ref[...]
    inv_n = 1.0 / HEAD_SIZE
    y = jnp.concatenate([y_ref[0, p] for p in range(y_ref.shape[1])], axis=1)
    mean = _head_sum(y, sel, selt) * inv_n
    yc = y - mean
    var = _head_sum(yc * yc, sel, selt) * inv_n
    yn = yc * lax.rsqrt(var + GN_EPS) * lnw_ref[...] + lnb_ref[...]
    ya = ((yn + bonus_ref[...]) * g_ref[...]).astype(BF16)
    return ya, jnp.concatenate(groups, axis=1)


def _post_mix_body(y_ref, bonus_ref, g_ref, u_ref, ga_ref, gb_ref, x_ref, lnw_ref, lnb_ref, sel_ref, selt_ref,
                   pw_ref, ps_ref, pa_ref, pb_ref, wo_ref, o_ref, ext_ref, **static):
    ya, yb = _mixer_outputs(y_ref, bonus_ref, g_ref, u_ref, lnw_ref, lnb_ref, sel_ref, selt_ref, pw_ref, ps_ref,
                            ext_ref, **static)
    ma = jnp.dot(ya, pa_ref[...], preferred_element_type=F32)
    mb = jnp.dot(yb, pb_ref[...], preferred_element_type=F32)
    merged = (_sigmoid(ga_ref[...]) * ma + _sigmoid(gb_ref[...]) * mb).astype(BF16)
    o_ref[...] = x_ref[...] + jnp.dot(merged, wo_ref[...], preferred_element_type=F32)


def _post_mix(y, bonus, g, p_all, pool_col, ga_col, gb_col, x, consts, weights, seq, tm=256):
    m, R = bonus.shape
    d = x.shape[1]
    lnw, lnb, sel, selt, pw, ps = consts
    tiles_per_seq = seq // tm
    row = lambda i: (i, 0)
    fixed2 = lambda i: (0, 0)
    resident = lambda w: pl.BlockSpec(w.shape, fixed2, pipeline_mode=pl.Buffered(1))
    in_specs = [pl.BlockSpec((1, y.shape[1], tm, LANES), lambda i: (i // tiles_per_seq, 0, i % tiles_per_seq, 0)),
                pl.BlockSpec((tm, R), row), pl.BlockSpec((tm, R), row),
                pl.BlockSpec((tm, R), lambda i: (i, pool_col // R)),
                pl.BlockSpec((tm, d), lambda i: (i, ga_col // d)),
                pl.BlockSpec((tm, d), lambda i: (i, gb_col // d)),
                pl.BlockSpec((tm, d), row)]
    in_specs += [pl.BlockSpec(c.shape, fixed2) for c in (lnw, lnb, sel, selt)]
    in_specs += [pl.BlockSpec(pw.shape, lambda i: (0, 0, 0)), pl.BlockSpec(ps.shape, fixed2)]
    in_specs += [resident(w) for w in weights]
    return pl.pallas_call(
        functools.partial(_post_mix_body, tiles_per_seq=tiles_per_seq, tm=tm, group_width=R // len(POOL_WINDOWS)),
        grid=(m // tm,),
        in_specs=in_specs,
        out_specs=pl.BlockSpec((tm, d), row),
        out_shape=jax.ShapeDtypeStruct((m, d), F32),
        scratch_shapes=[pltpu.VMEM((POOL_HALO + tm, R), F32)],
        compiler_params=_cparams(1),
        name="post_mix",
    )(y, bonus, g, p_all, p_all, p_all, x, lnw, lnb, sel, selt, pw, ps, *weights)


def _router_body(x_ref, g_ref, wr_ref, tri_ref, h_o, idx_o, gate_o, cnt_o, carry_ref):
    @pl.when(pl.program_id(0) == 0)
    def _():
        carry_ref[...] = jnp.zeros_like(carry_ref)

    x = x_ref[...]
    inv = lax.rsqrt(jnp.mean(x * x, axis=-1, keepdims=True) + NORM_EPS)
    h = x * inv * g_ref[...]
    h_o[...] = h.reshape(h_o.shape)
    logits = _dot_x3_presplit(h, wr_ref[...])
    lane = lax.broadcasted_iota(jnp.int32, logits.shape, 1)
    neg_inf = jnp.float32(-jnp.inf)
    l1 = jnp.where(lane < N_EXPERTS, logits, neg_inf)
    m1 = jnp.max(l1, axis=-1, keepdims=True)
    i1 = jnp.min(jnp.where(l1 == m1, lane, LANES), axis=-1, keepdims=True)
    l2 = jnp.where(lane == i1, neg_inf, l1)
    m2 = jnp.max(l2, axis=-1, keepdims=True)
    i2 = jnp.min(jnp.where(l2 == m2, lane, LANES), axis=-1, keepdims=True)
    e = jnp.exp(m2 - m1)
    g1 = 1.0 / (1.0 + e)
    gate_o[...] = jnp.where(lane == 0, g1, jnp.where(lane == 1, e * g1, 0.0))

    first, second = lane == i1, lane == i2
    picked = jnp.where(jnp.logical_or(first, second), 1.0, 0.0)
    before = carry_ref[...] + jnp.dot(tri_ref[...], picked.astype(BF16), preferred_element_type=F32)
    rank1 = jnp.sum(jnp.where(first, before, 0.0), axis=-1, keepdims=True).astype(jnp.int32)
    rank2 = jnp.sum(jnp.where(second, before, 0.0), axis=-1, keepdims=True).astype(jnp.int32)
    idx_o[...] = jnp.where(lane == 0, i1, jnp.where(lane == 1, i2, jnp.where(lane == 2, rank1,
                                                                              jnp.where(lane == 3, rank2, 0))))
    carry_ref[...] += jnp.sum(picked, axis=0, keepdims=True)
    cnt_o[...] = jnp.broadcast_to(carry_ref[...], cnt_o.shape).astype(jnp.int32)


def _router(x, g, wr_pad, tm=256):
    m, d = x.shape
    row = lambda i: (i, 0)
    t_idx = jnp.arange(tm, dtype=jnp.int32)
    tri_strict = (t_idx[None, :] < t_idx[:, None]).astype(BF16)
    return pl.pallas_call(
        _router_body,
        grid=(m // tm,),
        in_specs=[pl.BlockSpec((tm, d), row), pl.BlockSpec((1, d), lambda i: (0, 0)),
                  pl.BlockSpec(wr_pad.shape, lambda i: (0, 0)), pl.BlockSpec((tm, tm), lambda i: (0, 0))],
        out_specs=[pl.BlockSpec((tm, d // LANES, LANES), lambda i: (i, 0, 0)),
                   pl.BlockSpec((tm, LANES), row), pl.BlockSpec((tm, LANES), row),
                   pl.BlockSpec((8, LANES), lambda i: (0, 0))],
        out_shape=[jax.ShapeDtypeStruct((m, d // LANES, LANES), F32), jax.ShapeDtypeStruct((m, LANES), jnp.int32),
                   jax.ShapeDtypeStruct((m, LANES), F32), jax.ShapeDtypeStruct((8, LANES), jnp.int32)],
        scratch_shapes=[pltpu.VMEM((1, LANES), F32)],
        compiler_params=_cparams(1),
        name="router",
    )(x, g.reshape(1, d), wr_pad, tri_strict)


def _gather_body(tok_ref, used_ref, h_hbm, o_ref, buf_ref, sems, *, tg, n_steps):
    i = pl.program_id(0)

    def row_copy(step, r):
        slot = step % 2
        return pltpu.make_async_copy(h_hbm.at[pl.ds(tok_ref[step * tg + r], 1)], buf_ref.at[slot, pl.ds(r, 1)],
                                     sems.at[slot])

    def fetch(step):
        @pl.when(jnp.logical_and(step < n_steps, step * tg < used_ref[0]))
        def _():
            def start(q, c):
                for u in range(DMA_UNROLL):
                    row_copy(step, q * DMA_UNROLL + u).start()
                return c
            lax.fori_loop(0, tg // DMA_UNROLL, start, 0)

    @pl.when(i == 0)
    def _():
        fetch(i)

    fetch(i + 1)

    @pl.when(i * tg < used_ref[0])
    def _():
        pltpu.make_async_copy(h_hbm.at[pl.ds(0, tg)], buf_ref.at[i % 2], sems.at[i % 2]).wait()
        o_ref[...] = buf_ref[i % 2].reshape(o_ref.shape).astype(o_ref.dtype)

    @pl.when(i * tg >= used_ref[0])
    def _():
        o_ref[...] = jnp.zeros_like(o_ref)


def _gather_rows(tok, used_rows, h3, tg=256):
    p = tok.shape[0]
    _, nc, lanes = h3.shape
    n_steps = p // tg
    return pl.pallas_call(
        functools.partial(_gather_body, tg=tg, n_steps=n_steps),
        grid_spec=pltpu.PrefetchScalarGridSpec(
            num_scalar_prefetch=2,
            grid=(n_steps,),
            in_specs=[pl.BlockSpec(memory_space=pl.ANY)],
            out_specs=pl.BlockSpec((tg, nc * lanes), lambda i, tok, used: (i, 0)),
            scratch_shapes=[pltpu.VMEM((2, tg, nc, lanes), F32), pltpu.SemaphoreType.DMA((2,))]),
        out_shape=jax.ShapeDtypeStruct((p, nc * lanes), BF16),
        compiler_params=_cparams(1),
        name="moe_gather",
    )(tok, used_rows, h3)


def _experts_body(plan_ref, x_ref, wg_ref, wu_ref, wd_ref, o_ref, wgb_ref, wub_ref, wdb_ref, *, n_items):
    w = pl.program_id(0)
    j = pl.program_id(1)
    field = lambda f: plan_ref[f * n_items + w]
    lo, hi, zero, live = field(3), field(4), field(5), field(6)
    n_sub = MOE_ROWS // MOE_SUB

    @pl.when(jnp.logical_and(zero == 1, j == 0))
    def _():
        o_ref[...] = jnp.zeros_like(o_ref)

    def rows(sl, wg, wu, wd):
        x = x_ref[sl, :]
        g = jnp.dot(x, wg, preferred_element_type=F32)
        u = jnp.dot(x, wu, preferred_element_type=F32)
        h = (g * _sigmoid(g) * u).astype(BF16)
        o_ref[sl, :] += jnp.dot(h, wd, preferred_element_type=F32)

    whole = jnp.logical_and(live == 1, jnp.logical_and(lo == 0, hi == n_sub))
    partial = jnp.logical_and(live == 1, jnp.logical_not(whole))

    @pl.when(whole)
    def _():
        rows(slice(None), wg_ref[...].astype(BF16), wu_ref[...].astype(BF16), wd_ref[...].astype(BF16))

    @pl.when(partial)
    def _():
        wgb_ref[...] = wg_ref[...].astype(BF16)
        wub_ref[...] = wu_ref[...].astype(BF16)
        wdb_ref[...] = wd_ref[...].astype(BF16)

    for s in range(n_sub):
        @pl.when(jnp.logical_and(partial, jnp.logical_and(lo <= s, s < hi)))
        def _():
            rows(slice(s * MOE_SUB, (s + 1) * MOE_SUB), wgb_ref[...], wub_ref[...], wdb_ref[...])


def _experts(plan, xs, w_gate, w_up, w_down):
    p, d = xs.shape
    f = w_gate.shape[2]
    tn = MOE_FF_TILE
    nj = f // tn
    n_items = plan.shape[0] // PLAN_FIELDS
    fld = lambda plan, f_idx, w: plan[f_idx * n_items + w]
    col = lambda plan, w, j: jnp.where(fld(plan, 6, w) == 1, j, nj - 1)
    return pl.pallas_call(
        functools.partial(_experts_body, n_items=n_items),
        grid_spec=pltpu.PrefetchScalarGridSpec(
            num_scalar_prefetch=1,
            grid=(n_items, nj),
            in_specs=[pl.BlockSpec((MOE_ROWS, d), lambda w, j, plan: (fld(plan, 1, w), 0)),
                      pl.BlockSpec((None, d, tn), lambda w, j, plan: (fld(plan, 2, w), 0, col(plan, w, j))),
                      pl.BlockSpec((None, d, tn), lambda w, j, plan: (fld(plan, 2, w), 0, col(plan, w, j))),
                      pl.BlockSpec((None, tn, d), lambda w, j, plan: (fld(plan, 2, w), col(plan, w, j), 0))],
            out_specs=pl.BlockSpec((MOE_ROWS, d), lambda w, j, plan: (fld(plan, 0, w), 0)),
            scratch_shapes=[pltpu.VMEM((d, tn), BF16), pltpu.VMEM((d, tn), BF16), pltpu.VMEM((tn, d), BF16)]),
        out_shape=jax.ShapeDtypeStruct((p, d), F32),
        compiler_params=_cparams(2),
        name="moe_experts",
    )(plan, xs, w_gate, w_up, w_down)


def _combine_body(d0_ref, d1_ref, x_ref, gate_ref, g_ref, y_hbm, o_ref, buf_ref, sems, *, tg, n_steps, apply_norm):
    i = pl.program_id(0)

    def row_copy(step, r, d_ref, pick):
        slot = step % 2
        return pltpu.make_async_copy(y_hbm.at[pl.ds(d_ref[step * tg + r], 1), :],
                                     buf_ref.at[slot, pick, pl.ds(r, 1), :], sems.at[slot])

    def fetch(step):
        @pl.when(step < n_steps)
        def _():
            def start(q, c):
                for u in range(DMA_UNROLL):
                    row_copy(step, q * DMA_UNROLL + u, d0_ref, 0).start()
                    row_copy(step, q * DMA_UNROLL + u, d1_ref, 1).start()
                return c
            lax.fori_loop(0, tg // DMA_UNROLL, start, 0)

    @pl.when(i == 0)
    def _():
        fetch(i)

    fetch(i + 1)
    slot = i % 2
    for pick in range(2):
        pltpu.make_async_copy(y_hbm.at[pl.ds(0, tg), :], buf_ref.at[slot, pick], sems.at[slot]).wait()
    gate = gate_ref[...]
    x = x_ref[...] + gate[:, 0:1] * buf_ref[slot, 0] + gate[:, 1:2] * buf_ref[slot, 1]
    if apply_norm:
        x = x * lax.rsqrt(jnp.mean(x * x, axis=-1, keepdims=True) + NORM_EPS) * g_ref[...]
    o_ref[...] = x


def _combine(d0, d1, x, gate, g, ybuf, apply_norm, tg=256):
    m, d = x.shape
    row = lambda i, d0, d1: (i, 0)
    return pl.pallas_call(
        functools.partial(_combine_body, tg=tg, n_steps=m // tg, apply_norm=apply_norm),
        grid_spec=pltpu.PrefetchScalarGridSpec(
            num_scalar_prefetch=2,
            grid=(m // tg,),
            in_specs=[pl.BlockSpec((tg, d), row), pl.BlockSpec((tg, LANES), row),
                      pl.BlockSpec((1, d), lambda i, d0, d1: (0, 0)),
                      pl.BlockSpec(memory_space=pl.ANY)],
            out_specs=pl.BlockSpec((tg, d), row),
            scratch_shapes=[pltpu.VMEM((2, 2, tg, d), F32), pltpu.SemaphoreType.DMA((2,))]),
        out_shape=jax.ShapeDtypeStruct((m, d), F32),
        compiler_params=_cparams(1),
        name="moe_combine",
    )(d0, d1, x, gate, g.reshape(1, d), ybuf)


def _moe_plan(top_e, rank, counts):
    i32 = jnp.int32
    n_tok = top_e.shape[0]
    n_slot = n_tok * TOP_K
    flat_e = top_e.reshape(n_slot)
    rank = rank.reshape(n_slot)
    padded = (counts + MOE_SUB - 1) // MOE_SUB * MOE_SUB
    pend = jnp.cumsum(padded)
    pstart = pend - padded
    dest = pstart[flat_e] + rank
    n_blocks = -(-(n_slot + N_EXPERTS * MOE_SUB) // MOE_ROWS)
    tok = jnp.zeros((n_blocks * MOE_ROWS,), i32).at[dest].set(jnp.arange(n_slot, dtype=i32) // TOP_K)
    used_rows = pend[-1]
    used_blocks = (used_rows + MOE_ROWS - 1) // MOE_ROWS

    n_items = n_blocks + N_EXPERTS
    blk_lo = (jnp.arange(n_blocks, dtype=i32) * MOE_ROWS)[:, None]
    lo_row = jnp.maximum(pstart[None, :], blk_lo)
    hi_row = jnp.minimum(pend[None, :], blk_lo + MOE_ROWS)
    valid = (hi_row > lo_row).reshape(-1)
    n_live = jnp.sum(valid.astype(i32))
    pos = jnp.where(valid, jnp.cumsum(valid.astype(i32)) - 1, n_items)
    compact = lambda v: jnp.zeros((n_items,), i32).at[pos].set(v.reshape(-1).astype(i32), mode="drop")
    blk_c = compact(jnp.broadcast_to(jnp.arange(n_blocks, dtype=i32)[:, None], lo_row.shape))
    exp_c = compact(jnp.broadcast_to(jnp.arange(N_EXPERTS, dtype=i32)[None, :], lo_row.shape))
    lo_c = compact((lo_row - blk_lo) // MOE_SUB)
    hi_c = compact((hi_row - blk_lo) // MOE_SUB)
    first_c = compact(lo_row == blk_lo)

    w = jnp.arange(n_items, dtype=i32)
    live = w < n_live
    tail_blk = used_blocks + (w - n_live)
    last = n_live - 1
    out_blk = jnp.where(live, blk_c, jnp.minimum(tail_blk, n_blocks - 1))
    x_blk = jnp.where(live, blk_c, blk_c[last])
    expert = jnp.where(live, exp_c, exp_c[last])
    zero = jnp.where(live, first_c, (tail_blk < n_blocks).astype(i32))
    plan = jnp.concatenate([out_blk, x_blk, expert, lo_c, hi_c, zero, live.astype(i32)]).astype(i32)
    dest = dest.reshape(n_tok, TOP_K).astype(i32)
    return tok, used_rows.astype(i32).reshape(1), plan, dest[:, 0], dest[:, 1]


def kernel(x, norm_mix, w_in, shift_mu, decay_w0, decay_up, iclr_a0, iclr_up, outgate_up, k_k, k_a, r_k, lnx_w, lnx_b, vres_down, vres_mu, vres_up, vres_v0, pool_w, pool_scale, proj_a, proj_b, w_o, norm_ffn, ffn_gate, ffn_up, ffn_down, router, moe_gate, moe_up, moe_down, norm_final):
    bsz, seq, d = x.shape
    depth = w_in.shape[0]
    R = decay_w0.shape[1]
    dl, il, gl = decay_up.shape[1], iclr_up.shape[1], outgate_up.shape[1]
    vl = vres_up.shape[1]
    assert (dl, il, vl, gl) == (96, 96, 64, 256) and 3 * R + dl + il + gl == shift_mu.shape[1]
    m = bsz * seq
    x = x.reshape(m, d)
    row = lambda t: t.reshape(1, -1)

    head_of_lane = jnp.arange(R, dtype=jnp.int32) // HEAD_SIZE
    sel = (head_of_lane[:, None] == jnp.arange(LANES, dtype=jnp.int32)[None, :]).astype(BF16)
    sel_t = sel.T
    t_idx = jnp.arange(PREP_ROWS, dtype=jnp.int32)
    same_chunk = (t_idx[:, None] // WKV_CHUNK) == (t_idx[None, :] // WKV_CHUNK)
    ones_bd = same_chunk.astype(BF16)
    tri_bd = (same_chunk & (t_idx[None, :] <= t_idx[:, None])).astype(BF16)

    w_in_t = jnp.swapaxes(w_in, 1, 2)
    v_first = None
    for l in range(depth):
        c0 = 3 * R
        c3 = c0 + dl + il + gl
        tn = 512
        lora_w = c3 - c0 + vl
        assert c0 % tn == 0 and (w_in.shape[2] - c3) % tn == 0 and lora_w == tn
        src_rows = list(range(0, c0, tn)) + list(range(c3, w_in.shape[2], tn)) + [c0]
        vd_mu = vres_mu[l - 1] if l > 0 else jnp.zeros((vl,), F32)
        extra = None
        if l > 0:
            extra = jnp.concatenate([jnp.zeros((d, c3 - c0), F32), vres_down[l - 1]], axis=1).astype(BF16)
        mu = shift_mu[l]
        mu_rkv = row(mu[:c0])
        mu_lora = row(jnp.concatenate([mu[c0:c3], vd_mu]))
        pool_col = c0
        ga_col = c0 + R
        gb_col = ga_col + d
        zeros = lambda n: jnp.zeros((n, R), F32)
        dup = _hi_lo_rows(jnp.concatenate([decay_up[l], zeros(128 - dl)], axis=0))
        iup = _hi_lo_rows(jnp.concatenate([zeros(dl), iclr_up[l], zeros(256 - dl - il)], axis=0))
        gup = jnp.concatenate([zeros(dl + il - 128), outgate_up[l], zeros(vl)], axis=0).astype(BF16)

        p_all = _in_proj(x, norm_mix[l], w_in_t, l, src_rows, extra, c3 - c0, tn=tn)

        consts = [mu_rkv, mu_lora, row(decay_w0[l]), row(iclr_a0[l]), row(k_k[l]), row(k_a[l]), row(r_k[l]),
                  dup, iup, gup, sel, sel_t, tri_bd, ones_bd]
        vres = None
        if l > 0:
            vup = jnp.concatenate([zeros(128 - vl), vres_up[l - 1]], axis=0).astype(BF16)
            vres = (v_first, row(vres_v0[l - 1]), vup)
        emit_v = l == 0 and depth > 1
        outs = _rwkv_prep(p_all, bsz, seq, R, consts, vres, emit_v)
        wc, bonus, g = outs[7:10]
        if emit_v:
            v_first = outs[10]
        y = _wkv(*outs[:7], wc)

        assert pool_col % R == 0 and ga_col % d == 0 and gb_col % d == 0
        x = _post_mix(y, bonus, g, p_all, pool_col, ga_col, gb_col, x,
                      [row(lnx_w[l]), row(lnx_b[l]), sel, sel_t, pool_w[l], row(pool_scale[l])],
                      [proj_a[l].astype(BF16), proj_b[l].astype(BF16), w_o[l].astype(BF16)], seq)

        i = l // 2
        if l % 2 == 0:
            hidden = _norm_swiglu(x, norm_ffn[l], ffn_gate, ffn_up, i, tm=1024, tn=512)
            x = _matmul_residual(hidden, ffn_down, i, x, tm=512, tn=512)
            if l == depth - 1:
                x = _rmsnorm(x, norm_final, F32)
        else:
            wr_pad = _hi_lo_rows(jnp.concatenate([router[i], jnp.zeros((d, LANES - N_EXPERTS), F32)], axis=1))
            h3, idx, gate, counts = _router(x, norm_ffn[l], wr_pad)
            tok, used_rows, plan, d0, d1 = _moe_plan(idx[:, :TOP_K], idx[:, TOP_K:2 * TOP_K], counts[0, :N_EXPERTS])
            xs = _gather_rows(tok, used_rows, h3)
            ybuf = _experts(plan, xs, moe_gate[i], moe_up[i], moe_down[i])
            x = _combine(d0, d1, x, gate, norm_final, ybuf, apply_norm=l == depth - 1)
    return x.reshape(bsz, seq, d)
```

```python
import functools

import jax
import jax.numpy as jnp
from jax import lax
from jax.experimental import pallas as pl
from jax.experimental.pallas import tpu as pltpu

F32 = jnp.float32
BF16 = jnp.bfloat16

HEAD_SIZE = 64
POOL_WINDOWS = (2, 4, 8, 16)
N_EXPERTS = 8
TOP_K = 2
NORM_EPS = 1e-6
GN_EPS = 64e-5
LANES = 128
VMEM_LIMIT_BYTES = 60 * 1024 * 1024
WKV_CHUNK = 64
POOL_HALO = 16
PREP_ROWS = 256
IN_PROJ_ROW_UNIT = 32
DMA_UNROLL = 8
MOE_ROWS = 1024
MOE_SUB = 256
MOE_FF_TILE = 256
PLAN_FIELDS = 7


def _cparams(n_axes):
    return pltpu.CompilerParams(dimension_semantics=("arbitrary",) * n_axes,
                                vmem_limit_bytes=VMEM_LIMIT_BYTES)


def _dot(a, b):
    return jnp.dot(a.astype(BF16), b.astype(BF16), preferred_element_type=F32)


def _split(x):
    hi = x.astype(BF16)
    lo = (x - hi.astype(F32)).astype(BF16)
    return hi, lo


def _dot_x3(a, b):
    a_hi, a_lo = _split(a)
    b_hi, b_lo = _split(b)
    d = functools.partial(jnp.dot, preferred_element_type=F32)
    return d(a_hi, b_hi) + d(a_hi, b_lo) + d(a_lo, b_hi)


def _hi_lo_rows(w):
    hi = w.astype(BF16)
    return jnp.concatenate([hi, (w - hi.astype(F32)).astype(BF16)], axis=0)


def _dot_x3_presplit(a, w_hi_lo):
    k = w_hi_lo.shape[0] // 2
    a_hi, a_lo = _split(a)
    d = functools.partial(jnp.dot, preferred_element_type=F32)
    return d(a_hi, w_hi_lo[:k]) + d(a_hi, w_hi_lo[k:]) + d(a_lo, w_hi_lo[:k])


def _dot_lhs_split(a, b_bf16):
    a_hi, a_lo = _split(a)
    d = functools.partial(jnp.dot, preferred_element_type=F32)
    return d(a_hi, b_bf16) + d(a_lo, b_bf16)


def _dot_tn(a, b):
    return lax.dot_general(a.astype(BF16), b.astype(BF16), (((0,), (0,)), ((), ())),
                           preferred_element_type=F32)


def _dot_nt(a, b):
    return lax.dot_general(a.astype(BF16), b.astype(BF16), (((1,), (1,)), ((), ())),
                           preferred_element_type=F32)


def _sigmoid(x):
    return 1.0 / (1.0 + jnp.exp(-x))


def _rmsnorm_body(x_ref, g_ref, o_ref):
    x = x_ref[...]
    inv = lax.rsqrt(jnp.mean(x * x, axis=-1, keepdims=True) + NORM_EPS)
    o_ref[...] = (x * inv * g_ref[...]).astype(o_ref.dtype)


def _rmsnorm(x, g, out_dtype, tm=512):
    m, d = x.shape
    return pl.pallas_call(
        _rmsnorm_body,
        grid=(m // tm,),
        in_specs=[pl.BlockSpec((tm, d), lambda i: (i, 0)),
                  pl.BlockSpec((1, d), lambda i: (0, 0))],
        out_specs=pl.BlockSpec((tm, d), lambda i: (i, 0)),
        out_shape=jax.ShapeDtypeStruct((m, d), out_dtype),
        compiler_params=_cparams(1),
        name="rmsnorm",
    )(x, g.reshape(1, d))


def _mm_residual_body(a_ref, w_ref, res_ref, o_ref, wb_ref):
    @pl.when(pl.program_id(1) == 0)
    def _():
        wb_ref[...] = w_ref[...].astype(BF16)

    o_ref[...] = res_ref[...] + jnp.dot(a_ref[...], wb_ref[...], preferred_element_type=F32)


def _matmul_residual(a, w, layer, residual, *, tm, tn):
    m, k = a.shape
    n = w.shape[-1]
    o_spec = pl.BlockSpec((tm, tn), lambda j, i: (i, j))
    return pl.pallas_call(
        _mm_residual_body,
        grid=(n // tn, m // tm),
        in_specs=[pl.BlockSpec((tm, k), lambda j, i: (i, 0)),
                  pl.BlockSpec((None, k, tn), lambda j, i: (layer, 0, j)),
                  o_spec],
        out_specs=o_spec,
        out_shape=jax.ShapeDtypeStruct((m, n), F32),
        scratch_shapes=[pltpu.VMEM((k, tn), BF16)],
        compiler_params=_cparams(2),
        name="matmul_residual",
    )(a, w, residual)


def _norm_swiglu_body(x_ref, g_ref, wg_ref, wu_ref, o_ref, hb_ref):
    @pl.when(pl.program_id(1) == 0)
    def _():
        x = x_ref[...]
        inv = lax.rsqrt(jnp.mean(x * x, axis=-1, keepdims=True) + NORM_EPS)
        hb_ref[...] = (x * inv * g_ref[...]).astype(BF16)

    h = hb_ref[...]
    g = jnp.dot(h, wg_ref[...].astype(BF16), preferred_element_type=F32)
    u = jnp.dot(h, wu_ref[...].astype(BF16), preferred_element_type=F32)
    o_ref[...] = (g * _sigmoid(g) * u).astype(o_ref.dtype)


def _norm_swiglu(x, gain, w_gate, w_up, layer, *, tm, tn):
    m, k = x.shape
    n = w_gate.shape[-1]
    w_spec = pl.BlockSpec((None, k, tn), lambda i, j: (layer, 0, j))
    return pl.pallas_call(
        _norm_swiglu_body,
        grid=(m // tm, n // tn),
        in_specs=[pl.BlockSpec((tm, k), lambda i, j: (i, 0)), pl.BlockSpec((1, k), lambda i, j: (0, 0)),
                  w_spec, w_spec],
        out_specs=pl.BlockSpec((tm, tn), lambda i, j: (i, j)),
        out_shape=jax.ShapeDtypeStruct((m, n), BF16),
        scratch_shapes=[pltpu.VMEM((tm, k), BF16)],
        compiler_params=_cparams(2),
        name="norm_swiglu",
    )(x, gain.reshape(1, k), w_gate, w_up)


def _in_proj_body(*refs, has_extra, n_tiles, keep_cols):
    if has_extra:
        src_ref, x_ref, g_ref, wt_ref, extra_ref, o_ref, hb_ref = refs
    else:
        src_ref, x_ref, g_ref, wt_ref, o_ref, hb_ref = refs
    j = pl.program_id(1)

    @pl.when(j == 0)
    def _():
        x = x_ref[...]
        inv = lax.rsqrt(jnp.mean(x * x, axis=-1, keepdims=True) + NORM_EPS)
        hb_ref[...] = (x * inv * g_ref[...]).astype(BF16)

    h = hb_ref[...]
    out = lax.dot_general(h, wt_ref[0].astype(BF16), (((1,), (1,)), ((), ())), preferred_element_type=F32)
    if has_extra:
        @pl.when(j < n_tiles - 1)
        def _():
            o_ref[...] = out

        @pl.when(j == n_tiles - 1)
        def _():
            col = lax.broadcasted_iota(jnp.int32, out.shape, 1)
            o_ref[...] = jnp.where(col < keep_cols, out, 0.0) + jnp.dot(h, extra_ref[...],
                                                                        preferred_element_type=F32)
    else:
        o_ref[...] = out


def _in_proj(x, gain, wt, layer, src_rows, extra, keep_cols, tm=2048, tn=512):
    m, k = x.shape
    n_tiles = len(src_rows)
    assert all(r % IN_PROJ_ROW_UNIT == 0 for r in src_rows)
    src = jnp.asarray([r // IN_PROJ_ROW_UNIT for r in src_rows], jnp.int32)
    element = lambda n: pl.Element(n)
    tm = min(tm, m)
    in_specs = [pl.BlockSpec((tm, k), lambda i, j, src: (i, 0), pipeline_mode=pl.Buffered(1)),
                pl.BlockSpec((1, k), lambda i, j, src: (0, 0)),
                pl.BlockSpec((element(1), element(tn), element(k)),
                             lambda i, j, src: (layer, src[j] * IN_PROJ_ROW_UNIT, 0))]
    args = [src, x, gain.reshape(1, k), wt]
    if extra is not None:
        in_specs.append(pl.BlockSpec((k, tn), lambda i, j, src: (0, 0), pipeline_mode=pl.Buffered(1)))
        args.append(extra)
    return pl.pallas_call(
        functools.partial(_in_proj_body, has_extra=extra is not None, n_tiles=n_tiles, keep_cols=keep_cols),
        grid_spec=pltpu.PrefetchScalarGridSpec(
            num_scalar_prefetch=1,
            grid=(m // tm, n_tiles),
            in_specs=in_specs,
            out_specs=pl.BlockSpec((tm, tn), lambda i, j, src: (i, j)),
            scratch_shapes=[pltpu.VMEM((tm, k), BF16)]),
        out_shape=jax.ShapeDtypeStruct((m, n_tiles * tn), F32),
        compiler_params=_cparams(2),
        name="in_proj",
    )(*args)


def _head_sum(x, sel, sel_t):
    s = _dot_lhs_split(x, sel)
    return _dot_lhs_split(s, sel_t)


def _token_shift(p, carry_ref, mu):
    rows = p.shape[0]
    row = lax.broadcasted_iota(jnp.int32, p.shape, 0)
    prev = jnp.where(row == 0, carry_ref[...], pltpu.roll(p, 1, axis=0))
    carry_ref[...] = p[rows - 1:rows, :]
    return p + (prev - p) * mu


def _prep_body(*refs, tiles_per_seq, has_vres, emit_v, width):
    (rkv_ref, lora_ref, mu_rkv_ref, mu_lora_ref, w0_ref, a0_ref, kk_ref, ka_ref, rk_ref,
     dup_ref, iup_ref, gup_ref, sel_ref, selt_ref, tri_ref, ones_ref) = refs[:16]
    pos = 16
    if has_vres:
        vfirst_ref, v0_ref, vup_ref = refs[pos:pos + 3]
        pos += 3
    at_o, rt_o, bt_o, kt_o, bp_o, kp_o, vb_o, wc_o, bonus_o, g_o = refs[pos:pos + 10]
    pos += 10
    if emit_v:
        v_o = refs[pos]
        pos += 1
    carry_rkv, carry_lora = refs[pos:]
    R = width

    @pl.when(pl.program_id(0) % tiles_per_seq == 0)
    def _():
        carry_rkv[...] = jnp.zeros_like(carry_rkv)
        carry_lora[...] = jnp.zeros_like(carry_lora)

    z = _token_shift(rkv_ref[...], carry_rkv, mu_rkv_ref[...])
    zl = _token_shift(lora_ref[...], carry_lora, mu_lora_ref[...])
    r, k, v = z[:, :R], z[:, R:2 * R], z[:, 2 * R:]

    decay_logit = w0_ref[...] + _dot_x3_presplit(jnp.tanh(zl[:, :128]), dup_ref[...])
    neg = -decay_logit
    softplus = jnp.maximum(neg, 0.0) + jnp.log(1.0 + jnp.exp(-jnp.abs(neg)))
    lw = -jnp.exp(-softplus - 0.5)
    a = _sigmoid(a0_ref[...] + _dot_x3_presplit(zl[:, :256], iup_ref[...]))
    g_o[...] = _dot(_sigmoid(zl[:, 128:]), gup_ref[...])
    if has_vres:
        mix = _sigmoid(v0_ref[...] + _dot(zl[:, 384:], vup_ref[...]))
        v = v + (vfirst_ref[...] - v) * mix
    if emit_v:
        v_o[...] = v

    sel, selt = sel_ref[...], selt_ref[...]
    kk = k * kk_ref[...]
    norm = jnp.sqrt(_head_sum(kk * kk, sel, selt))
    kk = kk / jnp.maximum(norm, 1e-12)
    k = k * (1.0 + (a - 1.0) * ka_ref[...])
    bonus_o[...] = _head_sum(r * k * rk_ref[...], sel, selt) * v

    d = functools.partial(jnp.dot, preferred_element_type=F32)
    lw_hi = lw.astype(BF16)
    rest = lw - lw_hi.astype(F32)
    lw_mid = rest.astype(BF16)
    lw_lo = (rest - lw_mid.astype(F32)).astype(BF16)
    tri, ones = tri_ref[...], ones_ref[...]
    cum = d(tri, lw_hi) + d(tri, lw_mid) + d(tri, lw_lo)
    total = d(ones, lw_hi) + d(ones, lw_mid) + d(ones, lw_lo)
    w_inv = jnp.exp(-cum)
    w_rem = jnp.exp(total - cum)
    b = kk * a
    wc_o[...] = jnp.exp(total)
    pairs = R // LANES
    for o_ref, val in ((at_o, -kk * jnp.exp(cum - lw)), (rt_o, r * jnp.exp(cum)), (bt_o, b * w_inv),
                       (kt_o, k * w_inv), (bp_o, b * w_rem), (kp_o, k * w_rem), (vb_o, v)):
        val = val.astype(BF16)
        for p in range(pairs):
            o_ref[0, p] = val[:, p * LANES:(p + 1) * LANES]


def _rwkv_prep(p_all, bsz, seq, width, consts, vres, emit_v):
    tm = PREP_ROWS
    m = p_all.shape[0]
    R = width
    lora_w = 512
    lora_blk = (p_all.shape[1] - lora_w) // lora_w
    tiles_per_seq = seq // tm
    row = lambda i: (i, 0)
    fixed = lambda i: (0, 0)
    in_specs = [pl.BlockSpec((tm, 3 * R), row),
                pl.BlockSpec((tm, lora_w), lambda i: (i, lora_blk))]
    args = [p_all, p_all]
    for c in consts:
        in_specs.append(pl.BlockSpec(c.shape, fixed))
        args.append(c)
    if vres is not None:
        vfirst, v0, vup = vres
        in_specs += [pl.BlockSpec((tm, R), row), pl.BlockSpec(v0.shape, fixed), pl.BlockSpec(vup.shape, fixed)]
        args += [vfirst, v0, vup]
    pairs = R // LANES
    pm_spec = pl.BlockSpec((1, pairs, tm, LANES), lambda i: (i // tiles_per_seq, 0, i % tiles_per_seq, 0))
    pm_shape = jax.ShapeDtypeStruct((bsz, pairs, seq, LANES), BF16)
    nat_spec = pl.BlockSpec((tm, R), row)
    nat_shape = jax.ShapeDtypeStruct((m, R), F32)
    n_nat = 4 if emit_v else 3
    return pl.pallas_call(
        functools.partial(_prep_body, tiles_per_seq=tiles_per_seq, has_vres=vres is not None, emit_v=emit_v,
                          width=R),
        grid=(m // tm,),
        in_specs=in_specs,
        out_specs=[pm_spec] * 7 + [nat_spec] * n_nat,
        out_shape=[pm_shape] * 7 + [nat_shape] * n_nat,
        scratch_shapes=[pltpu.VMEM((1, 3 * R), F32), pltpu.VMEM((1, lora_w), F32)],
        compiler_params=_cparams(1),
        name="rwkv_prep",
    )(*args)


def _wkv_body(at_ref, rt_ref, bt_ref, kt_ref, bp_ref, kp_ref, v_ref, wc_ref, y_ref, state_ref, *,
              batches, pairs, chunks):
    C = WKV_CHUNK
    hs = HEAD_SIZE

    @pl.when(pl.program_id(1) == 0)
    def _():
        state_ref[...] = jnp.zeros_like(state_ref)

    def head_masks(rows, width):
        lane = lax.broadcasted_iota(jnp.int32, (rows, width), 1)
        first = (lane % LANES) < hs
        return first, jnp.logical_not(first)

    def stack_heads(x):
        m0, m1 = head_masks(*x.shape)
        zero = jnp.zeros_like(x)
        return jnp.concatenate([jnp.where(m0, x, zero), jnp.where(m1, x, zero)], axis=0)

    row = lax.broadcasted_iota(jnp.int32, (C, LANES), 0)
    col = lax.broadcasted_iota(jnp.int32, (C, LANES), 1) % hs
    strict, incl = row > col, row >= col
    sq_r = lax.broadcasted_iota(jnp.int32, (LANES, LANES), 0)
    sq_c = lax.broadcasted_iota(jnp.int32, (LANES, LANES), 1)
    same_head = (sq_r // hs) == (sq_c // hs)
    eye = sq_r == sq_c
    levels = C.bit_length() - 1
    dot = functools.partial(jnp.dot, preferred_element_type=F32)
    P = range(batches * pairs)

    def step(c, carry):
        sl = pl.ds(pl.multiple_of(c * C, C), C)
        at = [at_ref[p // pairs, p % pairs, sl, :] for p in P]
        rt = [rt_ref[p // pairs, p % pairs, sl, :] for p in P]
        v = [v_ref[p // pairs, p % pairs, sl, :] for p in P]
        vs = [stack_heads(v[p]) for p in P]
        lhs = [jnp.concatenate([at[p], rt[p]], axis=0) for p in P]
        rhs = [jnp.concatenate([stack_heads(bt_ref[p // pairs, p % pairs, sl, :]), stack_heads(kt_ref[p // pairs, p % pairs, sl, :])], axis=0)
               for p in P]
        abk = [_dot_nt(lhs[p], rhs[p]) for p in P]
        n = [jnp.where(strict, abk[p][:C, :LANES], 0.0) for p in P]
        a_rb = [jnp.where(incl, abk[p][C:, :LANES], 0.0) for p in P]
        a_ak = [jnp.where(strict, abk[p][:C, LANES:], 0.0) for p in P]
        a_rk = [jnp.where(incl, abk[p][C:, LANES:], 0.0) for p in P]
        av = [dot(a_ak[p].astype(BF16), vs[p]) for p in P]
        x = [jnp.concatenate([at[p].astype(F32), av[p]], axis=1) for p in P]
        for lvl in range(levels):
            last = lvl + 1 == levels
            for p in P:
                z = x[p] if last else jnp.concatenate([n[p], x[p]], axis=1)
                out = dot(n[p].astype(BF16), stack_heads(z.astype(BF16)))
                if last:
                    x[p] = x[p] + out
                else:
                    n[p] = out[:, :LANES]
                    x[p] = x[p] + out[:, LANES:]
        xb = [x[p].astype(BF16) for p in P]
        zeros_c = jnp.zeros((C, LANES), BF16)
        bx = [_dot_tn(jnp.concatenate([bp_ref[p // pairs, p % pairs, sl, :], kp_ref[p // pairs, p % pairs, sl, :]], axis=0),
                      jnp.concatenate([xb[p], jnp.concatenate([zeros_c, v[p]], axis=1)], axis=0)) for p in P]
        zeros_2c = jnp.zeros((2 * C, LANES), BF16)
        rx = [dot(jnp.concatenate([a_rb[p], a_rk[p]], axis=1).astype(BF16),
                  jnp.concatenate([stack_heads(xb[p]), jnp.concatenate([zeros_2c, vs[p]], axis=1)], axis=0))
              for p in P]
        for p in P:
            state = state_ref[p]
            lanes = slice((p % pairs) * LANES, (p % pairs + 1) * LANES)
            wc = wc_ref[p // pairs, pl.ds(pl.multiple_of(c * C, C), 1), lanes]
            trans = jnp.where(same_head, bx[p][:, :LANES], 0.0) + jnp.where(eye, wc, 0.0)
            inject = jnp.where(same_head, bx[p][:, LANES:], 0.0)
            r_eff = rt[p].astype(F32) + rx[p][:, :LANES]
            y_ref[p // pairs, p % pairs, sl, :] = _dot(r_eff, state) + rx[p][:, LANES:]
            state_ref[p] = _dot_x3(trans, state) + inject
        return carry

    lax.fori_loop(0, chunks, step, 0)


def _wkv(at, rt, bt, kt, bp, kp, v, wc, pairs=8, span=512):
    bsz, n_pairs, seq, _ = at.shape
    spec = pl.BlockSpec((bsz, pairs, span, LANES), lambda g, t: (0, g, t, 0))
    wc_spec = pl.BlockSpec((bsz, span, pairs * LANES), lambda g, t: (0, t, g))
    return pl.pallas_call(
        functools.partial(_wkv_body, batches=bsz, pairs=pairs, chunks=span // WKV_CHUNK),
        grid=(n_pairs // pairs, seq // span),
        in_specs=[spec] * 7 + [wc_spec],
        out_specs=spec,
        out_shape=jax.ShapeDtypeStruct((bsz, n_pairs, seq, LANES), F32),
        scratch_shapes=[pltpu.VMEM((bsz * pairs, LANES, LANES), F32)],
        compiler_params=_cparams(2),
        name="wkv7",
    )(at, rt, bt, kt, bp, kp, v, wc.reshape(bsz, seq, wc.shape[1]))


def _mixer_outputs(y_ref, bonus_ref, g_ref, u_ref, lnw_ref, lnb_ref, sel_ref, selt_ref, pw_ref, ps_ref, ext_ref,
                   *, tiles_per_seq, tm, group_width):
    seq_tile = pl.program_id(0) % tiles_per_seq

    @pl.when(seq_tile == 0)
    def _():
        ext_ref[0:POOL_HALO, :] = jnp.zeros((POOL_HALO, ext_ref.shape[1]), F32)

    u = u_ref[...]
    ext_ref[POOL_HALO:POOL_HALO + tm, :] = u
    t_pos = seq_tile * tm + lax.broadcasted_iota(jnp.int32, (tm, group_width), 0)
    groups = []
    for gi, w in enumerate(POOL_WINDOWS):
        lanes = slice(gi * group_width, (gi + 1) * group_width)
        acc = u[:, lanes]
        for j in range(1, w):
            acc = acc + ext_ref[POOL_HALO - j:POOL_HALO - j + tm, lanes]
        count = jnp.minimum(t_pos + 1, w).astype(F32)
        mixed = acc / count - u[:, lanes]
        yg = _dot_x3(mixed, pw_ref[gi])
        groups.append((yg * ps_ref[:, lanes]).astype(BF16))
    ext_ref[0:POOL_HALO, :] = u[tm - POOL_HALO:, :]

    sel, selt = sel_ref[...], selt_ref[...]
    inv_n = 1.0 / HEAD_SIZE
    y = jnp.concatenate([y_ref[0, p] for p in range(y_ref.shape[1])], axis=1)
    mean = _head_sum(y, sel, selt) * inv_n
    yc = y - mean
    var = _head_sum(yc * yc, sel, selt) * inv_n
    yn = yc * lax.rsqrt(var + GN_EPS) * lnw_ref[...] + lnb_ref[...]
    ya = ((yn + bonus_ref[...]) * g_ref[...]).astype(BF16)
    return ya, jnp.concatenate(groups, axis=1)


def _post_mix_body(y_ref, bonus_ref, g_ref, u_ref, ga_ref, gb_ref, x_ref, lnw_ref, lnb_ref, sel_ref, selt_ref,
                   pw_ref, ps_ref, pa_ref, pb_ref, wo_ref, o_ref, ext_ref, **static):
    ya, yb = _mixer_outputs(y_ref, bonus_ref, g_ref, u_ref, lnw_ref, lnb_ref, sel_ref, selt_ref, pw_ref, ps_ref,
                            ext_ref, **static)
    ma = jnp.dot(ya, pa_ref[...], preferred_element_type=F32)
    mb = jnp.dot(yb, pb_ref[...], preferred_element_type=F32)
    merged = (_sigmoid(ga_ref[...]) * ma + _sigmoid(gb_ref[...]) * mb).astype(BF16)
    o_ref[...] = x_ref[...] + jnp.dot(merged, wo_ref[...], preferred_element_type=F32)


def _post_mix(y, bonus, g, p_all, pool_col, ga_col, gb_col, x, consts, weights, seq, tm=256):
    m, R = bonus.shape
    d = x.shape[1]
    lnw, lnb, sel, selt, pw, ps = consts
    tiles_per_seq = seq // tm
    row = lambda i: (i, 0)
    fixed2 = lambda i: (0, 0)
    resident = lambda w: pl.BlockSpec(w.shape, fixed2, pipeline_mode=pl.Buffered(1))
    in_specs = [pl.BlockSpec((1, y.shape[1], tm, LANES), lambda i: (i // tiles_per_seq, 0, i % tiles_per_seq, 0)),
                pl.BlockSpec((tm, R), row), pl.BlockSpec((tm, R), row),
                pl.BlockSpec((tm, R), lambda i: (i, pool_col // R)),
                pl.BlockSpec((tm, d), lambda i: (i, ga_col // d)),
                pl.BlockSpec((tm, d), lambda i: (i, gb_col // d)),
                pl.BlockSpec((tm, d), row)]
    in_specs += [pl.BlockSpec(c.shape, fixed2) for c in (lnw, lnb, sel, selt)]
    in_specs += [pl.BlockSpec(pw.shape, lambda i: (0, 0, 0)), pl.BlockSpec(ps.shape, fixed2)]
    in_specs += [resident(w) for w in weights]
    return pl.pallas_call(
        functools.partial(_post_mix_body, tiles_per_seq=tiles_per_seq, tm=tm, group_width=R // len(POOL_WINDOWS)),
        grid=(m // tm,),
        in_specs=in_specs,
        out_specs=pl.BlockSpec((tm, d), row),
        out_shape=jax.ShapeDtypeStruct((m, d), F32),
        scratch_shapes=[pltpu.VMEM((POOL_HALO + tm, R), F32)],
        compiler_params=_cparams(1),
        name="post_mix",
    )(y, bonus, g, p_all, p_all, p_all, x, lnw, lnb, sel, selt, pw, ps, *weights)


def _router_body(x_ref, g_ref, wr_ref, tri_ref, h_o, idx_o, gate_o, cnt_o, carry_ref):
    @pl.when(pl.program_id(0) == 0)
    def _():
        carry_ref[...] = jnp.zeros_like(carry_ref)

    x = x_ref[...]
    inv = lax.rsqrt(jnp.mean(x * x, axis=-1, keepdims=True) + NORM_EPS)
    h = x * inv * g_ref[...]
    h_o[...] = h.reshape(h_o.shape)
    logits = _dot_x3_presplit(h, wr_ref[...])
    lane = lax.broadcasted_iota(jnp.int32, logits.shape, 1)
    neg_inf = jnp.float32(-jnp.inf)
    l1 = jnp.where(lane < N_EXPERTS, logits, neg_inf)
    m1 = jnp.max(l1, axis=-1, keepdims=True)
    i1 = jnp.min(jnp.where(l1 == m1, lane, LANES), axis=-1, keepdims=True)
    l2 = jnp.where(lane == i1, neg_inf, l1)
    m2 = jnp.max(l2, axis=-1, keepdims=True)
    i2 = jnp.min(jnp.where(l2 == m2, lane, LANES), axis=-1, keepdims=True)
    e = jnp.exp(m2 - m1)
    g1 = 1.0 / (1.0 + e)
    gate_o[...] = jnp.where(lane == 0, g1, jnp.where(lane == 1, e * g1, 0.0))

    first, second = lane == i1, lane == i2
    picked = jnp.where(jnp.logical_or(first, second), 1.0, 0.0)
    before = carry_ref[...] + jnp.dot(tri_ref[...], picked.astype(BF16), preferred_element_type=F32)
    rank1 = jnp.sum(jnp.where(first, before, 0.0), axis=-1, keepdims=True).astype(jnp.int32)
    rank2 = jnp.sum(jnp.where(second, before, 0.0), axis=-1, keepdims=True).astype(jnp.int32)
    idx_o[...] = jnp.where(lane == 0, i1, jnp.where(lane == 1, i2, jnp.where(lane == 2, rank1,
                                                                              jnp.where(lane == 3, rank2, 0))))
    carry_ref[...] += jnp.sum(picked, axis=0, keepdims=True)
    cnt_o[...] = jnp.broadcast_to(carry_ref[...], cnt_o.shape).astype(jnp.int32)


def _router(x, g, wr_pad, tm=256):
    m, d = x.shape
    row = lambda i: (i, 0)
    t_idx = jnp.arange(tm, dtype=jnp.int32)
    tri_strict = (t_idx[None, :] < t_idx[:, None]).astype(BF16)
    return pl.pallas_call(
        _router_body,
        grid=(m // tm,),
        in_specs=[pl.BlockSpec((tm, d), row), pl.BlockSpec((1, d), lambda i: (0, 0)),
                  pl.BlockSpec(wr_pad.shape, lambda i: (0, 0)), pl.BlockSpec((tm, tm), lambda i: (0, 0))],
        out_specs=[pl.BlockSpec((tm, d // LANES, LANES), lambda i: (i, 0, 0)),
                   pl.BlockSpec((tm, LANES), row), pl.BlockSpec((tm, LANES), row),
                   pl.BlockSpec((8, LANES), lambda i: (0, 0))],
        out_shape=[jax.ShapeDtypeStruct((m, d // LANES, LANES), F32), jax.ShapeDtypeStruct((m, LANES), jnp.int32),
                   jax.ShapeDtypeStruct((m, LANES), F32), jax.ShapeDtypeStruct((8, LANES), jnp.int32)],
        scratch_shapes=[pltpu.VMEM((1, LANES), F32)],
        compiler_params=_cparams(1),
        name="router",
    )(x, g.reshape(1, d), wr_pad, tri_strict)


def _gather_body(tok_ref, used_ref, h_hbm, o_ref, buf_ref, sems, *, tg, n_steps):
    i = pl.program_id(0)

    def row_copy(step, r):
        slot = step % 2
        return pltpu.make_async_copy(h_hbm.at[pl.ds(tok_ref[step * tg + r], 1)], buf_ref.at[slot, pl.ds(r, 1)],
                                     sems.at[slot])

    def fetch(step):
        @pl.when(jnp.logical_and(step < n_steps, step * tg < used_ref[0]))
        def _():
            def start(q, c):
                for u in range(DMA_UNROLL):
                    row_copy(step, q * DMA_UNROLL + u).start()
                return c
            lax.fori_loop(0, tg // DMA_UNROLL, start, 0)

    @pl.when(i == 0)
    def _():
        fetch(i)

    fetch(i + 1)

    @pl.when(i * tg < used_ref[0])
    def _():
        pltpu.make_async_copy(h_hbm.at[pl.ds(0, tg)], buf_ref.at[i % 2], sems.at[i % 2]).wait()
        o_ref[...] = buf_ref[i % 2].reshape(o_ref.shape).astype(o_ref.dtype)

    @pl.when(i * tg >= used_ref[0])
    def _():
        o_ref[...] = jnp.zeros_like(o_ref)


def _gather_rows(tok, used_rows, h3, tg=256):
    p = tok.shape[0]
    _, nc, lanes = h3.shape
    n_steps = p // tg
    return pl.pallas_call(
        functools.partial(_gather_body, tg=tg, n_steps=n_steps),
        grid_spec=pltpu.PrefetchScalarGridSpec(
            num_scalar_prefetch=2,
            grid=(n_steps,),
            in_specs=[pl.BlockSpec(memory_space=pl.ANY)],
            out_specs=pl.BlockSpec((tg, nc * lanes), lambda i, tok, used: (i, 0)),
            scratch_shapes=[pltpu.VMEM((2, tg, nc, lanes), F32), pltpu.SemaphoreType.DMA((2,))]),
        out_shape=jax.ShapeDtypeStruct((p, nc * lanes), BF16),
        compiler_params=_cparams(1),
        name="moe_gather",
    )(tok, used_rows, h3)


def _experts_body(plan_ref, x_ref, wg_ref, wu_ref, wd_ref, o_ref, *, n_items):
    w = pl.program_id(0)
    j = pl.program_id(1)
    field = lambda f: plan_ref[f * n_items + w]
    lo, hi, zero, live = field(3), field(4), field(5), field(6)
    n_sub = MOE_ROWS // MOE_SUB

    @pl.when(jnp.logical_and(zero == 1, j == 0))
    def _():
        o_ref[...] = jnp.zeros_like(o_ref)

    def rows(sl, wg, wu, wd):
        x = x_ref[sl, :]
        g = jnp.dot(x, wg, preferred_element_type=F32)
        u = jnp.dot(x, wu, preferred_element_type=F32)
        h = (g * _sigmoid(g) * u).astype(BF16)
        o_ref[sl, :] += jnp.dot(h, wd, preferred_element_type=F32)

    for first in range(n_sub):
        for end in range(first + 1, n_sub + 1):
            @pl.when(jnp.logical_and(live == 1, jnp.logical_and(lo == first, hi == end)))
            def _():
                rows(slice(first * MOE_SUB, end * MOE_SUB),
                     wg_ref[...].astype(BF16), wu_ref[...].astype(BF16), wd_ref[...].astype(BF16))


def _experts(plan, xs, w_gate, w_up, w_down):
    p, d = xs.shape
    f = w_gate.shape[2]
    tn = MOE_FF_TILE
    nj = f // tn
    n_items = plan.shape[0] // PLAN_FIELDS
    fld = lambda plan, f_idx, w: plan[f_idx * n_items + w]
    col = lambda plan, w, j: jnp.where(fld(plan, 6, w) == 1, j, nj - 1)
    return pl.pallas_call(
        functools.partial(_experts_body, n_items=n_items),
        grid_spec=pltpu.PrefetchScalarGridSpec(
            num_scalar_prefetch=1,
            grid=(n_items, nj),
            in_specs=[pl.BlockSpec((MOE_ROWS, d), lambda w, j, plan: (fld(plan, 1, w), 0)),
                      pl.BlockSpec((None, d, tn), lambda w, j, plan: (fld(plan, 2, w), 0, col(plan, w, j))),
                      pl.BlockSpec((None, d, tn), lambda w, j, plan: (fld(plan, 2, w), 0, col(plan, w, j))),
                      pl.BlockSpec((None, tn, d), lambda w, j, plan: (fld(plan, 2, w), col(plan, w, j), 0))],
            out_specs=pl.BlockSpec((MOE_ROWS, d), lambda w, j, plan: (fld(plan, 0, w), 0))),
        out_shape=jax.ShapeDtypeStruct((p, d), F32),
        compiler_params=_cparams(2),
        name="moe_experts",
    )(plan, xs, w_gate, w_up, w_down)


def _combine_body(d0_ref, d1_ref, x_ref, gate_ref, g_ref, y_hbm, o_ref, buf_ref, sems, *, tg, n_steps, apply_norm):
    i = pl.program_id(0)

    def row_copy(step, r, d_ref, pick):
        slot = step % 2
        return pltpu.make_async_copy(y_hbm.at[pl.ds(d_ref[step * tg + r], 1), :],
                                     buf_ref.at[slot, pick, pl.ds(r, 1), :], sems.at[slot])

    def fetch(step):
        @pl.when(step < n_steps)
        def _():
            def start(q, c):
                for u in range(DMA_UNROLL):
                    row_copy(step, q * DMA_UNROLL + u, d0_ref, 0).start()
                    row_copy(step, q * DMA_UNROLL + u, d1_ref, 1).start()
                return c
            lax.fori_loop(0, tg // DMA_UNROLL, start, 0)

    @pl.when(i == 0)
    def _():
        fetch(i)

    fetch(i + 1)
    slot = i % 2
    for pick in range(2):
        pltpu.make_async_copy(y_hbm.at[pl.ds(0, tg), :], buf_ref.at[slot, pick], sems.at[slot]).wait()
    gate = gate_ref[...]
    x = x_ref[...] + gate[:, 0:1] * buf_ref[slot, 0] + gate[:, 1:2] * buf_ref[slot, 1]
    if apply_norm:
        x = x * lax.rsqrt(jnp.mean(x * x, axis=-1, keepdims=True) + NORM_EPS) * g_ref[...]
    o_ref[...] = x


def _combine(d0, d1, x, gate, g, ybuf, apply_norm, tg=256):
    m, d = x.shape
    row = lambda i, d0, d1: (i, 0)
    return pl.pallas_call(
        functools.partial(_combine_body, tg=tg, n_steps=m // tg, apply_norm=apply_norm),
        grid_spec=pltpu.PrefetchScalarGridSpec(
            num_scalar_prefetch=2,
            grid=(m // tg,),
            in_specs=[pl.BlockSpec((tg, d), row), pl.BlockSpec((tg, LANES), row),
                      pl.BlockSpec((1, d), lambda i, d0, d1: (0, 0)),
                      pl.BlockSpec(memory_space=pl.ANY)],
            out_specs=pl.BlockSpec((tg, d), row),
            scratch_shapes=[pltpu.VMEM((2, 2, tg, d), F32), pltpu.SemaphoreType.DMA((2,))]),
        out_shape=jax.ShapeDtypeStruct((m, d), F32),
        compiler_params=_cparams(1),
        name="moe_combine",
    )(d0, d1, x, gate, g.reshape(1, d), ybuf)


def _moe_plan(top_e, rank, counts):
    i32 = jnp.int32
    n_tok = top_e.shape[0]
    n_slot = n_tok * TOP_K
    flat_e = top_e.reshape(n_slot)
    rank = rank.reshape(n_slot)
    padded = (counts + MOE_SUB - 1) // MOE_SUB * MOE_SUB
    pend = jnp.cumsum(padded)
    pstart = pend - padded
    dest = pstart[flat_e] + rank
    n_blocks = -(-(n_slot + N_EXPERTS * MOE_SUB) // MOE_ROWS)
    tok = jnp.zeros((n_blocks * MOE_ROWS,), i32).at[dest].set(jnp.arange(n_slot, dtype=i32) // TOP_K)
    used_rows = pend[-1]
    used_blocks = (used_rows + MOE_ROWS - 1) // MOE_ROWS

    n_items = n_blocks + N_EXPERTS
    blk_lo = (jnp.arange(n_blocks, dtype=i32) * MOE_ROWS)[:, None]
    lo_row = jnp.maximum(pstart[None, :], blk_lo)
    hi_row = jnp.minimum(pend[None, :], blk_lo + MOE_ROWS)
    valid = (hi_row > lo_row).reshape(-1)
    n_live = jnp.sum(valid.astype(i32))
    pos = jnp.where(valid, jnp.cumsum(valid.astype(i32)) - 1, n_items)
    compact = lambda v: jnp.zeros((n_items,), i32).at[pos].set(v.reshape(-1).astype(i32), mode="drop")
    blk_c = compact(jnp.broadcast_to(jnp.arange(n_blocks, dtype=i32)[:, None], lo_row.shape))
    exp_c = compact(jnp.broadcast_to(jnp.arange(N_EXPERTS, dtype=i32)[None, :], lo_row.shape))
    lo_c = compact((lo_row - blk_lo) // MOE_SUB)
    hi_c = compact((hi_row - blk_lo) // MOE_SUB)
    first_c = compact(lo_row == blk_lo)

    w = jnp.arange(n_items, dtype=i32)
    live = w < n_live
    tail_blk = used_blocks + (w - n_live)
    last = n_live - 1
    out_blk = jnp.where(live, blk_c, jnp.minimum(tail_blk, n_blocks - 1))
    x_blk = jnp.where(live, blk_c, blk_c[last])
    expert = jnp.where(live, exp_c, exp_c[last])
    zero = jnp.where(live, first_c, (tail_blk < n_blocks).astype(i32))
    plan = jnp.concatenate([out_blk, x_blk, expert, lo_c, hi_c, zero, live.astype(i32)]).astype(i32)
    dest = dest.reshape(n_tok, TOP_K).astype(i32)
    return tok, used_rows.astype(i32).reshape(1), plan, dest[:, 0], dest[:, 1]


def kernel(x, norm_mix, w_in, shift_mu, decay_w0, decay_up, iclr_a0, iclr_up, outgate_up, k_k, k_a, r_k, lnx_w, lnx_b, vres_down, vres_mu, vres_up, vres_v0, pool_w, pool_scale, proj_a, proj_b, w_o, norm_ffn, ffn_gate, ffn_up, ffn_down, router, moe_gate, moe_up, moe_down, norm_final):
    bsz, seq, d = x.shape
    depth = w_in.shape[0]
    R = decay_w0.shape[1]
    dl, il, gl = decay_up.shape[1], iclr_up.shape[1], outgate_up.shape[1]
    vl = vres_up.shape[1]
    assert (dl, il, vl, gl) == (96, 96, 64, 256) and 3 * R + dl + il + gl == shift_mu.shape[1]
    m = bsz * seq
    x = x.reshape(m, d)
    row = lambda t: t.reshape(1, -1)

    head_of_lane = jnp.arange(R, dtype=jnp.int32) // HEAD_SIZE
    sel = (head_of_lane[:, None] == jnp.arange(LANES, dtype=jnp.int32)[None, :]).astype(BF16)
    sel_t = sel.T
    t_idx = jnp.arange(PREP_ROWS, dtype=jnp.int32)
    same_chunk = (t_idx[:, None] // WKV_CHUNK) == (t_idx[None, :] // WKV_CHUNK)
    ones_bd = same_chunk.astype(BF16)
    tri_bd = (same_chunk & (t_idx[None, :] <= t_idx[:, None])).astype(BF16)

    w_in_t = jnp.swapaxes(w_in, 1, 2)
    v_first = None
    for l in range(depth):
        c0 = 3 * R
        c3 = c0 + dl + il + gl
        tn = 512
        lora_w = c3 - c0 + vl
        assert c0 % tn == 0 and (w_in.shape[2] - c3) % tn == 0 and lora_w == tn
        src_rows = list(range(0, c0, tn)) + list(range(c3, w_in.shape[2], tn)) + [c0]
        vd_mu = vres_mu[l - 1] if l > 0 else jnp.zeros((vl,), F32)
        extra = None
        if l > 0:
            extra = jnp.concatenate([jnp.zeros((d, c3 - c0), F32), vres_down[l - 1]], axis=1).astype(BF16)
        mu = shift_mu[l]
        mu_rkv = row(mu[:c0])
        mu_lora = row(jnp.concatenate([mu[c0:c3], vd_mu]))
        pool_col = c0
        ga_col = c0 + R
        gb_col = ga_col + d
        zeros = lambda n: jnp.zeros((n, R), F32)
        dup = _hi_lo_rows(jnp.concatenate([decay_up[l], zeros(128 - dl)], axis=0))
        iup = _hi_lo_rows(jnp.concatenate([zeros(dl), iclr_up[l], zeros(256 - dl - il)], axis=0))
        gup = jnp.concatenate([zeros(dl + il - 128), outgate_up[l], zeros(vl)], axis=0).astype(BF16)

        p_all = _in_proj(x, norm_mix[l], w_in_t, l, src_rows, extra, c3 - c0, tn=tn)

        consts = [mu_rkv, mu_lora, row(decay_w0[l]), row(iclr_a0[l]), row(k_k[l]), row(k_a[l]), row(r_k[l]),
                  dup, iup, gup, sel, sel_t, tri_bd, ones_bd]
        vres = None
        if l > 0:
            vup = jnp.concatenate([zeros(128 - vl), vres_up[l - 1]], axis=0).astype(BF16)
            vres = (v_first, row(vres_v0[l - 1]), vup)
        emit_v = l == 0 and depth > 1
        outs = _rwkv_prep(p_all, bsz, seq, R, consts, vres, emit_v)
        wc, bonus, g = outs[7:10]
        if emit_v:
            v_first = outs[10]
        y = _wkv(*outs[:7], wc)

        assert pool_col % R == 0 and ga_col % d == 0 and gb_col % d == 0
        x = _post_mix(y, bonus, g, p_all, pool_col, ga_col, gb_col, x,
                      [row(lnx_w[l]), row(lnx_b[l]), sel, sel_t, pool_w[l], row(pool_scale[l])],
                      [proj_a[l].astype(BF16), proj_b[l].astype(BF16), w_o[l].astype(BF16)], seq)

        i = l // 2
        if l % 2 == 0:
            hidden = _norm_swiglu(x, norm_ffn[l], ffn_gate, ffn_up, i, tm=1024, tn=512)
            x = _matmul_residual(hidden, ffn_down, i, x, tm=512, tn=512)
            if l == depth - 1:
                x = _rmsnorm(x, norm_final, F32)
        else:
            wr_pad = _hi_lo_rows(jnp.concatenate([router[i], jnp.zeros((d, LANES - N_EXPERTS), F32)], axis=1))
            h3, idx, gate, counts = _router(x, norm_ffn[l], wr_pad)
            tok, used_rows, plan, d0, d1 = _moe_plan(idx[:, :TOP_K], idx[:, TOP_K:2 * TOP_K], counts[0, :N_EXPERTS])
            xs = _gather_rows(tok, used_rows, h3)
            ybuf = _experts(plan, xs, moe_gate[i], moe_up[i], moe_down[i])
            x = _combine(d0, d1, x, gate, norm_final, ybuf, apply_norm=l == depth - 1)
    return x.reshape(bsz, seq, d)
```

```python
import functools

import jax
import jax.numpy as jnp
from jax import lax
from jax.experimental import pallas as pl
from jax.experimental.pallas import tpu as pltpu

F32 = jnp.float32
BF16 = jnp.bfloat16

HEAD_SIZE = 64
POOL_WINDOWS = (2, 4, 8, 16)
N_EXPERTS = 8
TOP_K = 2
NORM_EPS = 1e-6
GN_EPS = 64e-5
LANES = 128
VMEM_LIMIT_BYTES = 60 * 1024 * 1024
WKV_CHUNK = 64
POOL_HALO = 16
PREP_ROWS = 256
IN_PROJ_ROW_UNIT = 32
DMA_UNROLL = 8
MOE_ROWS = 1024
MOE_SUB = 256
MOE_FF_TILE = 256
PLAN_FIELDS = 7


def _cparams(n_axes):
    return pltpu.CompilerParams(dimension_semantics=("arbitrary",) * n_axes,
                                vmem_limit_bytes=VMEM_LIMIT_BYTES)


def _dot(a, b):
    return jnp.dot(a.astype(BF16), b.astype(BF16), preferred_element_type=F32)


def _split(x):
    hi = x.astype(BF16)
    lo = (x - hi.astype(F32)).astype(BF16)
    return hi, lo


def _dot_x3(a, b):
    a_hi, a_lo = _split(a)
    b_hi, b_lo = _split(b)
    d = functools.partial(jnp.dot, preferred_element_type=F32)
    return d(a_hi, b_hi) + d(a_hi, b_lo) + d(a_lo, b_hi)


def _hi_lo_rows(w):
    hi = w.astype(BF16)
    return jnp.concatenate([hi, (w - hi.astype(F32)).astype(BF16)], axis=0)


def _dot_x3_presplit(a, w_hi_lo):
    k = w_hi_lo.shape[0] // 2
    a_hi, a_lo = _split(a)
    d = functools.partial(jnp.dot, preferred_element_type=F32)
    return d(a_hi, w_hi_lo[:k]) + d(a_hi, w_hi_lo[k:]) + d(a_lo, w_hi_lo[:k])


def _dot_lhs_split(a, b_bf16):
    a_hi, a_lo = _split(a)
    d = functools.partial(jnp.dot, preferred_element_type=F32)
    return d(a_hi, b_bf16) + d(a_lo, b_bf16)


def _dot_tn(a, b):
    return lax.dot_general(a.astype(BF16), b.astype(BF16), (((0,), (0,)), ((), ())),
                           preferred_element_type=F32)


def _dot_nt(a, b):
    return lax.dot_general(a.astype(BF16), b.astype(BF16), (((1,), (1,)), ((), ())),
                           preferred_element_type=F32)


def _sigmoid(x):
    return 1.0 / (1.0 + jnp.exp(-x))


def _rmsnorm_body(x_ref, g_ref, o_ref):
    x = x_ref[...]
    inv = lax.rsqrt(jnp.mean(x * x, axis=-1, keepdims=True) + NORM_EPS)
    o_ref[...] = (x * inv * g_ref[...]).astype(o_ref.dtype)


def _rmsnorm(x, g, out_dtype, tm=512):
    m, d = x.shape
    return pl.pallas_call(
        _rmsnorm_body,
        grid=(m // tm,),
        in_specs=[pl.BlockSpec((tm, d), lambda i: (i, 0)),
                  pl.BlockSpec((1, d), lambda i: (0, 0))],
        out_specs=pl.BlockSpec((tm, d), lambda i: (i, 0)),
        out_shape=jax.ShapeDtypeStruct((m, d), out_dtype),
        compiler_params=_cparams(1),
        name="rmsnorm",
    )(x, g.reshape(1, d))


def _mm_residual_body(a_ref, w_ref, res_ref, o_ref, wb_ref):
    @pl.when(pl.program_id(1) == 0)
    def _():
        wb_ref[...] = w_ref[...].astype(BF16)

    o_ref[...] = res_ref[...] + jnp.dot(a_ref[...], wb_ref[...], preferred_element_type=F32)


def _matmul_residual(a, w, layer, residual, *, tm, tn):
    m, k = a.shape
    n = w.shape[-1]
    o_spec = pl.BlockSpec((tm, tn), lambda j, i: (i, j))
    return pl.pallas_call(
        _mm_residual_body,
        grid=(n // tn, m // tm),
        in_specs=[pl.BlockSpec((tm, k), lambda j, i: (i, 0)),
                  pl.BlockSpec((None, k, tn), lambda j, i: (layer, 0, j)),
                  o_spec],
        out_specs=o_spec,
        out_shape=jax.ShapeDtypeStruct((m, n), F32),
        scratch_shapes=[pltpu.VMEM((k, tn), BF16)],
        compiler_params=_cparams(2),
        name="matmul_residual",
    )(a, w, residual)


def _norm_swiglu_body(x_ref, g_ref, wg_ref, wu_ref, o_ref, hb_ref):
    @pl.when(pl.program_id(1) == 0)
    def _():
        x = x_ref[...]
        inv = lax.rsqrt(jnp.mean(x * x, axis=-1, keepdims=True) + NORM_EPS)
        hb_ref[...] = (x * inv * g_ref[...]).astype(BF16)

    h = hb_ref[...]
    g = jnp.dot(h, wg_ref[...].astype(BF16), preferred_element_type=F32)
    u = jnp.dot(h, wu_ref[...].astype(BF16), preferred_element_type=F32)
    o_ref[...] = (g * _sigmoid(g) * u).astype(o_ref.dtype)


def _norm_swiglu(x, gain, w_gate, w_up, layer, *, tm, tn):
    m, k = x.shape
    n = w_gate.shape[-1]
    w_spec = pl.BlockSpec((None, k, tn), lambda i, j: (layer, 0, j))
    return pl.pallas_call(
        _norm_swiglu_body,
        grid=(m // tm, n // tn),
        in_specs=[pl.BlockSpec((tm, k), lambda i, j: (i, 0)), pl.BlockSpec((1, k), lambda i, j: (0, 0)),
                  w_spec, w_spec],
        out_specs=pl.BlockSpec((tm, tn), lambda i, j: (i, j)),
        out_shape=jax.ShapeDtypeStruct((m, n), BF16),
        scratch_shapes=[pltpu.VMEM((tm, k), BF16)],
        compiler_params=_cparams(2),
        name="norm_swiglu",
    )(x, gain.reshape(1, k), w_gate, w_up)


def _in_proj_body(*refs, has_extra, n_tiles, keep_cols):
    if has_extra:
        src_ref, x_ref, g_ref, wt_ref, extra_ref, o_ref, hb_ref = refs
    else:
        src_ref, x_ref, g_ref, wt_ref, o_ref, hb_ref = refs
    j = pl.program_id(1)

    @pl.when(j == 0)
    def _():
        x = x_ref[...]
        inv = lax.rsqrt(jnp.mean(x * x, axis=-1, keepdims=True) + NORM_EPS)
        hb_ref[...] = (x * inv * g_ref[...]).astype(BF16)

    h = hb_ref[...]
    out = lax.dot_general(h, wt_ref[0].astype(BF16), (((1,), (1,)), ((), ())), preferred_element_type=F32)
    if has_extra:
        @pl.when(j < n_tiles - 1)
        def _():
            o_ref[...] = out

        @pl.when(j == n_tiles - 1)
        def _():
            col = lax.broadcasted_iota(jnp.int32, out.shape, 1)
            o_ref[...] = jnp.where(col < keep_cols, out, 0.0) + jnp.dot(h, extra_ref[...],
                                                                        preferred_element_type=F32)
    else:
        o_ref[...] = out


def _in_proj(x, gain, wt, layer, src_rows, extra, keep_cols, tm=2048, tn=512):
    m, k = x.shape
    n_tiles = len(src_rows)
    assert all(r % IN_PROJ_ROW_UNIT == 0 for r in src_rows)
    src = jnp.asarray([r // IN_PROJ_ROW_UNIT for r in src_rows], jnp.int32)
    element = lambda n: pl.Element(n)
    tm = min(tm, m)
    in_specs = [pl.BlockSpec((tm, k), lambda i, j, src: (i, 0), pipeline_mode=pl.Buffered(1)),
                pl.BlockSpec((1, k), lambda i, j, src: (0, 0)),
                pl.BlockSpec((element(1), element(tn), element(k)),
                             lambda i, j, src: (layer, src[j] * IN_PROJ_ROW_UNIT, 0))]
    args = [src, x, gain.reshape(1, k), wt]
    if extra is not None:
        in_specs.append(pl.BlockSpec((k, tn), lambda i, j, src: (0, 0), pipeline_mode=pl.Buffered(1)))
        args.append(extra)
    return pl.pallas_call(
        functools.partial(_in_proj_body, has_extra=extra is not None, n_tiles=n_tiles, keep_cols=keep_cols),
        grid_spec=pltpu.PrefetchScalarGridSpec(
            num_scalar_prefetch=1,
            grid=(m // tm, n_tiles),
            in_specs=in_specs,
            out_specs=pl.BlockSpec((tm, tn), lambda i, j, src: (i, j)),
            scratch_shapes=[pltpu.VMEM((tm, k), BF16)]),
        out_shape=jax.ShapeDtypeStruct((m, n_tiles * tn), F32),
        compiler_params=_cparams(2),
        name="in_proj",
    )(*args)


def _head_sum(x, sel, sel_t):
    s = _dot_lhs_split(x, sel)
    return _dot_lhs_split(s, sel_t)


def _token_shift(p, carry_ref, mu):
    rows = p.shape[0]
    row = lax.broadcasted_iota(jnp.int32, p.shape, 0)
    prev = jnp.where(row == 0, carry_ref[...], pltpu.roll(p, 1, axis=0))
    carry_ref[...] = p[rows - 1:rows, :]
    return p + (prev - p) * mu


def _prep_body(*refs, tiles_per_seq, has_vres, emit_v, width):
    (rkv_ref, lora_ref, mu_rkv_ref, mu_lora_ref, w0_ref, a0_ref, kk_ref, ka_ref, rk_ref,
     dup_ref, iup_ref, gup_ref, sel_ref, selt_ref, tri_ref, ones_ref) = refs[:16]
    pos = 16
    if has_vres:
        vfirst_ref, v0_ref, vup_ref = refs[pos:pos + 3]
        pos += 3
    at_o, rt_o, bt_o, kt_o, bp_o, kp_o, vb_o, wc_o, bonus_o, g_o = refs[pos:pos + 10]
    pos += 10
    if emit_v:
        v_o = refs[pos]
        pos += 1
    carry_rkv, carry_lora = refs[pos:]
    R = width

    @pl.when(pl.program_id(0) % tiles_per_seq == 0)
    def _():
        carry_rkv[...] = jnp.zeros_like(carry_rkv)
        carry_lora[...] = jnp.zeros_like(carry_lora)

    z = _token_shift(rkv_ref[...], carry_rkv, mu_rkv_ref[...])
    zl = _token_shift(lora_ref[...], carry_lora, mu_lora_ref[...])
    r, k, v = z[:, :R], z[:, R:2 * R], z[:, 2 * R:]

    decay_logit = w0_ref[...] + _dot_x3_presplit(jnp.tanh(zl[:, :128]), dup_ref[...])
    neg = -decay_logit
    softplus = jnp.maximum(neg, 0.0) + jnp.log(1.0 + jnp.exp(-jnp.abs(neg)))
    lw = -jnp.exp(-softplus - 0.5)
    a = _sigmoid(a0_ref[...] + _dot_x3_presplit(zl[:, :256], iup_ref[...]))
    g_o[...] = _dot(_sigmoid(zl[:, 128:]), gup_ref[...])
    if has_vres:
        mix = _sigmoid(v0_ref[...] + _dot(zl[:, 384:], vup_ref[...]))
        v = v + (vfirst_ref[...] - v) * mix
    if emit_v:
        v_o[...] = v

    sel, selt = sel_ref[...], selt_ref[...]
    kk = k * kk_ref[...]
    norm = jnp.sqrt(_head_sum(kk * kk, sel, selt))
    kk = kk / jnp.maximum(norm, 1e-12)
    k = k * (1.0 + (a - 1.0) * ka_ref[...])
    bonus_o[...] = _head_sum(r * k * rk_ref[...], sel, selt) * v

    d = functools.partial(jnp.dot, preferred_element_type=F32)
    lw_hi = lw.astype(BF16)
    rest = lw - lw_hi.astype(F32)
    lw_mid = rest.astype(BF16)
    lw_lo = (rest - lw_mid.astype(F32)).astype(BF16)
    tri, ones = tri_ref[...], ones_ref[...]
    cum = d(tri, lw_hi) + d(tri, lw_mid) + d(tri, lw_lo)
    total = d(ones, lw_hi) + d(ones, lw_mid) + d(ones, lw_lo)
    w_inv = jnp.exp(-cum)
    w_rem = jnp.exp(total - cum)
    b = kk * a
    wc_o[...] = jnp.exp(total)
    pairs = R // LANES
    for o_ref, val in ((at_o, -kk * jnp.exp(cum - lw)), (rt_o, r * jnp.exp(cum)), (bt_o, b * w_inv),
                       (kt_o, k * w_inv), (bp_o, b * w_rem), (kp_o, k * w_rem), (vb_o, v)):
        val = val.astype(BF16)
        for p in range(pairs):
            o_ref[0, p] = val[:, p * LANES:(p + 1) * LANES]


def _rwkv_prep(p_all, bsz, seq, width, consts, vres, emit_v):
    tm = PREP_ROWS
    m = p_all.shape[0]
    R = width
    lora_w = 512
    lora_blk = (p_all.shape[1] - lora_w) // lora_w
    tiles_per_seq = seq // tm
    row = lambda i: (i, 0)
    fixed = lambda i: (0, 0)
    in_specs = [pl.BlockSpec((tm, 3 * R), row),
                pl.BlockSpec((tm, lora_w), lambda i: (i, lora_blk))]
    args = [p_all, p_all]
    for c in consts:
        in_specs.append(pl.BlockSpec(c.shape, fixed))
        args.append(c)
    if vres is not None:
        vfirst, v0, vup = vres
        in_specs += [pl.BlockSpec((tm, R), row), pl.BlockSpec(v0.shape, fixed), pl.BlockSpec(vup.shape, fixed)]
        args += [vfirst, v0, vup]
    pairs = R // LANES
    pm_spec = pl.BlockSpec((1, pairs, tm, LANES), lambda i: (i // tiles_per_seq, 0, i % tiles_per_seq, 0))
    pm_shape = jax.ShapeDtypeStruct((bsz, pairs, seq, LANES), BF16)
    nat_spec = pl.BlockSpec((tm, R), row)
    nat_shape = jax.ShapeDtypeStruct((m, R), F32)
    n_nat = 4 if emit_v else 3
    return pl.pallas_call(
        functools.partial(_prep_body, tiles_per_seq=tiles_per_seq, has_vres=vres is not None, emit_v=emit_v,
                          width=R),
        grid=(m // tm,),
        in_specs=in_specs,
        out_specs=[pm_spec] * 7 + [nat_spec] * n_nat,
        out_shape=[pm_shape] * 7 + [nat_shape] * n_nat,
        scratch_shapes=[pltpu.VMEM((1, 3 * R), F32), pltpu.VMEM((1, lora_w), F32)],
        compiler_params=_cparams(1),
        name="rwkv_prep",
    )(*args)


def _wkv_body(at_ref, rt_ref, bt_ref, kt_ref, bp_ref, kp_ref, v_ref, wc_ref, y_ref, state_ref, *,
              batches, pairs, chunks):
    C = WKV_CHUNK
    hs = HEAD_SIZE

    @pl.when(pl.program_id(1) == 0)
    def _():
        state_ref[...] = jnp.zeros_like(state_ref)

    def head_masks(rows, width):
        lane = lax.broadcasted_iota(jnp.int32, (rows, width), 1)
        first = (lane % LANES) < hs
        return first, jnp.logical_not(first)

    def stack_heads(x):
        m0, m1 = head_masks(*x.shape)
        zero = jnp.zeros_like(x)
        return jnp.concatenate([jnp.where(m0, x, zero), jnp.where(m1, x, zero)], axis=0)

    row = lax.broadcasted_iota(jnp.int32, (C, LANES), 0)
    col = lax.broadcasted_iota(jnp.int32, (C, LANES), 1) % hs
    strict, incl = row > col, row >= col
    sq_r = lax.broadcasted_iota(jnp.int32, (LANES, LANES), 0)
    sq_c = lax.broadcasted_iota(jnp.int32, (LANES, LANES), 1)
    same_head = (sq_r // hs) == (sq_c // hs)
    eye = sq_r == sq_c
    levels = C.bit_length() - 1
    dot = functools.partial(jnp.dot, preferred_element_type=F32)
    P = range(batches * pairs)

    def step(c, carry):
        sl = pl.ds(pl.multiple_of(c * C, C), C)
        at = [at_ref[p // pairs, p % pairs, sl, :] for p in P]
        rt = [rt_ref[p // pairs, p % pairs, sl, :] for p in P]
        v = [v_ref[p // pairs, p % pairs, sl, :] for p in P]
        vs = [stack_heads(v[p]) for p in P]
        lhs = [jnp.concatenate([at[p], rt[p]], axis=0) for p in P]
        rhs = [jnp.concatenate([stack_heads(bt_ref[p // pairs, p % pairs, sl, :]), stack_heads(kt_ref[p // pairs, p % pairs, sl, :])], axis=0)
               for p in P]
        abk = [_dot_nt(lhs[p], rhs[p]) for p in P]
        n = [jnp.where(strict, abk[p][:C, :LANES], 0.0) for p in P]
        a_rb = [jnp.where(incl, abk[p][C:, :LANES], 0.0) for p in P]
        a_ak = [jnp.where(strict, abk[p][:C, LANES:], 0.0) for p in P]
        a_rk = [jnp.where(incl, abk[p][C:, LANES:], 0.0) for p in P]
        av = [dot(a_ak[p].astype(BF16), vs[p]) for p in P]
        x = [jnp.concatenate([at[p].astype(F32), av[p]], axis=1) for p in P]
        for lvl in range(levels):
            last = lvl + 1 == levels
            for p in P:
                z = x[p] if last else jnp.concatenate([n[p], x[p]], axis=1)
                out = dot(n[p].astype(BF16), stack_heads(z.astype(BF16)))
                if last:
                    x[p] = x[p] + out
                else:
                    n[p] = out[:, :LANES]
                    x[p] = x[p] + out[:, LANES:]
        xb = [x[p].astype(BF16) for p in P]
        zeros_c = jnp.zeros((C, LANES), BF16)
        bx = [_dot_tn(jnp.concatenate([bp_ref[p // pairs, p % pairs, sl, :], kp_ref[p // pairs, p % pairs, sl, :]], axis=0),
                      jnp.concatenate([xb[p], jnp.concatenate([zeros_c, v[p]], axis=1)], axis=0)) for p in P]
        zeros_2c = jnp.zeros((2 * C, LANES), BF16)
        rx = [dot(jnp.concatenate([a_rb[p], a_rk[p]], axis=1).astype(BF16),
                  jnp.concatenate([stack_heads(xb[p]), jnp.concatenate([zeros_2c, vs[p]], axis=1)], axis=0))
              for p in P]
        for p in P:
            state = state_ref[p]
            lanes = slice((p % pairs) * LANES, (p % pairs + 1) * LANES)
            wc = wc_ref[p // pairs, pl.ds(pl.multiple_of(c * C, C), 1), lanes]
            trans = jnp.where(same_head, bx[p][:, :LANES], 0.0) + jnp.where(eye, wc, 0.0)
            inject = jnp.where(same_head, bx[p][:, LANES:], 0.0)
            r_eff = rt[p].astype(F32) + rx[p][:, :LANES]
            y_ref[p // pairs, p % pairs, sl, :] = _dot(r_eff, state) + rx[p][:, LANES:]
            state_ref[p] = _dot_x3(trans, state) + inject
        return carry

    lax.fori_loop(0, chunks, step, 0)


def _wkv(at, rt, bt, kt, bp, kp, v, wc, pairs=8, span=512):
    bsz, n_pairs, seq, _ = at.shape
    spec = pl.BlockSpec((bsz, pairs, span, LANES), lambda g, t: (0, g, t, 0))
    wc_spec = pl.BlockSpec((bsz, span, pairs * LANES), lambda g, t: (0, t, g))
    return pl.pallas_call(
        functools.partial(_wkv_body, batches=bsz, pairs=pairs, chunks=span // WKV_CHUNK),
        grid=(n_pairs // pairs, seq // span),
        in_specs=[spec] * 7 + [wc_spec],
        out_specs=spec,
        out_shape=jax.ShapeDtypeStruct((bsz, n_pairs, seq, LANES), F32),
        scratch_shapes=[pltpu.VMEM((bsz * pairs, LANES, LANES), F32)],
        compiler_params=_cparams(2),
        name="wkv7",
    )(at, rt, bt, kt, bp, kp, v, wc.reshape(bsz, seq, wc.shape[1]))


def _mixer_outputs(y_ref, bonus_ref, g_ref, u_ref, lnw_ref, lnb_ref, sel_ref, selt_ref, pw_ref, ps_ref, ext_ref,
                   *, tiles_per_seq, tm, group_width):
    seq_tile = pl.program_id(0) % tiles_per_seq

    @pl.when(seq_tile == 0)
    def _():
        ext_ref[0:POOL_HALO, :] = jnp.zeros((POOL_HALO, ext_ref.shape[1]), F32)

    u = u_ref[...]
    ext_ref[POOL_HALO:POOL_HALO + tm, :] = u
    t_pos = seq_tile * tm + lax.broadcasted_iota(jnp.int32, (tm, group_width), 0)
    groups = []
    for gi, w in enumerate(POOL_WINDOWS):
        lanes = slice(gi * group_width, (gi + 1) * group_width)
        acc = u[:, lanes]
        for j in range(1, w):
            acc = acc + ext_ref[POOL_HALO - j:POOL_HALO - j + tm, lanes]
        count = jnp.minimum(t_pos + 1, w).astype(F32)
        mixed = acc / count - u[:, lanes]
        yg = _dot_x3(mixed, pw_ref[gi])
        groups.append((yg * ps_ref[:, lanes]).astype(BF16))
    ext_ref[0:POOL_HALO, :] = u[tm - POOL_HALO:, :]

    sel, selt = sel_ref[...], selt_ref[...]
    inv_n = 1.0 / HEAD_SIZE
    y = jnp.concatenate([y_ref[0, p] for p in range(y_ref.shape[1])], axis=1)
    mean = _head_sum(y, sel, selt) * inv_n
    yc = y - mean
    var = _head_sum(yc * yc, sel, selt) * inv_n
    yn = yc * lax.rsqrt(var + GN_EPS) * lnw_ref[...] + lnb_ref[...]
    ya = ((yn + bonus_ref[...]) * g_ref[...]).astype(BF16)
    return ya, jnp.concatenate(groups, axis=1)


def _post_mix_body(y_ref, bonus_ref, g_ref, u_ref, ga_ref, gb_ref, x_ref, lnw_ref, lnb_ref, sel_ref, selt_ref,
                   pw_ref, ps_ref, pa_ref, pb_ref, wo_ref, o_ref, ext_ref, **static):
    ya, yb = _mixer_outputs(y_ref, bonus_ref, g_ref, u_ref, lnw_ref, lnb_ref, sel_ref, selt_ref, pw_ref, ps_ref,
                            ext_ref, **static)
    ma = jnp.dot(ya, pa_ref[...], preferred_element_type=F32)
    mb = jnp.dot(yb, pb_ref[...], preferred_element_type=F32)
    merged = (_sigmoid(ga_ref[...]) * ma + _sigmoid(gb_ref[...]) * mb).astype(BF16)
    o_ref[...] = x_ref[...] + jnp.dot(merged, wo_ref[...], preferred_element_type=F32)


def _post_mix(y, bonus, g, p_all, pool_col, ga_col, gb_col, x, consts, weights, seq, tm=256):
    m, R = bonus.shape
    d = x.shape[1]
    lnw, lnb, sel, selt, pw, ps = consts
    tiles_per_seq = seq // tm
    row = lambda i: (i, 0)
    fixed2 = lambda i: (0, 0)
    resident = lambda w: pl.BlockSpec(w.shape, fixed2, pipeline_mode=pl.Buffered(1))
    in_specs = [pl.BlockSpec((1, y.shape[1], tm, LANES), lambda i: (i // tiles_per_seq, 0, i % tiles_per_seq, 0)),
                pl.BlockSpec((tm, R), row), pl.BlockSpec((tm, R), row),
                pl.BlockSpec((tm, R), lambda i: (i, pool_col // R)),
                pl.BlockSpec((tm, d), lambda i: (i, ga_col // d)),
                pl.BlockSpec((tm, d), lambda i: (i, gb_col // d)),
                pl.BlockSpec((tm, d), row)]
    in_specs += [pl.BlockSpec(c.shape, fixed2) for c in (lnw, lnb, sel, selt)]
    in_specs += [pl.BlockSpec(pw.shape, lambda i: (0, 0, 0)), pl.BlockSpec(ps.shape, fixed2)]
    in_specs += [resident(w) for w in weights]
    return pl.pallas_call(
        functools.partial(_post_mix_body, tiles_per_seq=tiles_per_seq, tm=tm, group_width=R // len(POOL_WINDOWS)),
        grid=(m // tm,),
        in_specs=in_specs,
        out_specs=pl.BlockSpec((tm, d), row),
        out_shape=jax.ShapeDtypeStruct((m, d), F32),
        scratch_shapes=[pltpu.VMEM((POOL_HALO + tm, R), F32)],
        compiler_params=_cparams(1),
        name="post_mix",
    )(y, bonus, g, p_all, p_all, p_all, x, lnw, lnb, sel, selt, pw, ps, *weights)


def _router_body(x_ref, g_ref, wr_ref, tri_ref, h_o, idx_o, gate_o, cnt_o, carry_ref):
    @pl.when(pl.program_id(0) == 0)
    def _():
        carry_ref[...] = jnp.zeros_like(carry_ref)

    x = x_ref[...]
    inv = lax.rsqrt(jnp.mean(x * x, axis=-1, keepdims=True) + NORM_EPS)
    h = x * inv * g_ref[...]
    h_o[...] = h.reshape(h_o.shape)
    logits = _dot_x3_presplit(h, wr_ref[...])
    lane = lax.broadcasted_iota(jnp.int32, logits.shape, 1)
    neg_inf = jnp.float32(-jnp.inf)
    l1 = jnp.where(lane < N_EXPERTS, logits, neg_inf)
    m1 = jnp.max(l1, axis=-1, keepdims=True)
    i1 = jnp.min(jnp.where(l1 == m1, lane, LANES), axis=-1, keepdims=True)
    l2 = jnp.where(lane == i1, neg_inf, l1)
    m2 = jnp.max(l2, axis=-1, keepdims=True)
    i2 = jnp.min(jnp.where(l2 == m2, lane, LANES), axis=-1, keepdims=True)
    e = jnp.exp(m2 - m1)
    g1 = 1.0 / (1.0 + e)
    gate_o[...] = jnp.where(lane == 0, g1, jnp.where(lane == 1, e * g1, 0.0))

    first, second = lane == i1, lane == i2
    picked = jnp.where(jnp.logical_or(first, second), 1.0, 0.0)
    before = carry_ref[...] + jnp.dot(tri_ref[...], picked.astype(BF16), preferred_element_type=F32)
    rank1 = jnp.sum(jnp.where(first, before, 0.0), axis=-1, keepdims=True).astype(jnp.int32)
    rank2 = jnp.sum(jnp.where(second, before, 0.0), axis=-1, keepdims=True).astype(jnp.int32)
    idx_o[...] = jnp.where(lane == 0, i1, jnp.where(lane == 1, i2, jnp.where(lane == 2, rank1,
                                                                              jnp.where(lane == 3, rank2, 0))))
    carry_ref[...] += jnp.sum(picked, axis=0, keepdims=True)
    cnt_o[...] = jnp.broadcast_to(carry_ref[...], cnt_o.shape).astype(jnp.int32)


def _router(x, g, wr_pad, tm=256):
    m, d = x.shape
    row = lambda i: (i, 0)
    t_idx = jnp.arange(tm, dtype=jnp.int32)
    tri_strict = (t_idx[None, :] < t_idx[:, None]).astype(BF16)
    return pl.pallas_call(
        _router_body,
        grid=(m // tm,),
        in_specs=[pl.BlockSpec((tm, d), row), pl.BlockSpec((1, d), lambda i: (0, 0)),
                  pl.BlockSpec(wr_pad.shape, lambda i: (0, 0)), pl.BlockSpec((tm, tm), lambda i: (0, 0))],
        out_specs=[pl.BlockSpec((tm, d // LANES, LANES), lambda i: (i, 0, 0)),
                   pl.BlockSpec((tm, LANES), row), pl.BlockSpec((tm, LANES), row),
                   pl.BlockSpec((8, LANES), lambda i: (0, 0))],
        out_shape=[jax.ShapeDtypeStruct((m, d // LANES, LANES), F32), jax.ShapeDtypeStruct((m, LANES), jnp.int32),
                   jax.ShapeDtypeStruct((m, LANES), F32), jax.ShapeDtypeStruct((8, LANES), jnp.int32)],
        scratch_shapes=[pltpu.VMEM((1, LANES), F32)],
        compiler_params=_cparams(1),
        name="router",
    )(x, g.reshape(1, d), wr_pad, tri_strict)


def _gather_body(tok_ref, used_ref, h_hbm, o_ref, buf_ref, sems, *, tg, n_steps):
    i = pl.program_id(0)

    def row_copy(step, r):
        slot = step % 2
        return pltpu.make_async_copy(h_hbm.at[pl.ds(tok_ref[step * tg + r], 1)], buf_ref.at[slot, pl.ds(r, 1)],
                                     sems.at[slot])

    def fetch(step):
        @pl.when(jnp.logical_and(step < n_steps, step * tg < used_ref[0]))
        def _():
            def start(q, c):
                for u in range(DMA_UNROLL):
                    row_copy(step, q * DMA_UNROLL + u).start(priority=u % 2)
                return c
            lax.fori_loop(0, tg // DMA_UNROLL, start, 0)

    @pl.when(i == 0)
    def _():
        fetch(i)

    fetch(i + 1)

    @pl.when(i * tg < used_ref[0])
    def _():
        pltpu.make_async_copy(h_hbm.at[pl.ds(0, tg)], buf_ref.at[i % 2], sems.at[i % 2]).wait()
        o_ref[...] = buf_ref[i % 2].reshape(o_ref.shape).astype(o_ref.dtype)

    @pl.when(i * tg >= used_ref[0])
    def _():
        o_ref[...] = jnp.zeros_like(o_ref)


def _gather_rows(tok, used_rows, h3, tg=256):
    p = tok.shape[0]
    _, nc, lanes = h3.shape
    n_steps = p // tg
    return pl.pallas_call(
        functools.partial(_gather_body, tg=tg, n_steps=n_steps),
        grid_spec=pltpu.PrefetchScalarGridSpec(
            num_scalar_prefetch=2,
            grid=(n_steps,),
            in_specs=[pl.BlockSpec(memory_space=pl.ANY)],
            out_specs=pl.BlockSpec((tg, nc * lanes), lambda i, tok, used: (i, 0)),
            scratch_shapes=[pltpu.VMEM((2, tg, nc, lanes), F32), pltpu.SemaphoreType.DMA((2,))]),
        out_shape=jax.ShapeDtypeStruct((p, nc * lanes), BF16),
        compiler_params=_cparams(1),
        name="moe_gather",
    )(tok, used_rows, h3)


def _experts_body(plan_ref, x_ref, wg_ref, wu_ref, wd_ref, o_ref, *, n_items):
    w = pl.program_id(0)
    j = pl.program_id(1)
    field = lambda f: plan_ref[f * n_items + w]
    lo, hi, zero, live = field(3), field(4), field(5), field(6)
    n_sub = MOE_ROWS // MOE_SUB

    @pl.when(jnp.logical_and(zero == 1, j == 0))
    def _():
        o_ref[...] = jnp.zeros_like(o_ref)

    def rows(sl, wg, wu, wd):
        x = x_ref[sl, :]
        g = jnp.dot(x, wg, preferred_element_type=F32)
        u = jnp.dot(x, wu, preferred_element_type=F32)
        h = (g * _sigmoid(g) * u).astype(BF16)
        o_ref[sl, :] += jnp.dot(h, wd, preferred_element_type=F32)

    for first in range(n_sub):
        for end in range(first + 1, n_sub + 1):
            @pl.when(jnp.logical_and(live == 1, jnp.logical_and(lo == first, hi == end)))
            def _():
                rows(slice(first * MOE_SUB, end * MOE_SUB),
                     wg_ref[...].astype(BF16), wu_ref[...].astype(BF16), wd_ref[...].astype(BF16))


def _experts(plan, xs, w_gate, w_up, w_down):
    p, d = xs.shape
    f = w_gate.shape[2]
    tn = MOE_FF_TILE
    nj = f // tn
    n_items = plan.shape[0] // PLAN_FIELDS
    fld = lambda plan, f_idx, w: plan[f_idx * n_items + w]
    col = lambda plan, w, j: jnp.where(fld(plan, 6, w) == 1, j, nj - 1)
    return pl.pallas_call(
        functools.partial(_experts_body, n_items=n_items),
        grid_spec=pltpu.PrefetchScalarGridSpec(
            num_scalar_prefetch=1,
            grid=(n_items, nj),
            in_specs=[pl.BlockSpec((MOE_ROWS, d), lambda w, j, plan: (fld(plan, 1, w), 0)),
                      pl.BlockSpec((None, d, tn), lambda w, j, plan: (fld(plan, 2, w), 0, col(plan, w, j))),
                      pl.BlockSpec((None, d, tn), lambda w, j, plan: (fld(plan, 2, w), 0, col(plan, w, j))),
                      pl.BlockSpec((None, tn, d), lambda w, j, plan: (fld(plan, 2, w), col(plan, w, j), 0))],
            out_specs=pl.BlockSpec((MOE_ROWS, d), lambda w, j, plan: (fld(plan, 0, w), 0))),
        out_shape=jax.ShapeDtypeStruct((p, d), F32),
        compiler_params=_cparams(2),
        name="moe_experts",
    )(plan, xs, w_gate, w_up, w_down)


def _combine_body(d0_ref, d1_ref, x_ref, gate_ref, g_ref, y_hbm, o_ref, buf_ref, sems, *, tg, n_steps, apply_norm):
    i = pl.program_id(0)

    def row_copy(step, r, d_ref, pick):
        slot = step % 2
        return pltpu.make_async_copy(y_hbm.at[pl.ds(d_ref[step * tg + r], 1), :],
                                     buf_ref.at[slot, pick, pl.ds(r, 1), :], sems.at[slot])

    def fetch(step):
        @pl.when(step < n_steps)
        def _():
            def start(q, c):
                for u in range(DMA_UNROLL):
                    row_copy(step, q * DMA_UNROLL + u, d0_ref, 0).start(priority=0)
                    row_copy(step, q * DMA_UNROLL + u, d1_ref, 1).start(priority=1)
                return c
            lax.fori_loop(0, tg // DMA_UNROLL, start, 0)

    @pl.when(i == 0)
    def _():
        fetch(i)

    fetch(i + 1)
    slot = i % 2
    for pick in range(2):
        pltpu.make_async_copy(y_hbm.at[pl.ds(0, tg), :], buf_ref.at[slot, pick], sems.at[slot]).wait()
    gate = gate_ref[...]
    x = x_ref[...] + gate[:, 0:1] * buf_ref[slot, 0] + gate[:, 1:2] * buf_ref[slot, 1]
    if apply_norm:
        x = x * lax.rsqrt(jnp.mean(x * x, axis=-1, keepdims=True) + NORM_EPS) * g_ref[...]
    o_ref[...] = x


def _combine(d0, d1, x, gate, g, ybuf, apply_norm, tg=256):
    m, d = x.shape
    row = lambda i, d0, d1: (i, 0)
    return pl.pallas_call(
        functools.partial(_combine_body, tg=tg, n_steps=m // tg, apply_norm=apply_norm),
        grid_spec=pltpu.PrefetchScalarGridSpec(
            num_scalar_prefetch=2,
            grid=(m // tg,),
            in_specs=[pl.BlockSpec((tg, d), row), pl.BlockSpec((tg, LANES), row),
                      pl.BlockSpec((1, d), lambda i, d0, d1: (0, 0)),
                      pl.BlockSpec(memory_space=pl.ANY)],
            out_specs=pl.BlockSpec((tg, d), row),
            scratch_shapes=[pltpu.VMEM((2, 2, tg, d), F32), pltpu.SemaphoreType.DMA((2,))]),
        out_shape=jax.ShapeDtypeStruct((m, d), F32),
        compiler_params=_cparams(1),
        name="moe_combine",
    )(d0, d1, x, gate, g.reshape(1, d), ybuf)


def _moe_plan(top_e, rank, counts):
    i32 = jnp.int32
    n_tok = top_e.shape[0]
    n_slot = n_tok * TOP_K
    flat_e = top_e.reshape(n_slot)
    rank = rank.reshape(n_slot)
    padded = (counts + MOE_SUB - 1) // MOE_SUB * MOE_SUB
    pend = jnp.cumsum(padded)
    pstart = pend - padded
    dest = pstart[flat_e] + rank
    n_blocks = -(-(n_slot + N_EXPERTS * MOE_SUB) // MOE_ROWS)
    tok = jnp.zeros((n_blocks * MOE_ROWS,), i32).at[dest].set(jnp.arange(n_slot, dtype=i32) // TOP_K)
    used_rows = pend[-1]
    used_blocks = (used_rows + MOE_ROWS - 1) // MOE_ROWS

    n_items = n_blocks + N_EXPERTS
    blk_lo = (jnp.arange(n_blocks, dtype=i32) * MOE_ROWS)[:, None]
    lo_row = jnp.maximum(pstart[None, :], blk_lo)
    hi_row = jnp.minimum(pend[None, :], blk_lo + MOE_ROWS)
    valid = (hi_row > lo_row).reshape(-1)
    n_live = jnp.sum(valid.astype(i32))
    pos = jnp.where(valid, jnp.cumsum(valid.astype(i32)) - 1, n_items)
    compact = lambda v: jnp.zeros((n_items,), i32).at[pos].set(v.reshape(-1).astype(i32), mode="drop")
    blk_c = compact(jnp.broadcast_to(jnp.arange(n_blocks, dtype=i32)[:, None], lo_row.shape))
    exp_c = compact(jnp.broadcast_to(jnp.arange(N_EXPERTS, dtype=i32)[None, :], lo_row.shape))
    lo_c = compact((lo_row - blk_lo) // MOE_SUB)
    hi_c = compact((hi_row - blk_lo) // MOE_SUB)
    first_c = compact(lo_row == blk_lo)

    w = jnp.arange(n_items, dtype=i32)
    live = w < n_live
    tail_blk = used_blocks + (w - n_live)
    last = n_live - 1
    out_blk = jnp.where(live, blk_c, jnp.minimum(tail_blk, n_blocks - 1))
    x_blk = jnp.where(live, blk_c, blk_c[last])
    expert = jnp.where(live, exp_c, exp_c[last])
    zero = jnp.where(live, first_c, (tail_blk < n_blocks).astype(i32))
    plan = jnp.concatenate([out_blk, x_blk, expert, lo_c, hi_c, zero, live.astype(i32)]).astype(i32)
    dest = dest.reshape(n_tok, TOP_K).astype(i32)
    return tok, used_rows.astype(i32).reshape(1), plan, dest[:, 0], dest[:, 1]


def kernel(x, norm_mix, w_in, shift_mu, decay_w0, decay_up, iclr_a0, iclr_up, outgate_up, k_k, k_a, r_k, lnx_w, lnx_b, vres_down, vres_mu, vres_up, vres_v0, pool_w, pool_scale, proj_a, proj_b, w_o, norm_ffn, ffn_gate, ffn_up, ffn_down, router, moe_gate, moe_up, moe_down, norm_final):
    bsz, seq, d = x.shape
    depth = w_in.shape[0]
    R = decay_w0.shape[1]
    dl, il, gl = decay_up.shape[1], iclr_up.shape[1], outgate_up.shape[1]
    vl = vres_up.shape[1]
    assert (dl, il, vl, gl) == (96, 96, 64, 256) and 3 * R + dl + il + gl == shift_mu.shape[1]
    m = bsz * seq
    x = x.reshape(m, d)
    row = lambda t: t.reshape(1, -1)

    head_of_lane = jnp.arange(R, dtype=jnp.int32) // HEAD_SIZE
    sel = (head_of_lane[:, None] == jnp.arange(LANES, dtype=jnp.int32)[None, :]).astype(BF16)
    sel_t = sel.T
    t_idx = jnp.arange(PREP_ROWS, dtype=jnp.int32)
    same_chunk = (t_idx[:, None] // WKV_CHUNK) == (t_idx[None, :] // WKV_CHUNK)
    ones_bd = same_chunk.astype(BF16)
    tri_bd = (same_chunk & (t_idx[None, :] <= t_idx[:, None])).astype(BF16)

    w_in_t = jnp.swapaxes(w_in, 1, 2)
    v_first = None
    for l in range(depth):
        c0 = 3 * R
        c3 = c0 + dl + il + gl
        tn = 512
        lora_w = c3 - c0 + vl
        assert c0 % tn == 0 and (w_in.shape[2] - c3) % tn == 0 and lora_w == tn
        src_rows = list(range(0, c0, tn)) + list(range(c3, w_in.shape[2], tn)) + [c0]
        vd_mu = vres_mu[l - 1] if l > 0 else jnp.zeros((vl,), F32)
        extra = None
        if l > 0:
            extra = jnp.concatenate([jnp.zeros((d, c3 - c0), F32), vres_down[l - 1]], axis=1).astype(BF16)
        mu = shift_mu[l]
        mu_rkv = row(mu[:c0])
        mu_lora = row(jnp.concatenate([mu[c0:c3], vd_mu]))
        pool_col = c0
        ga_col = c0 + R
        gb_col = ga_col + d
        zeros = lambda n: jnp.zeros((n, R), F32)
        dup = _hi_lo_rows(jnp.concatenate([decay_up[l], zeros(128 - dl)], axis=0))
        iup = _hi_lo_rows(jnp.concatenate([zeros(dl), iclr_up[l], zeros(256 - dl - il)], axis=0))
        gup = jnp.concatenate([zeros(dl + il - 128), outgate_up[l], zeros(vl)], axis=0).astype(BF16)

        p_all = _in_proj(x, norm_mix[l], w_in_t, l, src_rows, extra, c3 - c0, tn=tn)

        consts = [mu_rkv, mu_lora, row(decay_w0[l]), row(iclr_a0[l]), row(k_k[l]), row(k_a[l]), row(r_k[l]),
                  dup, iup, gup, sel, sel_t, tri_bd, ones_bd]
        vres = None
        if l > 0:
            vup = jnp.concatenate([zeros(128 - vl), vres_up[l - 1]], axis=0).astype(BF16)
            vres = (v_first, row(vres_v0[l - 1]), vup)
        emit_v = l == 0 and depth > 1
        outs = _rwkv_prep(p_all, bsz, seq, R, consts, vres, emit_v)
        wc, bonus, g = outs[7:10]
        if emit_v:
            v_first = outs[10]
        y = _wkv(*outs[:7], wc)

        assert pool_col % R == 0 and ga_col % d == 0 and gb_col % d == 0
        x = _post_mix(y, bonus, g, p_all, pool_col, ga_col, gb_col, x,
                      [row(lnx_w[l]), row(lnx_b[l]), sel, sel_t, pool_w[l], row(pool_scale[l])],
                      [proj_a[l].astype(BF16), proj_b[l].astype(BF16), w_o[l].astype(BF16)], seq)

        i = l // 2
        if l % 2 == 0:
            hidden = _norm_swiglu(x, norm_ffn[l], ffn_gate, ffn_up, i, tm=1024, tn=512)
            x = _matmul_residual(hidden, ffn_down, i, x, tm=512, tn=512)
            if l == depth - 1:
                x = _rmsnorm(x, norm_final, F32)
        else:
            wr_pad = _hi_lo_rows(jnp.concatenate([router[i], jnp.zeros((d, LANES - N_EXPERTS), F32)], axis=1))
            h3, idx, gate, counts = _router(x, norm_ffn[l], wr_pad)
            tok, used_rows, plan, d0, d1 = _moe_plan(idx[:, :TOP_K], idx[:, TOP_K:2 * TOP_K], counts[0, :N_EXPERTS])
            xs = _gather_rows(tok, used_rows, h3)
            ybuf = _experts(plan, xs, moe_gate[i], moe_up[i], moe_down[i])
            x = _combine(d0, d1, x, gate, norm_final, ybuf, apply_norm=l == depth - 1)
    return x.reshape(bsz, seq, d)
```
